```python
import math
import jax, jax.numpy as jnp
from jax import lax
import numpy as np

D_MODEL = 2048
BATCH = 8
SEQ = 8192
DEPTH = 4

N_A_LAYERS = DEPTH // 2
N_B_LAYERS = DEPTH - N_A_LAYERS
EPS = 1e-6
D_FF = ((8 * D_MODEL // 3 + 255) // 256) * 256
HEAD_DIM_A = 128
N_HEADS_A = D_MODEL // HEAD_DIM_A
DILATED_BRANCHES = ((128, 1), (512, 4), (2048, 16))
N_HEADS_B = D_MODEL // 128
QK_NOPE_DIM = 128
QK_ROPE_DIM = 64
V_HEAD_DIM = 128
KV_LORA_RANK = D_MODEL // 4
Q_LORA_RANK = D_MODEL // 4
ROPE_THETA = 10000.0
Q_BLOCK = 128

kernel_name = "yoco_dilated_swa_mla_macaron"


def rms_norm(x, g):
    xf = x.astype(jnp.float32)
    y = xf * lax.rsqrt(jnp.mean(xf * xf, axis=-1, keepdims=True) + EPS)
    return (y * g.astype(jnp.float32)).astype(x.dtype)


def swiglu(h, w_gate, w_up, w_down):
    return (jax.nn.silu(h @ w_gate) * (h @ w_up)) @ w_down


def alibi_slopes(n_heads):
    return jnp.asarray(2.0 ** (-8.0 * (np.arange(n_heads) + 1) / n_heads), dtype=jnp.float32)


def rope_tables(seq):
    inv = 1.0 / (ROPE_THETA ** (jnp.arange(0, QK_ROPE_DIM, 2, dtype=jnp.float32) / QK_ROPE_DIM))
    ang = jnp.arange(seq, dtype=jnp.float32)[:, None] * inv[None, :]
    return jnp.cos(ang), jnp.sin(ang)


def apply_rope(t, cos, sin):
    tf = t.astype(jnp.float32)
    t1, t2 = jnp.split(tf, 2, axis=-1)
    return jnp.concatenate([t1 * cos - t2 * sin, t1 * sin + t2 * cos], axis=-1).astype(t.dtype)


def dilated_branch(q, k, v, window, dilation, slopes):
    B, S, H, Dh = q.shape
    n = window // dilation
    span = n * dilation
    Sp = -(-S // span) * span
    nb = Sp // span

    def to_blocks(t):
        t = jnp.pad(t, ((0, 0), (0, Sp - S), (0, 0), (0, 0)))
        t = t.reshape(B, Sp // dilation, dilation, H, Dh).transpose(0, 2, 1, 3, 4)
        return t.reshape(B, dilation, nb, n, H, Dh)

    def with_prev(t):
        prev = jnp.pad(t[:, :, :-1], ((0, 0), (0, 0), (1, 0), (0, 0), (0, 0), (0, 0)))
        return jnp.concatenate([prev, t], axis=3)

    qb = to_blocks(q)
    kw = with_prev(to_blocks(k))
    vw = with_prev(to_blocks(v))
    s = jnp.einsum('brcihd,brcjhd->brchij', qb, kw, preferred_element_type=jnp.float32) * (Dh ** -0.5)
    i = jnp.arange(n)[:, None]
    j = jnp.arange(2 * n)[None, :]
    steps = n + i - j
    band = (steps >= 0) & (steps <= n)
    valid = band[None] & ((jnp.arange(nb)[:, None, None] > 0) | (j >= n)[None])
    bias = -slopes[:, None, None] * (dilation * steps).astype(jnp.float32)[None]
    s = jnp.where(valid[None, None, :, None], s + bias[None, None, None], -jnp.inf)
    m = jnp.max(s, axis=-1, keepdims=True)
    p = jnp.exp(s - m)
    l = jnp.sum(p, axis=-1, keepdims=True)
    o = jnp.einsum('brchij,brcjhd->brcihd', (p / l).astype(v.dtype), vw)
    lse = (m + jnp.log(l))[..., 0]
    o = o.reshape(B, dilation, Sp // dilation, H, Dh).transpose(0, 2, 1, 3, 4).reshape(B, Sp, H, Dh)[:, :S]
    lse = lse.transpose(0, 1, 2, 4, 3).reshape(B, dilation, Sp // dilation, H)
    lse = lse.transpose(0, 2, 1, 3).reshape(B, Sp, H)[:, :S]
    return o, lse


def dilated_attention(h, w_qkv, w_o, slopes):
    B, S, _ = h.shape
    qkv = (h @ w_qkv).reshape(B, S, 3, N_HEADS_A, HEAD_DIM_A)
    q, k, v = qkv[:, :, 0], qkv[:, :, 1], qkv[:, :, 2]
    outs, lses = [], []
    for window, dilation in DILATED_BRANCHES:
        o, lse = dilated_branch(q, k, v, window, dilation, slopes)
        outs.append(o)
        lses.append(lse)
    wts = jax.nn.softmax(jnp.stack(lses, axis=0), axis=0)
    o = jnp.einsum('gbsh,gbshd->bshd', wts.astype(q.dtype), jnp.stack(outs, axis=0))
    return o.reshape(B, S, N_HEADS_A * HEAD_DIM_A) @ w_o


def mla_shared_kv(x, kv_norm, b_wdkv, b_ckv_norm, b_wkr, b_wuk, b_wuv, cos, sin):
    h = rms_norm(x, kv_norm)
    c_kv = rms_norm(h @ b_wdkv, b_ckv_norm)
    k_nope = jnp.einsum('bsc,chd->bshd', c_kv, b_wuk)
    v = jnp.einsum('bsc,chd->bshd', c_kv, b_wuv)
    k_rope = apply_rope(h @ b_wkr, cos, sin)
    return k_nope, k_rope, v


def mla_attention(h, k_nope, k_rope, v, w_dq, cq_norm, w_uq, w_o, cos, sin):
    B, S, _ = h.shape
    c_q = rms_norm(h @ w_dq, cq_norm)
    q = jnp.einsum('bsc,chd->bshd', c_q, w_uq)
    q_nope = q[..., :QK_NOPE_DIM]
    q_rope = apply_rope(q[..., QK_NOPE_DIM:], cos[:, None, :], sin[:, None, :])
    nb = S // Q_BLOCK
    qn_b = q_nope.reshape(B, nb, Q_BLOCK, N_HEADS_B, QK_NOPE_DIM).transpose(1, 0, 2, 3, 4)
    qr_b = q_rope.reshape(B, nb, Q_BLOCK, N_HEADS_B, QK_ROPE_DIM).transpose(1, 0, 2, 3, 4)
    starts = jnp.arange(nb, dtype=jnp.int32) * Q_BLOCK
    scale = (QK_NOPE_DIM + QK_ROPE_DIM) ** -0.5
    kpos = jnp.arange(S, dtype=jnp.int32)

    def attend(args):
        qn, qr, start = args
        s = (jnp.einsum('bihd,bjhd->bhij', qn, k_nope, preferred_element_type=jnp.float32)
             + jnp.einsum('bihr,bjr->bhij', qr, k_rope, preferred_element_type=jnp.float32)) * scale
        qpos = start + jnp.arange(Q_BLOCK, dtype=jnp.int32)
        s = jnp.where(kpos[None, :] <= qpos[:, None], s, -jnp.inf)
        p = jax.nn.softmax(s, axis=-1)
        return jnp.einsum('bhij,bjhd->bihd', p.astype(v.dtype), v)

    o = lax.map(attend, (qn_b, qr_b, starts))
    o = o.transpose(1, 0, 2, 3, 4).reshape(B, S, N_HEADS_B * V_HEAD_DIM)
    return o @ w_o


def _fwd_setup_inputs(seed: int = 0) -> dict:
    key = jax.random.key(seed)
    ks = jax.random.split(key, 24)
    f32 = jnp.float32

    def w(k, shape, fan_in):
        return jax.random.normal(k, shape, f32) * (fan_in ** -0.5)

    def gain(k, shape):
        return 1.0 + 0.01 * jax.random.normal(k, shape, f32)

    D, F = D_MODEL, D_FF
    return {
        "x": jax.random.normal(ks[0], (BATCH, SEQ, D), f32),
        "ffn_norm1": gain(ks[1], (DEPTH, D)),
        "ffn1_wg": w(ks[2], (DEPTH, D, F), D),
        "ffn1_wu": w(ks[3], (DEPTH, D, F), D),
        "ffn1_wd": w(ks[4], (DEPTH, F, D), F),
        "mix_norm": gain(ks[5], (DEPTH, D)),
        "ffn_norm2": gain(ks[6], (DEPTH, D)),
        "ffn2_wg": w(ks[7], (DEPTH, D, F), D),
        "ffn2_wu": w(ks[8], (DEPTH, D, F), D),
        "ffn2_wd": w(ks[9], (DEPTH, F, D), F),
        "a_wqkv": w(ks[10], (N_A_LAYERS, D, 3 * N_HEADS_A * HEAD_DIM_A), D),
        "a_wo": w(ks[11], (N_A_LAYERS, N_HEADS_A * HEAD_DIM_A, D), N_HEADS_A * HEAD_DIM_A),
        "kv_norm": gain(ks[12], (D,)),
        "b_wdkv": w(ks[13], (D, KV_LORA_RANK), D),
        "b_ckv_norm": gain(ks[14], (KV_LORA_RANK,)),
        "b_wkr": w(ks[15], (D, QK_ROPE_DIM), D),
        "b_wuk": w(ks[16], (KV_LORA_RANK, N_HEADS_B, QK_NOPE_DIM), KV_LORA_RANK),
        "b_wuv": w(ks[17], (KV_LORA_RANK, N_HEADS_B, V_HEAD_DIM), KV_LORA_RANK),
        "b_wdq": w(ks[18], (N_B_LAYERS, D, Q_LORA_RANK), D),
        "b_cq_norm": gain(ks[19], (N_B_LAYERS, Q_LORA_RANK)),
        "b_wuq": w(ks[20], (N_B_LAYERS, Q_LORA_RANK, N_HEADS_B, QK_NOPE_DIM + QK_ROPE_DIM), Q_LORA_RANK),
        "b_wo": w(ks[21], (N_B_LAYERS, N_HEADS_B * V_HEAD_DIM, D), N_HEADS_B * V_HEAD_DIM),
        "final_norm": gain(ks[22], (D,)),
    }


def _fwd_reference(x, ffn_norm1, ffn1_wg, ffn1_wu, ffn1_wd, mix_norm, ffn_norm2, ffn2_wg, ffn2_wu, ffn2_wd,
              a_wqkv, a_wo, kv_norm, b_wdkv, b_ckv_norm, b_wkr, b_wuk, b_wuv,
              b_wdq, b_cq_norm, b_wuq, b_wo, final_norm):
    S = x.shape[1]
    slopes = alibi_slopes(N_HEADS_A)
    cos, sin = rope_tables(S)
    k_nope = k_rope = v_shared = None
    for layer in range(DEPTH):
        if layer == N_A_LAYERS:
            k_nope, k_rope, v_shared = mla_shared_kv(x, kv_norm, b_wdkv, b_ckv_norm, b_wkr, b_wuk, b_wuv, cos, sin)
        x = x + 0.5 * swiglu(rms_norm(x, ffn_norm1[layer]), ffn1_wg[layer], ffn1_wu[layer], ffn1_wd[layer])
        h = rms_norm(x, mix_norm[layer])
        if layer < N_A_LAYERS:
            x = x + dilated_attention(h, a_wqkv[layer], a_wo[layer], slopes)
        else:
            jb = layer - N_A_LAYERS
            x = x + mla_attention(h, k_nope, k_rope, v_shared, b_wdq[jb], b_cq_norm[jb], b_wuq[jb], b_wo[jb], cos, sin)
        x = x + 0.5 * swiglu(rms_norm(x, ffn_norm2[layer]), ffn2_wg[layer], ffn2_wu[layer], ffn2_wd[layer])
    return rms_norm(x, final_norm)


import jax as _jax
import jax.numpy as _jnp

TWIN_FORMAT = 'train_step'
FWD_PARAMS = ['x', 'ffn_norm1', 'ffn1_wg', 'ffn1_wu', 'ffn1_wd', 'mix_norm', 'ffn_norm2', 'ffn2_wg', 'ffn2_wu', 'ffn2_wd', 'a_wqkv', 'a_wo', 'kv_norm', 'b_wdkv', 'b_ckv_norm', 'b_wkr', 'b_wuk', 'b_wuv', 'b_wdq', 'b_cq_norm', 'b_wuq', 'b_wo', 'final_norm']
TWIN_WEIGHTS = ['ffn_norm1', 'ffn1_wg', 'ffn1_wu', 'ffn1_wd', 'mix_norm', 'ffn_norm2', 'ffn2_wg', 'ffn2_wu', 'ffn2_wd', 'a_wqkv', 'a_wo', 'kv_norm', 'b_wdkv', 'b_ckv_norm', 'b_wkr', 'b_wuk', 'b_wuv', 'b_wdq', 'b_cq_norm', 'b_wuq', 'b_wo', 'final_norm']
TWIN_DIFF_INPUT = 'x'
TWIN_INPUTS = ['x', 'ffn_norm1', 'ffn1_wg', 'ffn1_wu', 'ffn1_wd', 'mix_norm', 'ffn_norm2', 'ffn2_wg', 'ffn2_wu', 'ffn2_wd', 'a_wqkv', 'a_wo', 'kv_norm', 'b_wdkv', 'b_ckv_norm', 'b_wkr', 'b_wuk', 'b_wuv', 'b_wdq', 'b_cq_norm', 'b_wuq', 'b_wo', 'final_norm', 'loss_target', 'm_ffn_norm1', 'm_ffn1_wg', 'm_ffn1_wu', 'm_ffn1_wd', 'm_mix_norm', 'm_ffn_norm2', 'm_ffn2_wg', 'm_ffn2_wu', 'm_ffn2_wd', 'm_a_wqkv', 'm_a_wo', 'm_kv_norm', 'm_b_wdkv', 'm_b_ckv_norm', 'm_b_wkr', 'm_b_wuk', 'm_b_wuv', 'm_b_wdq', 'm_b_cq_norm', 'm_b_wuq', 'm_b_wo', 'm_final_norm', 'v_ffn_norm1', 'v_ffn1_wg', 'v_ffn1_wu', 'v_ffn1_wd', 'v_mix_norm', 'v_ffn_norm2', 'v_ffn2_wg', 'v_ffn2_wu', 'v_ffn2_wd', 'v_a_wqkv', 'v_a_wo', 'v_kv_norm', 'v_b_wdkv', 'v_b_ckv_norm', 'v_b_wkr', 'v_b_wuk', 'v_b_wuv', 'v_b_wdq', 'v_b_cq_norm', 'v_b_wuq', 'v_b_wo', 'v_final_norm']
TWIN_OUTPUTS = ['loss', 'grad_x', 'grad_ffn_norm1', 'grad_ffn1_wg', 'grad_ffn1_wu', 'grad_ffn1_wd', 'grad_mix_norm', 'grad_ffn_norm2', 'grad_ffn2_wg', 'grad_ffn2_wu', 'grad_ffn2_wd', 'grad_a_wqkv', 'grad_a_wo', 'grad_kv_norm', 'grad_b_wdkv', 'grad_b_ckv_norm', 'grad_b_wkr', 'grad_b_wuk', 'grad_b_wuv', 'grad_b_wdq', 'grad_b_cq_norm', 'grad_b_wuq', 'grad_b_wo', 'grad_final_norm', 'delta_ffn_norm1', 'delta_ffn1_wg', 'delta_ffn1_wu', 'delta_ffn1_wd', 'delta_mix_norm', 'delta_ffn_norm2', 'delta_ffn2_wg', 'delta_ffn2_wu', 'delta_ffn2_wd', 'delta_a_wqkv', 'delta_a_wo', 'delta_kv_norm', 'delta_b_wdkv', 'delta_b_ckv_norm', 'delta_b_wkr', 'delta_b_wuk', 'delta_b_wuv', 'delta_b_wdq', 'delta_b_cq_norm', 'delta_b_wuq', 'delta_b_wo', 'delta_final_norm', 'new_m_ffn_norm1', 'new_m_ffn1_wg', 'new_m_ffn1_wu', 'new_m_ffn1_wd', 'new_m_mix_norm', 'new_m_ffn_norm2', 'new_m_ffn2_wg', 'new_m_ffn2_wu', 'new_m_ffn2_wd', 'new_m_a_wqkv', 'new_m_a_wo', 'new_m_kv_norm', 'new_m_b_wdkv', 'new_m_b_ckv_norm', 'new_m_b_wkr', 'new_m_b_wuk', 'new_m_b_wuv', 'new_m_b_wdq', 'new_m_b_cq_norm', 'new_m_b_wuq', 'new_m_b_wo', 'new_m_final_norm', 'new_v_ffn_norm1', 'new_v_ffn1_wg', 'new_v_ffn1_wu', 'new_v_ffn1_wd', 'new_v_mix_norm', 'new_v_ffn_norm2', 'new_v_ffn2_wg', 'new_v_ffn2_wu', 'new_v_ffn2_wd', 'new_v_a_wqkv', 'new_v_a_wo', 'new_v_kv_norm', 'new_v_b_wdkv', 'new_v_b_ckv_norm', 'new_v_b_wkr', 'new_v_b_wuk', 'new_v_b_wuv', 'new_v_b_wdq', 'new_v_b_cq_norm', 'new_v_b_wuq', 'new_v_b_wo', 'new_v_final_norm']
TWIN_LEAF_KINDS = {'loss': 'loss', 'grad_x': 'grad_x', 'grad_ffn_norm1': 'grad_w', 'grad_ffn1_wg': 'grad_w', 'grad_ffn1_wu': 'grad_w', 'grad_ffn1_wd': 'grad_w', 'grad_mix_norm': 'grad_w', 'grad_ffn_norm2': 'grad_w', 'grad_ffn2_wg': 'grad_w', 'grad_ffn2_wu': 'grad_w', 'grad_ffn2_wd': 'grad_w', 'grad_a_wqkv': 'grad_w', 'grad_a_wo': 'grad_w', 'grad_kv_norm': 'grad_w', 'grad_b_wdkv': 'grad_w', 'grad_b_ckv_norm': 'grad_w', 'grad_b_wkr': 'grad_w', 'grad_b_wuk': 'grad_w', 'grad_b_wuv': 'grad_w', 'grad_b_wdq': 'grad_w', 'grad_b_cq_norm': 'grad_w', 'grad_b_wuq': 'grad_w', 'grad_b_wo': 'grad_w', 'grad_final_norm': 'grad_w', 'delta_ffn_norm1': 'delta_w', 'delta_ffn1_wg': 'delta_w', 'delta_ffn1_wu': 'delta_w', 'delta_ffn1_wd': 'delta_w', 'delta_mix_norm': 'delta_w', 'delta_ffn_norm2': 'delta_w', 'delta_ffn2_wg': 'delta_w', 'delta_ffn2_wu': 'delta_w', 'delta_ffn2_wd': 'delta_w', 'delta_a_wqkv': 'delta_w', 'delta_a_wo': 'delta_w', 'delta_kv_norm': 'delta_w', 'delta_b_wdkv': 'delta_w', 'delta_b_ckv_norm': 'delta_w', 'delta_b_wkr': 'delta_w', 'delta_b_wuk': 'delta_w', 'delta_b_wuv': 'delta_w', 'delta_b_wdq': 'delta_w', 'delta_b_cq_norm': 'delta_w', 'delta_b_wuq': 'delta_w', 'delta_b_wo': 'delta_w', 'delta_final_norm': 'delta_w', 'new_m_ffn_norm1': 'new_m', 'new_m_ffn1_wg': 'new_m', 'new_m_ffn1_wu': 'new_m', 'new_m_ffn1_wd': 'new_m', 'new_m_mix_norm': 'new_m', 'new_m_ffn_norm2': 'new_m', 'new_m_ffn2_wg': 'new_m', 'new_m_ffn2_wu': 'new_m', 'new_m_ffn2_wd': 'new_m', 'new_m_a_wqkv': 'new_m', 'new_m_a_wo': 'new_m', 'new_m_kv_norm': 'new_m', 'new_m_b_wdkv': 'new_m', 'new_m_b_ckv_norm': 'new_m', 'new_m_b_wkr': 'new_m', 'new_m_b_wuk': 'new_m', 'new_m_b_wuv': 'new_m', 'new_m_b_wdq': 'new_m', 'new_m_b_cq_norm': 'new_m', 'new_m_b_wuq': 'new_m', 'new_m_b_wo': 'new_m', 'new_m_final_norm': 'new_m', 'new_v_ffn_norm1': 'new_v', 'new_v_ffn1_wg': 'new_v', 'new_v_ffn1_wu': 'new_v', 'new_v_ffn1_wd': 'new_v', 'new_v_mix_norm': 'new_v', 'new_v_ffn_norm2': 'new_v', 'new_v_ffn2_wg': 'new_v', 'new_v_ffn2_wu': 'new_v', 'new_v_ffn2_wd': 'new_v', 'new_v_a_wqkv': 'new_v', 'new_v_a_wo': 'new_v', 'new_v_kv_norm': 'new_v', 'new_v_b_wdkv': 'new_v', 'new_v_b_ckv_norm': 'new_v', 'new_v_b_wkr': 'new_v', 'new_v_b_wuk': 'new_v', 'new_v_b_wuv': 'new_v', 'new_v_b_wdq': 'new_v', 'new_v_b_cq_norm': 'new_v', 'new_v_b_wuq': 'new_v', 'new_v_b_wo': 'new_v', 'new_v_final_norm': 'new_v'}


def _forward(args):
    return _fwd_reference(*[args[k] for k in FWD_PARAMS])


def _output_shape():
    def fwd():
        inp = _fwd_setup_inputs(0)
        return _fwd_reference(*[inp[k] for k in FWD_PARAMS])
    out = _jax.eval_shape(fwd)
    return out.shape, out.dtype

N_MICROBATCH = 1
ADAM_LR = 0.001
ADAM_B1 = 0.9
ADAM_B2 = 0.999
ADAM_EPS = 1e-08
ADAM_WD = 0.01
ADAM_STEP = 10
PER_EXAMPLE_BATCH_AXIS = {'x': 0, 'loss_target': 0}
SHARED_INPUTS = []
_WEIGHT_DTYPES = {'ffn_norm1': _jnp.float32, 'ffn1_wg': _jnp.float32, 'ffn1_wu': _jnp.float32, 'ffn1_wd': _jnp.float32, 'mix_norm': _jnp.float32, 'ffn_norm2': _jnp.float32, 'ffn2_wg': _jnp.float32, 'ffn2_wu': _jnp.float32, 'ffn2_wd': _jnp.float32, 'a_wqkv': _jnp.float32, 'a_wo': _jnp.float32, 'kv_norm': _jnp.float32, 'b_wdkv': _jnp.float32, 'b_ckv_norm': _jnp.float32, 'b_wkr': _jnp.float32, 'b_wuk': _jnp.float32, 'b_wuv': _jnp.float32, 'b_wdq': _jnp.float32, 'b_cq_norm': _jnp.float32, 'b_wuq': _jnp.float32, 'b_wo': _jnp.float32, 'final_norm': _jnp.float32}
MOMENT_SCALE = {'ffn_norm1': 5.502093e-02, 'ffn1_wg': 2.269829e-02, 'ffn1_wu': 2.198315e-02, 'ffn1_wd': 3.641915e-02, 'mix_norm': 5.244049e-02, 'ffn_norm2': 4.824514e-02, 'ffn2_wg': 2.019313e-02, 'ffn2_wu': 1.956939e-02, 'ffn2_wd': 3.243287e-02, 'a_wqkv': 4.232360e-02, 'a_wo': 5.616218e-02, 'kv_norm': 3.892107e-02, 'b_wdkv': 7.283447e-02, 'b_ckv_norm': 7.363303e-02, 'b_wkr': 6.902022e-02, 'b_wuk': 1.771447e-02, 'b_wuv': 3.244569e-02, 'b_wdq': 3.032014e-02, 'b_cq_norm': 2.946229e-02, 'b_wuq': 1.237811e-02, 'b_wo': 2.307698e-02, 'final_norm': 3.194504e+01}


def _to_microbatches(a, axis):
    t = _jnp.moveaxis(a, axis, 0)
    t = t.reshape((N_MICROBATCH, t.shape[0] // N_MICROBATCH) + t.shape[1:])
    return _jnp.moveaxis(t, 1, axis + 1)


def setup_inputs(seed: int = 0) -> dict:
    inp = _fwd_setup_inputs(seed)
    key = _jax.random.fold_in(_jax.random.key(seed), 7919)
    shape, _ = _output_shape()
    out = dict(inp)
    out["loss_target"] = _jax.random.normal(_jax.random.fold_in(key, 0), shape, _jnp.float32)
    for i, name in enumerate(TWIN_WEIGHTS):
        w = inp[name].astype(_jnp.float32)
        if MOMENT_SCALE is None:
            s = _jnp.sqrt(_jnp.mean(_jnp.square(w)) + 1e-30)
        else:
            s = MOMENT_SCALE[name]
        km, kv = _jax.random.split(_jax.random.fold_in(key, i + 1))
        out[name] = w
        out["m_" + name] = s * _jax.random.normal(km, w.shape, _jnp.float32)
        out["v_" + name] = (s * s) * _jax.random.uniform(kv, w.shape, _jnp.float32, 0.5, 1.5)
    if N_MICROBATCH > 1:
        for name, axis in PER_EXAMPLE_BATCH_AXIS.items():
            out[name] = _to_microbatches(out[name], axis)
    return {'x': out['x'], 'ffn_norm1': out['ffn_norm1'], 'ffn1_wg': out['ffn1_wg'], 'ffn1_wu': out['ffn1_wu'], 'ffn1_wd': out['ffn1_wd'], 'mix_norm': out['mix_norm'], 'ffn_norm2': out['ffn_norm2'], 'ffn2_wg': out['ffn2_wg'], 'ffn2_wu': out['ffn2_wu'], 'ffn2_wd': out['ffn2_wd'], 'a_wqkv': out['a_wqkv'], 'a_wo': out['a_wo'], 'kv_norm': out['kv_norm'], 'b_wdkv': out['b_wdkv'], 'b_ckv_norm': out['b_ckv_norm'], 'b_wkr': out['b_wkr'], 'b_wuk': out['b_wuk'], 'b_wuv': out['b_wuv'], 'b_wdq': out['b_wdq'], 'b_cq_norm': out['b_cq_norm'], 'b_wuq': out['b_wuq'], 'b_wo': out['b_wo'], 'final_norm': out['final_norm'], 'loss_target': out['loss_target'], 'm_ffn_norm1': out['m_ffn_norm1'], 'm_ffn1_wg': out['m_ffn1_wg'], 'm_ffn1_wu': out['m_ffn1_wu'], 'm_ffn1_wd': out['m_ffn1_wd'], 'm_mix_norm': out['m_mix_norm'], 'm_ffn_norm2': out['m_ffn_norm2'], 'm_ffn2_wg': out['m_ffn2_wg'], 'm_ffn2_wu': out['m_ffn2_wu'], 'm_ffn2_wd': out['m_ffn2_wd'], 'm_a_wqkv': out['m_a_wqkv'], 'm_a_wo': out['m_a_wo'], 'm_kv_norm': out['m_kv_norm'], 'm_b_wdkv': out['m_b_wdkv'], 'm_b_ckv_norm': out['m_b_ckv_norm'], 'm_b_wkr': out['m_b_wkr'], 'm_b_wuk': out['m_b_wuk'], 'm_b_wuv': out['m_b_wuv'], 'm_b_wdq': out['m_b_wdq'], 'm_b_cq_norm': out['m_b_cq_norm'], 'm_b_wuq': out['m_b_wuq'], 'm_b_wo': out['m_b_wo'], 'm_final_norm': out['m_final_norm'], 'v_ffn_norm1': out['v_ffn_norm1'], 'v_ffn1_wg': out['v_ffn1_wg'], 'v_ffn1_wu': out['v_ffn1_wu'], 'v_ffn1_wd': out['v_ffn1_wd'], 'v_mix_norm': out['v_mix_norm'], 'v_ffn_norm2': out['v_ffn_norm2'], 'v_ffn2_wg': out['v_ffn2_wg'], 'v_ffn2_wu': out['v_ffn2_wu'], 'v_ffn2_wd': out['v_ffn2_wd'], 'v_a_wqkv': out['v_a_wqkv'], 'v_a_wo': out['v_a_wo'], 'v_kv_norm': out['v_kv_norm'], 'v_b_wdkv': out['v_b_wdkv'], 'v_b_ckv_norm': out['v_b_ckv_norm'], 'v_b_wkr': out['v_b_wkr'], 'v_b_wuk': out['v_b_wuk'], 'v_b_wuv': out['v_b_wuv'], 'v_b_wdq': out['v_b_wdq'], 'v_b_cq_norm': out['v_b_cq_norm'], 'v_b_wuq': out['v_b_wuq'], 'v_b_wo': out['v_b_wo'], 'v_final_norm': out['v_final_norm']}


def _loss(weights, diff, rest, loss_target):
    with _jax.named_scope("forward"):
        args = {**rest, TWIN_DIFF_INPUT: diff, **{k: w.astype(_WEIGHT_DTYPES[k]) for k, w in weights.items()}}
        y = _forward(args)
    with _jax.named_scope("loss_head"):
        err = _jnp.square(y.astype(_jnp.float32) - loss_target)
        return 0.5 * _jnp.sum(_jnp.mean(err, axis=-1)) if err.ndim else 0.5 * err


def _adamw(w, g, m, v):
    m = ADAM_B1 * m + (1.0 - ADAM_B1) * g
    v = ADAM_B2 * v + (1.0 - ADAM_B2) * _jnp.square(g)
    m_hat = m / (1.0 - ADAM_B1 ** ADAM_STEP)
    v_hat = v / (1.0 - ADAM_B2 ** ADAM_STEP)
    delta = -ADAM_LR * (m_hat / (_jnp.sqrt(v_hat) + ADAM_EPS) + ADAM_WD * w)
    return delta, m, v


def reference(x, ffn_norm1, ffn1_wg, ffn1_wu, ffn1_wd, mix_norm, ffn_norm2, ffn2_wg, ffn2_wu, ffn2_wd, a_wqkv, a_wo, kv_norm, b_wdkv, b_ckv_norm, b_wkr, b_wuk, b_wuv, b_wdq, b_cq_norm, b_wuq, b_wo, final_norm, loss_target, m_ffn_norm1, m_ffn1_wg, m_ffn1_wu, m_ffn1_wd, m_mix_norm, m_ffn_norm2, m_ffn2_wg, m_ffn2_wu, m_ffn2_wd, m_a_wqkv, m_a_wo, m_kv_norm, m_b_wdkv, m_b_ckv_norm, m_b_wkr, m_b_wuk, m_b_wuv, m_b_wdq, m_b_cq_norm, m_b_wuq, m_b_wo, m_final_norm, v_ffn_norm1, v_ffn1_wg, v_ffn1_wu, v_ffn1_wd, v_mix_norm, v_ffn_norm2, v_ffn2_wg, v_ffn2_wu, v_ffn2_wd, v_a_wqkv, v_a_wo, v_kv_norm, v_b_wdkv, v_b_ckv_norm, v_b_wkr, v_b_wuk, v_b_wuv, v_b_wdq, v_b_cq_norm, v_b_wuq, v_b_wo, v_final_norm):
    given = dict(x=x, ffn_norm1=ffn_norm1, ffn1_wg=ffn1_wg, ffn1_wu=ffn1_wu, ffn1_wd=ffn1_wd, mix_norm=mix_norm, ffn_norm2=ffn_norm2, ffn2_wg=ffn2_wg, ffn2_wu=ffn2_wu, ffn2_wd=ffn2_wd, a_wqkv=a_wqkv, a_wo=a_wo, kv_norm=kv_norm, b_wdkv=b_wdkv, b_ckv_norm=b_ckv_norm, b_wkr=b_wkr, b_wuk=b_wuk, b_wuv=b_wuv, b_wdq=b_wdq, b_cq_norm=b_cq_norm, b_wuq=b_wuq, b_wo=b_wo, final_norm=final_norm, loss_target=loss_target, m_ffn_norm1=m_ffn_norm1, m_ffn1_wg=m_ffn1_wg, m_ffn1_wu=m_ffn1_wu, m_ffn1_wd=m_ffn1_wd, m_mix_norm=m_mix_norm, m_ffn_norm2=m_ffn_norm2, m_ffn2_wg=m_ffn2_wg, m_ffn2_wu=m_ffn2_wu, m_ffn2_wd=m_ffn2_wd, m_a_wqkv=m_a_wqkv, m_a_wo=m_a_wo, m_kv_norm=m_kv_norm, m_b_wdkv=m_b_wdkv, m_b_ckv_norm=m_b_ckv_norm, m_b_wkr=m_b_wkr, m_b_wuk=m_b_wuk, m_b_wuv=m_b_wuv, m_b_wdq=m_b_wdq, m_b_cq_norm=m_b_cq_norm, m_b_wuq=m_b_wuq, m_b_wo=m_b_wo, m_final_norm=m_final_norm, v_ffn_norm1=v_ffn_norm1, v_ffn1_wg=v_ffn1_wg, v_ffn1_wu=v_ffn1_wu, v_ffn1_wd=v_ffn1_wd, v_mix_norm=v_mix_norm, v_ffn_norm2=v_ffn_norm2, v_ffn2_wg=v_ffn2_wg, v_ffn2_wu=v_ffn2_wu, v_ffn2_wd=v_ffn2_wd, v_a_wqkv=v_a_wqkv, v_a_wo=v_a_wo, v_kv_norm=v_kv_norm, v_b_wdkv=v_b_wdkv, v_b_ckv_norm=v_b_ckv_norm, v_b_wkr=v_b_wkr, v_b_wuk=v_b_wuk, v_b_wuv=v_b_wuv, v_b_wdq=v_b_wdq, v_b_cq_norm=v_b_cq_norm, v_b_wuq=v_b_wuq, v_b_wo=v_b_wo, v_final_norm=v_final_norm)
    weights = {n: given[n] for n in TWIN_WEIGHTS}
    shared = {n: given[n] for n in SHARED_INPUTS}
    per_example = {n: given[n] for n in ['x']}
    grad_fn = _jax.value_and_grad(_loss, argnums=(0, 1))

    def one_microbatch(ex, loss_target):
        ex = dict(ex)
        diff = ex.pop(TWIN_DIFF_INPUT)
        return grad_fn(weights, diff, {**shared, **ex}, loss_target)

    if N_MICROBATCH == 1:
        loss, (grad_w, grad_x) = one_microbatch(per_example, given["loss_target"])
    else:
        def body(carry, xs):
            loss_sum, grad_sum = carry
            l_k, (gw_k, gx_k) = one_microbatch(xs[0], xs[1])
            with _jax.named_scope("update"):
                return (loss_sum + l_k, _jax.tree.map(_jnp.add, grad_sum, gw_k)), gx_k

        init = (_jnp.zeros((), _jnp.float32), _jax.tree.map(_jnp.zeros_like, weights))
        (loss, grad_w), grad_x = _jax.lax.scan(body, init, (per_example, given["loss_target"]))
    with _jax.named_scope("update"):
        delta_w, new_m, new_v = {}, {}, {}
        for n in TWIN_WEIGHTS:
            delta_w[n], new_m[n], new_v[n] = _adamw(weights[n], grad_w[n], given["m_" + n], given["v_" + n])
    return (loss, grad_x, *[grad_w[n] for n in TWIN_WEIGHTS], *[delta_w[n] for n in TWIN_WEIGHTS],
            *[new_m[n] for n in TWIN_WEIGHTS], *[new_v[n] for n in TWIN_WEIGHTS])
```

```python
import math

import numpy as np
import jax
import jax.numpy as jnp
from jax import lax
from jax.experimental import pallas as pl
from jax.experimental.pallas import tpu as pltpu

F32 = jnp.float32
BF16 = jnp.bfloat16
NORM_EPS = 1e-6
MASK_VALUE = -1e30
HEAD = 128
ROPE = 64
QK_PAD = 256
DILATED_BRANCHES = ((128, 1), (512, 4), (2048, 16))
ROPE_THETA = 10000.0
ADAM_LR, ADAM_B1, ADAM_B2, ADAM_EPS, ADAM_WD, ADAM_STEP = 0.001, 0.9, 0.999, 1e-08, 0.01, 10
VMEM_LIMIT_BYTES = 56 * 1024 * 1024
N_CHIPS = 4
MESH = pl.DeviceIdType.MESH


def _params(n_axes, vmem=VMEM_LIMIT_BYTES):
    return pltpu.CompilerParams(dimension_semantics=("arbitrary",) * n_axes, vmem_limit_bytes=vmem)


def _tile(n, pref, mult=128):
    if n <= pref:
        return n
    t = (pref // mult) * mult
    while t >= mult:
        if n % t == 0:
            return t
        t -= mult
    return n


def _mm_call(name, grid, ins, in_specs, out_shape, out_specs, n_a, n_b, terms, dims, acc_shapes, epilogue, into=None):
    n_in, n_out, nk = len(ins), len(out_shape), grid[2]
    aliases = {}
    if into is not None:
        ins = list(ins) + [into]
        in_specs = list(in_specs) + [pl.BlockSpec(memory_space=pl.ANY)]
        aliases = {n_in: 0}

    def body(*refs):
        in_refs, refs = refs[:n_in], refs[n_in + len(aliases):]
        out_refs, accs = refs[:n_out], refs[n_out:]
        a_refs, b_refs, extra = in_refs[:n_a], in_refs[n_a:n_a + n_b], in_refs[n_a + n_b:]
        k = pl.program_id(2)

        @pl.when(k == 0)
        def _():
            for acc in accs:
                acc[...] = jnp.zeros_like(acc)

        for t, ia, ib in terms:
            a = a_refs[ia][...].astype(BF16)
            b = b_refs[ib][...].astype(BF16)
            accs[t][...] += lax.dot_general(a, b, (dims, ((), ())), preferred_element_type=F32)

        @pl.when(k == nk - 1)
        def _():
            epilogue([acc[...] for acc in accs], extra, out_refs)

    return pl.pallas_call(
        body, name=name, grid=grid, in_specs=in_specs, out_specs=out_specs, out_shape=out_shape,
        scratch_shapes=[pltpu.VMEM(s, F32) for s in acc_shapes], input_output_aliases=aliases,
        compiler_params=_params(3),
    )(*ins)


def _b_spec_nn(w, tk, tn):
    b, layer = w
    if b.ndim == 3:
        return pl.BlockSpec((None, tk, tn), lambda i, j, k: (layer, k, j))
    per = b.shape[3] // tn
    return pl.BlockSpec((None, None, tk, tn), lambda i, j, k: (layer, j // per, k, j % per))


def _rope_tile(z, cs, s1, s2):
    return z * cs + pltpu.roll(z, 96, 1) * s1 + pltpu.roll(z, 32, 1) * s2


def mm_nn(name, a, w, *, out_dtype, res=None, alpha=1.0, rope=None, tm=512, tn=512, tk=512):
    m, kk = a.shape
    b = w[0]
    ns = b.shape[-1]
    n = b.shape[1] * ns if b.ndim == 4 else ns
    tm, tk = _tile(m, tm, 8), _tile(kk, tk)
    tn = _tile(ns, tn)
    if rope is not None and rope[0] == 'q':
        assert tn % QK_PAD == 0
    grid = (m // tm, n // tn, kk // tk)
    ins = [a, b]
    in_specs = [pl.BlockSpec((tm, tk), lambda i, j, k: (i, k)), _b_spec_nn(w, tk, tn)]
    if res is not None:
        ins.append(res)
        in_specs.append(pl.BlockSpec((tm, tn), lambda i, j, k: (i, j)))
    if rope is not None:
        ins += list(rope[1:])
        in_specs += [pl.BlockSpec((tm, HEAD), lambda i, j, k: (i, 0))] * 3

    def epilogue(accs, extra, outs):
        y = accs[0]
        extra = list(extra)
        if alpha != 1.0:
            y = alpha * y
        if res is not None:
            y = extra.pop(0)[...] + y
        if rope is not None:
            cs, s1, s2 = (e[...] for e in extra)
            step = QK_PAD if rope[0] == 'q' else HEAD
            for c0 in range(0, tn, step):
                if rope[0] == 'q':
                    outs[0][:, c0:c0 + HEAD] = y[:, c0:c0 + HEAD].astype(out_dtype)
                    c0 += HEAD
                outs[0][:, c0:c0 + HEAD] = _rope_tile(y[:, c0:c0 + HEAD], cs, s1, s2).astype(out_dtype)
        else:
            outs[0][...] = y.astype(out_dtype)

    return _mm_call(name, grid, ins, in_specs, [jax.ShapeDtypeStruct((m, n), out_dtype)],
                    [pl.BlockSpec((tm, tn), lambda i, j, k: (i, j))], 1, 1, [(0, 0, 0)], ((1,), (0,)),
                    [(tm, tn)], epilogue)[0]


def ffn_up(name, h, wg, wu, *, tm=512, tk=512):
    m, kk = h.shape
    fs = wg[0].shape[3]
    n = wg[0].shape[1] * fs
    tm, tk, tn = _tile(m, tm, 8), _tile(kk, tk), fs
    grid = (m // tm, n // tn, kk // tk)

    def epilogue(accs, extra, outs):
        g, u = accs
        outs[0][...] = g.astype(BF16)
        outs[1][...] = u.astype(BF16)
        outs[2][...] = (g * jax.nn.sigmoid(g) * u).astype(BF16)

    o_spec = pl.BlockSpec((tm, tn), lambda i, j, k: (i, j))
    return _mm_call(name, grid, [h, wg[0], wu[0]],
                    [pl.BlockSpec((tm, tk), lambda i, j, k: (i, k)), _b_spec_nn(wg, tk, tn), _b_spec_nn(wu, tk, tn)],
                    [jax.ShapeDtypeStruct((m, n), BF16)] * 3, [o_spec] * 3, 1, 2, [(0, 0, 0), (1, 0, 1)],
                    ((1,), (0,)), [(tm, tn)] * 2, epilogue)


def mm_nt(name, a_list, w_list, *, out_dtype, alpha=1.0, swiglu=None, res=None, tm=512, tn=512, tk=512):
    m, n = a_list[0].shape
    b0 = w_list[0][0]
    slab = b0.ndim == 4
    kk = b0.shape[-2]
    ns = b0.shape[-1]
    tm, tn = _tile(m, tm, 8), _tile(kk, tn)
    tk = ns if ns % 512 else _tile(ns, tk)
    grid = (m // tm, kk // tn, n // tk)

    def b_spec(w):
        layer = w[1]
        if slab:
            per = ns // tk
            return pl.BlockSpec((None, None, tn, tk), lambda i, j, k: (layer, k // per, j, k % per))
        return pl.BlockSpec((None, tn, tk), lambda i, j, k: (layer, j, k))

    p = len(a_list)
    ins = list(a_list) + [w[0] for w in w_list]
    in_specs = [pl.BlockSpec((tm, tk), lambda i, j, k: (i, k))] * p + [b_spec(w) for w in w_list]
    o_spec = pl.BlockSpec((tm, tn), lambda i, j, k: (i, j))
    if swiglu is not None:
        ins += list(swiglu)
        in_specs += [o_spec, o_spec]
        out_shape = [jax.ShapeDtypeStruct((m, kk), BF16)] * 2

        def epilogue(accs, extra, outs):
            da = alpha * accs[0]
            g = extra[0][...].astype(F32)
            u = extra[1][...].astype(F32)
            sg = jax.nn.sigmoid(g)
            silu = g * sg
            outs[0][...] = (da * u * (sg + silu * (1.0 - sg))).astype(BF16)
            outs[1][...] = (da * silu).astype(BF16)
    else:
        out_shape = [jax.ShapeDtypeStruct((m, kk), out_dtype)]
        if res is not None:
            ins.append(res)
            in_specs.append(o_spec)

        def epilogue(accs, extra, outs):
            y = alpha * accs[0]
            if res is not None:
                y = y + extra[0][...]
            outs[0][...] = y.astype(out_dtype)

    outs = _mm_call(name, grid, ins, in_specs, out_shape, [o_spec] * len(out_shape), p, p,
                    [(0, q, q) for q in range(p)], ((1,), (1,)), [(tm, tn)], epilogue)
    return outs if swiglu is not None else outs[0]


def mm_tn(name, a, b, into, *, alpha=1.0, tm=512, tn=512, tk=512):
    buf, layer = into
    m, kk = a.shape
    n = b.shape[1]
    ns = buf.shape[-1]
    tm, tk = _tile(kk, tm), _tile(m, tk, 8)
    tn = ns if ns % 512 else _tile(ns, tn)
    grid = (kk // tm, n // tn, m // tk)
    if buf.ndim == 4:
        per = ns // tn
        o_spec = pl.BlockSpec((None, None, tm, tn), lambda i, j, k: (layer, j // per, i, j % per))
    else:
        o_spec = pl.BlockSpec((None, tm, tn), lambda i, j, k: (layer, i, j))

    def epilogue(accs, extra, outs):
        outs[0][...] = (alpha * accs[0]).astype(buf.dtype)

    return _mm_call(name, grid, [a, b],
                    [pl.BlockSpec((tk, tm), lambda i, j, k: (k, i)), pl.BlockSpec((tk, tn), lambda i, j, k: (k, j))],
                    [jax.ShapeDtypeStruct(buf.shape, buf.dtype)], [o_spec], 1, 1, [(0, 0, 0)], ((0,), (0,)),
                    [(tm, tn)], epilogue, into=buf)[0]


def norm_fwd(name, x, gain, *, tm=256):
    m, d = x.shape
    tm = _tile(m, tm, 8)

    def body(x_ref, g_ref, o_ref):
        xv = x_ref[...]
        r = lax.rsqrt(jnp.mean(xv * xv, axis=-1, keepdims=True) + NORM_EPS)
        o_ref[...] = (xv * r * g_ref[...]).astype(BF16)

    return pl.pallas_call(
        body, name=name, grid=(m // tm,),
        in_specs=[pl.BlockSpec((tm, d), lambda i: (i, 0)), pl.BlockSpec((1, d), lambda i: (0, 0))],
        out_specs=pl.BlockSpec((tm, d), lambda i: (i, 0)), out_shape=jax.ShapeDtypeStruct((m, d), BF16),
        compiler_params=_params(1),
    )(x, gain)


def norm_bwd(name, x, gain, dh, dres=None, *, tm=256):
    m, d = x.shape
    tm = _tile(m, tm, 8)
    n_steps = m // tm
    has_res = dres is not None

    def body(*refs):
        if has_res:
            x_ref, g_ref, dh_ref, dres_ref, dx_ref, dxb_ref, dg_ref, acc = refs
        else:
            x_ref, g_ref, dh_ref, dx_ref, dxb_ref, dg_ref, acc = refs
        i = pl.program_id(0)
        xv = x_ref[...]
        dy = dh_ref[...].astype(F32)
        r = lax.rsqrt(jnp.mean(xv * xv, axis=-1, keepdims=True) + NORM_EPS)
        xhat = xv * r
        dxhat = dy * g_ref[...]
        dx = r * (dxhat - xhat * jnp.mean(dxhat * xhat, axis=-1, keepdims=True))
        if has_res:
            dx = dx + dres_ref[...]
        dx_ref[...] = dx
        dxb_ref[...] = dx.astype(BF16)

        @pl.when(i == 0)
        def _():
            acc[...] = jnp.zeros_like(acc)

        acc[...] += jnp.sum((dy * xhat).reshape(tm // 8, 8, d), axis=0)

        @pl.when(i == n_steps - 1)
        def _():
            dg_ref[...] = jnp.sum(acc[...], axis=0, keepdims=True)

    row = pl.BlockSpec((tm, d), lambda i: (i, 0))
    vec = pl.BlockSpec((1, d), lambda i: (0, 0))
    ins = [x, gain, dh] + ([dres] if has_res else [])
    return pl.pallas_call(
        body, name=name, grid=(n_steps,), in_specs=[row, vec, row] + ([row] if has_res else []),
        out_specs=[row, row, vec],
        out_shape=[jax.ShapeDtypeStruct((m, d), F32), jax.ShapeDtypeStruct((m, d), BF16), jax.ShapeDtypeStruct((1, d), F32)],
        scratch_shapes=[pltpu.VMEM((8, d), F32)], compiler_params=_params(1),
    )(*ins)


def loss_head(name, x, gain, target, *, tm=256):
    m, d = x.shape
    tm = _tile(m, tm, 8)
    n_steps = m // tm

    def body(x_ref, g_ref, t_ref, loss_ref, dx_ref, dxb_ref, dg_ref, acc, lacc):
        i = pl.program_id(0)
        xv = x_ref[...]
        g = g_ref[...]
        r = lax.rsqrt(jnp.mean(xv * xv, axis=-1, keepdims=True) + NORM_EPS)
        xhat = xv * r
        err = xhat * g - t_ref[...]
        dy = err * (1.0 / d)
        dxhat = dy * g
        dx = r * (dxhat - xhat * jnp.mean(dxhat * xhat, axis=-1, keepdims=True))
        dx_ref[...] = dx
        dxb_ref[...] = dx.astype(BF16)

        @pl.when(i == 0)
        def _():
            acc[...] = jnp.zeros_like(acc)
            lacc[...] = jnp.zeros_like(lacc)

        acc[...] += jnp.sum((dy * xhat).reshape(tm // 8, 8, d), axis=0)
        lacc[...] += jnp.sum((err * err).reshape(tm // 8, 8, d), axis=0)

        @pl.when(i == n_steps - 1)
        def _():
            dg_ref[...] = jnp.sum(acc[...], axis=0, keepdims=True)
            loss_ref[...] = (0.5 / d) * jnp.sum(jnp.sum(lacc[...], axis=0, keepdims=True), axis=1, keepdims=True)

    row = pl.BlockSpec((tm, d), lambda i: (i, 0))
    vec = pl.BlockSpec((1, d), lambda i: (0, 0))
    return pl.pallas_call(
        body, name=name, grid=(n_steps,), in_specs=[row, vec, row],
        out_specs=[pl.BlockSpec((1, 1), lambda i: (0, 0)), row, row, vec],
        out_shape=[jax.ShapeDtypeStruct((1, 1), F32), jax.ShapeDtypeStruct((m, d), F32),
                   jax.ShapeDtypeStruct((m, d), BF16), jax.ShapeDtypeStruct((1, d), F32)],
        scratch_shapes=[pltpu.VMEM((8, d), F32), pltpu.VMEM((8, d), F32)], compiler_params=_params(1),
    )(x, gain, target)


def _lane_is(h, shape):
    return lax.broadcasted_iota(jnp.int32, shape, 1) == h


def _take_lane(tile, h):
    return jnp.sum(jnp.where(_lane_is(h, tile.shape), tile, 0.0), axis=1, keepdims=True)


def _band_scores(q, kp, kc, h, c, dilation, n_heads):
    nt = (((1,), (1,)), ((), ()))
    scale = HEAD ** -0.5
    i = lax.broadcasted_iota(jnp.int32, (HEAD, HEAD), 0)
    j = lax.broadcasted_iota(jnp.int32, (HEAD, HEAD), 1)
    slope = jnp.exp(jnp.full((HEAD, HEAD), -8.0 * math.log(2.0) / n_heads, F32) * (h + 1).astype(F32))
    dist = (i - j).astype(F32) * float(dilation)
    s_c = lax.dot_general(q, kc, nt, preferred_element_type=F32) * scale - slope * dist
    s_p = lax.dot_general(q, kp, nt, preferred_element_type=F32) * scale - slope * (dist + float(dilation * HEAD))
    s_c = jnp.where(j <= i, s_c, MASK_VALUE)
    s_p = jnp.where((j >= i) & (c > 0), s_p, MASK_VALUE)
    return s_p, s_c


def dilated_fwd(name, qkv, state, dilation, n_heads, last):
    s, w3 = qkv.shape
    w = w3 // 3
    d = dilation
    rows, nb = s // d, s // (d * HEAD)
    first = state is None
    qkv_v = qkv.reshape(rows, d * w3)
    cols = w3 // HEAD

    def body(*refs):
        q_ref, kp_ref, kc_ref, vp_ref, vc_ref = refs[:5]
        refs = refs[5:]
        if not first:
            acc_in, m_in, l_in = refs[:3]
            refs = refs[3:]
        if last:
            o_ref, lse_ref = refs
        else:
            acc_out, m_out, l_out = refs
        c, h = pl.program_id(1), pl.program_id(2)
        s_p, s_c = _band_scores(q_ref[...], kp_ref[...], kc_ref[...], h, c, d, n_heads)
        m_new = jnp.maximum(jnp.max(s_c, axis=1, keepdims=True), jnp.max(s_p, axis=1, keepdims=True))
        if not first:
            m_prev = _take_lane(m_in[...], h)
            m_new = jnp.maximum(m_new, m_prev)
        p_c = jnp.exp(s_c - m_new)
        p_p = jnp.exp(s_p - m_new)
        l_new = jnp.sum(p_c, axis=1, keepdims=True) + jnp.sum(p_p, axis=1, keepdims=True)
        nn = (((1,), (0,)), ((), ()))
        acc = (lax.dot_general(p_c.astype(BF16), vc_ref[...], nn, preferred_element_type=F32)
               + lax.dot_general(p_p.astype(BF16), vp_ref[...], nn, preferred_element_type=F32))
        if not first:
            corr = jnp.exp(m_prev - m_new)
            l_new = l_new + corr * _take_lane(l_in[...], h)
            acc = acc + corr * acc_in[...]
        lane = _lane_is(h, (HEAD, HEAD))
        if last:
            o_ref[...] = (acc / l_new).astype(BF16)

            @pl.when(h == 0)
            def _():
                lse_ref[...] = jnp.zeros_like(lse_ref)

            lse_ref[...] = jnp.where(lane, m_new + jnp.log(l_new), lse_ref[...])
        else:
            acc_out[...] = acc

            @pl.when(h == 0)
            def _():
                m_out[...] = jnp.zeros_like(m_out)
                l_out[...] = jnp.zeros_like(l_out)

            m_out[...] = jnp.where(lane, m_new, m_out[...])
            l_out[...] = jnp.where(lane, l_new, l_out[...])

    blk = (HEAD, HEAD)
    prev = lambda c: jnp.maximum(c - 1, 0)
    in_specs = [
        pl.BlockSpec(blk, lambda r, c, h: (c, r * cols + h)),
        pl.BlockSpec(blk, lambda r, c, h: (prev(c), r * cols + n_heads + h)),
        pl.BlockSpec(blk, lambda r, c, h: (c, r * cols + n_heads + h)),
        pl.BlockSpec(blk, lambda r, c, h: (prev(c), r * cols + 2 * n_heads + h)),
        pl.BlockSpec(blk, lambda r, c, h: (c, r * cols + 2 * n_heads + h)),
    ]
    head_spec = pl.BlockSpec(blk, lambda r, c, h: (c, r * n_heads + h))
    stat_spec = pl.BlockSpec(blk, lambda r, c, h: (c, r))
    ins = [qkv_v] * 5
    if not first:
        acc0, m0, l0 = state
        ins += [acc0.reshape(rows, d * w), m0.reshape(rows, d * HEAD), l0.reshape(rows, d * HEAD)]
        in_specs += [head_spec, stat_spec, stat_spec]
    stat_shape = jax.ShapeDtypeStruct((rows, d * HEAD), F32)
    if last:
        out_shape = [jax.ShapeDtypeStruct((rows, d * w), BF16), stat_shape]
        out_specs = [head_spec, stat_spec]
    else:
        out_shape = [jax.ShapeDtypeStruct((rows, d * w), F32), stat_shape, stat_shape]
        out_specs = [head_spec, stat_spec, stat_spec]
    outs = pl.pallas_call(body, name=name, grid=(d, nb, n_heads), in_specs=in_specs, out_specs=out_specs,
                          out_shape=out_shape, compiler_params=_params(3))(*ins)
    if last:
        return outs[0].reshape(s, w), outs[1].reshape(s, HEAD)
    return outs[0].reshape(s, w), outs[1].reshape(s, HEAD), outs[2].reshape(s, HEAD)


def dilated_bwd(name, qkv, o, do, lse, dilation, n_heads):
    s, w3 = qkv.shape
    w = w3 // 3
    d = dilation
    rows, nb = s // d, s // (d * HEAD)
    cols = w3 // HEAD

    def body(q_ref, kp_ref, kc_ref, vp_ref, vc_ref, o_ref, do_ref, lse_ref, dq_ref, dk_ref, dv_ref, dk_c, dv_c):
        h, c = pl.program_id(1), pl.program_id(2)

        @pl.when(c == 0)
        def _():
            dk_c[...] = jnp.zeros_like(dk_c)
            dv_c[...] = jnp.zeros_like(dv_c)

        @pl.when(c < nb)
        def _():
            q, kp, kc, vp, vc, dout = q_ref[...], kp_ref[...], kc_ref[...], vp_ref[...], vc_ref[...], do_ref[...]
            s_p, s_c = _band_scores(q, kp, kc, h, c, d, n_heads)
            lse_h = _take_lane(lse_ref[...], h)
            delta = jnp.sum(dout.astype(F32) * o_ref[...].astype(F32), axis=1, keepdims=True)
            p_c = jnp.exp(s_c - lse_h)
            p_p = jnp.exp(s_p - lse_h)
            nt = (((1,), (1,)), ((), ()))
            nn = (((1,), (0,)), ((), ()))
            tn = (((0,), (0,)), ((), ()))
            scale = HEAD ** -0.5
            ds_c = (p_c * (lax.dot_general(dout, vc, nt, preferred_element_type=F32) - delta) * scale).astype(BF16)
            ds_p = (p_p * (lax.dot_general(dout, vp, nt, preferred_element_type=F32) - delta) * scale).astype(BF16)
            dq_ref[...] = (lax.dot_general(ds_c, kc, nn, preferred_element_type=F32)
                           + lax.dot_general(ds_p, kp, nn, preferred_element_type=F32)).astype(BF16)
            dk_ref[...] = (dk_c[...] + lax.dot_general(ds_p, q, tn, preferred_element_type=F32)).astype(BF16)
            dv_ref[...] = (dv_c[...] + lax.dot_general(p_p.astype(BF16), dout, tn, preferred_element_type=F32)).astype(BF16)
            dk_c[...] = lax.dot_general(ds_c, q, tn, preferred_element_type=F32)
            dv_c[...] = lax.dot_general(p_c.astype(BF16), dout, tn, preferred_element_type=F32)

        @pl.when(c == nb)
        def _():
            dk_ref[...] = dk_c[...].astype(BF16)
            dv_ref[...] = dv_c[...].astype(BF16)

    blk = (HEAD, HEAD)
    cur = lambda c: jnp.minimum(c, nb - 1)
    prev = lambda c: jnp.clip(c - 1, 0, nb - 1)
    in_specs = [
        pl.BlockSpec(blk, lambda r, h, c: (cur(c), r * cols + h)),
        pl.BlockSpec(blk, lambda r, h, c: (prev(c), r * cols + n_heads + h)),
        pl.BlockSpec(blk, lambda r, h, c: (cur(c), r * cols + n_heads + h)),
        pl.BlockSpec(blk, lambda r, h, c: (prev(c), r * cols + 2 * n_heads + h)),
        pl.BlockSpec(blk, lambda r, h, c: (cur(c), r * cols + 2 * n_heads + h)),
        pl.BlockSpec(blk, lambda r, h, c: (cur(c), r * n_heads + h)),
        pl.BlockSpec(blk, lambda r, h, c: (cur(c), r * n_heads + h)),
        pl.BlockSpec(blk, lambda r, h, c: (cur(c), r)),
    ]
    out_specs = [
        pl.BlockSpec(blk, lambda r, h, c: (cur(c), r * n_heads + h)),
        pl.BlockSpec(blk, lambda r, h, c: (prev(c), r * n_heads + h)),
        pl.BlockSpec(blk, lambda r, h, c: (prev(c), r * n_heads + h)),
    ]
    qkv_v = qkv.reshape(rows, d * w3)
    outs = pl.pallas_call(
        body, name=name, grid=(d, n_heads, nb + 1), in_specs=in_specs, out_specs=out_specs,
        out_shape=[jax.ShapeDtypeStruct((rows, d * w), BF16)] * 3,
        scratch_shapes=[pltpu.VMEM(blk, F32)] * 2, compiler_params=_params(3),
    )(qkv_v, qkv_v, qkv_v, qkv_v, qkv_v, o.reshape(rows, d * w), do.reshape(rows, d * w), lse.reshape(rows, d * HEAD))
    return [t.reshape(s, w) for t in outs]


def sum_branches(name, parts, *, tm=256):
    s, w = parts[0][0].shape
    tm = _tile(s, tm, 8)

    def body(*refs):
        out = refs[9]
        for t in range(3):
            acc = refs[t][...].astype(F32) + refs[3 + t][...].astype(F32) + refs[6 + t][...].astype(F32)
            out[:, t * w:(t + 1) * w] = acc.astype(BF16)

    row = pl.BlockSpec((tm, w), lambda i: (i, 0))
    return pl.pallas_call(
        body, name=name, grid=(s // tm,), in_specs=[row] * 9, out_specs=pl.BlockSpec((tm, 3 * w), lambda i: (i, 0)),
        out_shape=jax.ShapeDtypeStruct((s, 3 * w), BF16), compiler_params=_params(1),
    )(*[t for trip in parts for t in trip])


def _triangle(nq, n_heads, by_key):
    a, hh, b = [], [], []
    for outer in range(nq):
        for h in range(n_heads):
            inner = range(outer, nq) if by_key else range(outer + 1)
            for t in inner:
                a.append(outer)
                hh.append(h)
                b.append(t)
    return (jnp.asarray(np.array(a, np.int32)), jnp.asarray(np.array(hh, np.int32)), jnp.asarray(np.array(b, np.int32)))


def _causal_scores(q, kn, kr, qi, ki, t, scale):
    nt = (((1,), (1,)), ((), ()))
    k = jnp.concatenate([kn, kr], axis=1)
    s = lax.dot_general(q, k, nt, preferred_element_type=F32) * scale
    row = lax.broadcasted_iota(jnp.int32, (t, t), 0) + qi * t
    col = lax.broadcasted_iota(jnp.int32, (t, t), 1) + ki * t
    return jnp.where(col <= row, s, MASK_VALUE)


def mla_fwd(name, q, kn, kr, v, n_heads, scale, *, t=512):
    s = q.shape[0]
    t = _tile(s, t)
    nq = s // t
    qt, ht, kt = _triangle(nq, n_heads, by_key=False)

    def body(qt_ref, ht_ref, kt_ref, q_ref, kn_ref, kr_ref, v_ref, o_ref, lse_ref, m_sc, l_sc, acc_sc):
        step = pl.program_id(0)
        qi, h, ki = qt_ref[step], ht_ref[step], kt_ref[step]

        @pl.when(ki == 0)
        def _():
            m_sc[...] = jnp.full_like(m_sc, MASK_VALUE)
            l_sc[...] = jnp.zeros_like(l_sc)
            acc_sc[...] = jnp.zeros_like(acc_sc)

        sc = _causal_scores(q_ref[...], kn_ref[...], kr_ref[...], qi, ki, t, scale)
        m_prev = m_sc[...]
        m_new = jnp.maximum(m_prev, jnp.max(sc, axis=1, keepdims=True))
        p = jnp.exp(sc - m_new)
        corr = jnp.exp(m_prev - m_new)
        l_sc[...] = corr * l_sc[...] + jnp.sum(p, axis=1, keepdims=True)
        acc_sc[...] = corr * acc_sc[...] + lax.dot_general(p.astype(BF16), v_ref[...], (((1,), (0,)), ((), ())),
                                                           preferred_element_type=F32)
        m_sc[...] = m_new

        @pl.when(ki == qi)
        def _():
            o_ref[...] = (acc_sc[...] / l_sc[...]).astype(BF16)

            @pl.when(h == 0)
            def _():
                lse_ref[...] = jnp.zeros_like(lse_ref)

            lse_ref[...] = jnp.where(_lane_is(h, (t, HEAD)), m_sc[...] + jnp.log(l_sc[...]), lse_ref[...])

    grid_spec = pltpu.PrefetchScalarGridSpec(
        num_scalar_prefetch=3, grid=(int(qt.shape[0]),),
        in_specs=[
            pl.BlockSpec((t, QK_PAD), lambda i, a, b, c: (a[i], b[i])),
            pl.BlockSpec((t, HEAD), lambda i, a, b, c: (c[i], b[i])),
            pl.BlockSpec((t, HEAD), lambda i, a, b, c: (c[i], 0)),
            pl.BlockSpec((t, HEAD), lambda i, a, b, c: (c[i], b[i])),
        ],
        out_specs=[pl.BlockSpec((t, HEAD), lambda i, a, b, c: (a[i], b[i])),
                   pl.BlockSpec((t, HEAD), lambda i, a, b, c: (a[i], 0))],
        scratch_shapes=[pltpu.VMEM((t, 1), F32), pltpu.VMEM((t, 1), F32), pltpu.VMEM((t, HEAD), F32)],
    )
    return pl.pallas_call(
        body, name=name, grid_spec=grid_spec,
        out_shape=[jax.ShapeDtypeStruct((s, n_heads * HEAD), BF16), jax.ShapeDtypeStruct((s, HEAD), F32)],
        compiler_params=_params(1),
    )(qt, ht, kt, q, kn, kr, v)


def mla_bwd_dq(name, q, kn, kr, v, o, do, lse, tabs, n_heads, scale, *, t=512):
    s = q.shape[0]
    t = _tile(s, t)
    nq = s // t
    qt, ht, kt = _triangle(nq, n_heads, by_key=False)

    def body(qt_ref, ht_ref, kt_ref, q_ref, kn_ref, kr_ref, v_ref, o_ref, do_ref, lse_ref, cs_ref, s1_ref, s2_ref,
             dq_ref, acc_sc, delta_sc):
        step = pl.program_id(0)
        qi, h, ki = qt_ref[step], ht_ref[step], kt_ref[step]
        dout = do_ref[...]

        @pl.when(ki == 0)
        def _():
            acc_sc[...] = jnp.zeros_like(acc_sc)
            delta_sc[...] = jnp.sum(dout.astype(F32) * o_ref[...].astype(F32), axis=1, keepdims=True)

        kn_v, kr_v = kn_ref[...], kr_ref[...]
        sc = _causal_scores(q_ref[...], kn_v, kr_v, qi, ki, t, scale)
        p = jnp.exp(sc - _take_lane(lse_ref[...], h))
        dp = lax.dot_general(dout, v_ref[...], (((1,), (1,)), ((), ())), preferred_element_type=F32)
        ds = (p * (dp - delta_sc[...]) * scale).astype(BF16)
        k = jnp.concatenate([kn_v, kr_v], axis=1)
        acc_sc[...] += lax.dot_general(ds, k, (((1,), (0,)), ((), ())), preferred_element_type=F32)

        @pl.when(ki == qi)
        def _():
            dq = acc_sc[...]
            dq_ref[:, :HEAD] = dq[:, :HEAD].astype(BF16)
            dq_ref[:, HEAD:] = _rope_tile(dq[:, HEAD:], cs_ref[...], s1_ref[...], s2_ref[...]).astype(BF16)

    tab = pl.BlockSpec((t, HEAD), lambda i, a, b, c: (a[i], 0))
    grid_spec = pltpu.PrefetchScalarGridSpec(
        num_scalar_prefetch=3, grid=(int(qt.shape[0]),),
        in_specs=[
            pl.BlockSpec((t, QK_PAD), lambda i, a, b, c: (a[i], b[i])),
            pl.BlockSpec((t, HEAD), lambda i, a, b, c: (c[i], b[i])),
            pl.BlockSpec((t, HEAD), lambda i, a, b, c: (c[i], 0)),
            pl.BlockSpec((t, HEAD), lambda i, a, b, c: (c[i], b[i])),
            pl.BlockSpec((t, HEAD), lambda i, a, b, c: (a[i], b[i])),
            pl.BlockSpec((t, HEAD), lambda i, a, b, c: (a[i], b[i])),
            tab, tab, tab, tab,
        ],
        out_specs=pl.BlockSpec((t, QK_PAD), lambda i, a, b, c: (a[i], b[i])),
        scratch_shapes=[pltpu.VMEM((t, QK_PAD), F32), pltpu.VMEM((t, 1), F32)],
    )
    return pl.pallas_call(
        body, name=name, grid_spec=grid_spec, out_shape=jax.ShapeDtypeStruct((s, n_heads * QK_PAD), BF16),
        compiler_params=_params(1),
    )(qt, ht, kt, q, kn, kr, v, o, do, lse, *tabs)


def mla_bwd_dkv(name, q, kn, kr, v, o, do, lse, prev, n_heads, scale, *, t=512):
    s = q.shape[0]
    t = _tile(s, t)
    nq = s // t
    kt, ht, qt = _triangle(nq, n_heads, by_key=True)
    has_prev = prev is not None

    def body(*refs):
        kt_ref, ht_ref, qt_ref, q_ref, kn_ref, kr_ref, v_ref, o_ref, do_ref, lse_ref = refs[:10]
        refs = refs[10:]
        if has_prev:
            pkn_ref, pkr_ref, pv_ref = refs[:3]
            refs = refs[3:]
        dkn_ref, dkr_ref, dv_ref, dk_sc, dv_sc = refs
        step = pl.program_id(0)
        ki, h, qi = kt_ref[step], ht_ref[step], qt_ref[step]

        @pl.when(qi == ki)
        def _():
            dk_sc[...] = jnp.zeros_like(dk_sc)
            dv_sc[...] = jnp.zeros_like(dv_sc)

        q_v, dout = q_ref[...], do_ref[...]
        sc = _causal_scores(q_v, kn_ref[...], kr_ref[...], qi, ki, t, scale)
        p = jnp.exp(sc - _take_lane(lse_ref[...], h))
        delta = jnp.sum(dout.astype(F32) * o_ref[...].astype(F32), axis=1, keepdims=True)
        dp = lax.dot_general(dout, v_ref[...], (((1,), (1,)), ((), ())), preferred_element_type=F32)
        ds = (p * (dp - delta) * scale).astype(BF16)
        tn = (((0,), (0,)), ((), ()))
        dk_sc[...] += lax.dot_general(ds, q_v, tn, preferred_element_type=F32)
        dv_sc[...] += lax.dot_general(p.astype(BF16), dout, tn, preferred_element_type=F32)

        @pl.when(qi == nq - 1)
        def _():
            dk = dk_sc[...]
            dkn, dv = dk[:, :HEAD], dv_sc[...]
            if has_prev:
                dkn = dkn + pkn_ref[...].astype(F32)
                dv = dv + pv_ref[...].astype(F32)
            dkn_ref[...] = dkn.astype(BF16)
            dv_ref[...] = dv.astype(BF16)

            @pl.when(h == 0)
            def _():
                dkr_ref[...] = pkr_ref[...] if has_prev else jnp.zeros_like(dkr_ref)

            dkr_ref[...] += dk[:, HEAD:]

    head_q = pl.BlockSpec((t, HEAD), lambda i, a, b, c: (c[i], b[i]))
    head_k = pl.BlockSpec((t, HEAD), lambda i, a, b, c: (a[i], b[i]))
    shared_k = pl.BlockSpec((t, HEAD), lambda i, a, b, c: (a[i], 0))
    in_specs = [
        pl.BlockSpec((t, QK_PAD), lambda i, a, b, c: (c[i], b[i])), head_k, shared_k, head_k, head_q, head_q,
        pl.BlockSpec((t, HEAD), lambda i, a, b, c: (c[i], 0)),
    ]
    ins = [q, kn, kr, v, o, do, lse]
    if has_prev:
        in_specs += [head_k, shared_k, head_k]
        ins += list(prev)
    grid_spec = pltpu.PrefetchScalarGridSpec(
        num_scalar_prefetch=3, grid=(int(kt.shape[0]),), in_specs=in_specs, out_specs=[head_k, shared_k, head_k],
        scratch_shapes=[pltpu.VMEM((t, QK_PAD), F32), pltpu.VMEM((t, HEAD), F32)],
    )
    return pl.pallas_call(
        body, name=name, grid_spec=grid_spec,
        out_shape=[jax.ShapeDtypeStruct((s, n_heads * HEAD), BF16), jax.ShapeDtypeStruct((s, HEAD), F32),
                   jax.ShapeDtypeStruct((s, n_heads * HEAD), BF16)],
        compiler_params=_params(1),
    )(kt, ht, qt, *ins)


def rope_rows(name, z, tabs, *, tm=512):
    s = z.shape[0]
    tm = _tile(s, tm, 8)

    def body(z_ref, cs_ref, s1_ref, s2_ref, o_ref):
        o_ref[...] = _rope_tile(z_ref[...], cs_ref[...], s1_ref[...], s2_ref[...]).astype(BF16)

    row = pl.BlockSpec((tm, HEAD), lambda i: (i, 0))
    return pl.pallas_call(body, name=name, grid=(s // tm,), in_specs=[row] * 4, out_specs=row,
                          out_shape=jax.ShapeDtypeStruct((s, HEAD), BF16), compiler_params=_params(1))(z, *tabs)


def rope_tables(s):
    inv = 1.0 / (ROPE_THETA ** (jnp.arange(0, ROPE, 2, dtype=F32) / ROPE))
    ang = jnp.arange(s, dtype=F32)[:, None] * inv[None, :]
    cos, sin, zero = jnp.cos(ang), jnp.sin(ang), jnp.zeros((s, ROPE // 2), F32)
    pad = jnp.zeros((s, HEAD - ROPE), F32)
    cs = jnp.concatenate([cos, cos, pad], axis=1)
    lo = jnp.concatenate([sin, zero, pad], axis=1)
    hi = jnp.concatenate([zero, sin, pad], axis=1)
    return (cs, -lo, hi), (cs, lo, -hi)


def local_step(x, target, w, norms):
    s, d = x.shape
    depth = norms['ffn_norm1'].shape[0]
    n_a = w['a_wqkv'].shape[0]
    gw = {k: lax.empty(v.shape, BF16) for k, v in w.items()}
    n_heads = d // HEAD
    fwd_tabs, bwd_tabs = rope_tables(s)
    mla_scale = (HEAD + ROPE) ** -0.5
    gain = lambda name, i: norms[name][i:i + 1]
    saved = []
    kv = None

    def ffn_fwd(tag, x, g_norm, which, layer):
        wg, wu, wd = ((w[f'{which}_{nm}'], layer) for nm in ('wg', 'wu', 'wd'))
        h = norm_fwd(f'{tag}_norm', x, g_norm)
        g, u, a = ffn_up(f'{tag}_up', h, wg, wu)
        y = mm_nn(f'{tag}_down', a, wd, out_dtype=F32, res=x, alpha=0.5)
        return y, (x, h, g, u, a)

    def ffn_bwd(tag, dx, dxb, g_norm, which, layer, sv):
        wg, wu, wd = ((w[f'{which}_{nm}'], layer) for nm in ('wg', 'wu', 'wd'))
        x, h, g, u, a = sv
        dg, du = mm_nt(f'{tag}_bwd_da', [dxb], [wd], out_dtype=BF16, alpha=0.5, swiglu=(g, u))
        gw[f'{which}_wd'] = mm_tn(f'{tag}_bwd_dwd', a, dxb, (gw[f'{which}_wd'], layer), alpha=0.5)
        gw[f'{which}_wg'] = mm_tn(f'{tag}_bwd_dwg', h, dg, (gw[f'{which}_wg'], layer))
        gw[f'{which}_wu'] = mm_tn(f'{tag}_bwd_dwu', h, du, (gw[f'{which}_wu'], layer))
        dh = mm_nt(f'{tag}_bwd_dh', [dg, du], [wg, wu], out_dtype=F32)
        return norm_bwd(f'{tag}_bwd_norm', x, g_norm, dh, dx)

    for layer in range(depth):
        if layer == n_a:
            hk = norm_fwd('kv_norm', x, norms['kv_norm'])
            ckv_raw = mm_nn('kv_down', hk, (w['b_wdkv'], 0), out_dtype=F32)
            ckv = norm_fwd('kv_cnorm', ckv_raw, norms['b_ckv_norm'])
            k_nope = mm_nn('kv_uk', ckv, (w['b_wuk'], 0), out_dtype=BF16)
            v_shared = mm_nn('kv_uv', ckv, (w['b_wuv'], 0), out_dtype=BF16)
            k_rope = mm_nn('kv_kr', hk, (w['b_wkr'], 0), out_dtype=BF16, rope=('k',) + fwd_tabs, tn=HEAD)
            kv = (x, hk, ckv_raw, ckv, k_nope, k_rope, v_shared)
        x, sv1 = ffn_fwd(f'l{layer}_ffn1', x, gain('ffn_norm1', layer), 'ffn1', layer)
        h = norm_fwd(f'l{layer}_mix_norm', x, gain('mix_norm', layer))
        if layer < n_a:
            qkv = mm_nn(f'l{layer}_qkv', h, (w['a_wqkv'], layer), out_dtype=BF16)
            state = None
            for bi, (window, dilation) in enumerate(DILATED_BRANCHES):
                last = bi == len(DILATED_BRANCHES) - 1
                state = dilated_fwd(f'l{layer}_dil{dilation}', qkv, state, dilation, n_heads, last)
            o, lse = state
            x_new = mm_nn(f'l{layer}_wo', o, (w['a_wo'], layer), out_dtype=F32, res=x)
            svm = (x, h, qkv, o, lse)
        else:
            jb = layer - n_a
            cq_raw = mm_nn(f'l{layer}_dq', h, (w['b_wdq'], jb), out_dtype=F32)
            cq = norm_fwd(f'l{layer}_cq_norm', cq_raw, gain('b_cq_norm', jb))
            q = mm_nn(f'l{layer}_uq', cq, (w['b_wuq'], jb), out_dtype=BF16, rope=('q',) + fwd_tabs)
            o, lse = mla_fwd(f'l{layer}_mla', q, kv[4], kv[5], kv[6], n_heads, mla_scale)
            x_new = mm_nn(f'l{layer}_wo', o, (w['b_wo'], jb), out_dtype=F32, res=x)
            svm = (x, h, cq_raw, cq, q, o, lse)
        x = x_new
        x, sv2 = ffn_fwd(f'l{layer}_ffn2', x, gain('ffn_norm2', layer), 'ffn2', layer)
        saved.append((sv1, svm, sv2))

    loss, dx, dxb, d_final = loss_head('loss_head', x, norms['final_norm'], target)

    gn = {k: [None] * v.shape[0] for k, v in norms.items()}
    gn['final_norm'] = [d_final]
    dkv = None
    for layer in reversed(range(depth)):
        sv1, svm, sv2 = saved[layer]
        dx, dxb, gn['ffn_norm2'][layer] = ffn_bwd(f'l{layer}_ffn2', dx, dxb, gain('ffn_norm2', layer), 'ffn2', layer, sv2)
        if layer < n_a:
            xm, h, qkv, o, lse = svm
            do = mm_nt(f'l{layer}_bwd_do', [dxb], [(w['a_wo'], layer)], out_dtype=BF16)
            gw['a_wo'] = mm_tn(f'l{layer}_bwd_dwo', o, dxb, (gw['a_wo'], layer))
            parts = [dilated_bwd(f'l{layer}_bwd_dil{dilation}', qkv, o, do, lse, dilation, n_heads)
                     for _, dilation in DILATED_BRANCHES]
            dqkv = sum_branches(f'l{layer}_bwd_sum', parts)
            gw['a_wqkv'] = mm_tn(f'l{layer}_bwd_dwqkv', h, dqkv, (gw['a_wqkv'], layer))
            dh = mm_nt(f'l{layer}_bwd_dh', [dqkv], [(w['a_wqkv'], layer)], out_dtype=F32)
        else:
            jb = layer - n_a
            xm, h, cq_raw, cq, q, o, lse = svm
            do = mm_nt(f'l{layer}_bwd_do', [dxb], [(w['b_wo'], jb)], out_dtype=BF16)
            gw['b_wo'] = mm_tn(f'l{layer}_bwd_dwo', o, dxb, (gw['b_wo'], jb))
            dq = mla_bwd_dq(f'l{layer}_bwd_mla_dq', q, kv[4], kv[5], kv[6], o, do, lse, bwd_tabs, n_heads, mla_scale)
            dkv = mla_bwd_dkv(f'l{layer}_bwd_mla_dkv', q, kv[4], kv[5], kv[6], o, do, lse, dkv, n_heads, mla_scale)
            gw['b_wuq'] = mm_tn(f'l{layer}_bwd_dwuq', cq, dq, (gw['b_wuq'], jb))
            dcq = mm_nt(f'l{layer}_bwd_dcq', [dq], [(w['b_wuq'], jb)], out_dtype=F32)
            _, dcq_raw, gn['b_cq_norm'][jb] = norm_bwd(f'l{layer}_bwd_cq_norm', cq_raw, gain('b_cq_norm', jb), dcq)
            gw['b_wdq'] = mm_tn(f'l{layer}_bwd_dwdq', h, dcq_raw, (gw['b_wdq'], jb))
            dh = mm_nt(f'l{layer}_bwd_dh', [dcq_raw], [(w['b_wdq'], jb)], out_dtype=F32)
        dx, dxb, gn['mix_norm'][layer] = norm_bwd(f'l{layer}_bwd_mix_norm', xm, gain('mix_norm', layer), dh, dx)
        dx, dxb, gn['ffn_norm1'][layer] = ffn_bwd(f'l{layer}_ffn1', dx, dxb, gain('ffn_norm1', layer), 'ffn1', layer, sv1)
        if layer == n_a:
            xk, hk, ckv_raw, ckv, k_nope, k_rope, v_shared = kv
            dkn, dkr_rot, dv = dkv
            dkr = rope_rows('kv_bwd_rope', dkr_rot, bwd_tabs)
            gw['b_wuk'] = mm_tn('kv_bwd_dwuk', ckv, dkn, (gw['b_wuk'], 0))
            gw['b_wuv'] = mm_tn('kv_bwd_dwuv', ckv, dv, (gw['b_wuv'], 0))
            dckv = mm_nt('kv_bwd_dckv', [dkn, dv], [(w['b_wuk'], 0), (w['b_wuv'], 0)], out_dtype=F32)
            _, dckv_raw, g_ckv = norm_bwd('kv_bwd_cnorm', ckv_raw, norms['b_ckv_norm'], dckv)
            gw['b_wdkv'] = mm_tn('kv_bwd_dwdkv', hk, dckv_raw, (gw['b_wdkv'], 0))
            gw['b_wkr'] = mm_tn('kv_bwd_dwkr', hk, dkr, (gw['b_wkr'], 0))
            dhk = mm_nt('kv_bwd_dhk_c', [dckv_raw], [(w['b_wdkv'], 0)], out_dtype=F32)
            dhk = mm_nt('kv_bwd_dhk_r', [dkr], [(w['b_wkr'], 0)], out_dtype=F32, res=dhk)
            dx, dxb, g_kv = norm_bwd('kv_bwd_norm', xk, norms['kv_norm'], dhk, dx)
            gn['kv_norm'], gn['b_ckv_norm'] = [g_kv], [g_ckv]
    grads_n = {k: jnp.concatenate(v, axis=0) for k, v in gn.items()}
    return loss, dx, gw, grads_n


def _place():
    x, y, c = lax.axis_index("x"), lax.axis_index("y"), lax.axis_index("c")
    chips = [(1 - x, y), (x, 1 - y), (1 - x, 1 - y)]
    return x, y, c, 2 * x + y, chips, [2 * cx + cy for cx, cy in chips]


def _remote(src, dst, send_sem, recv_sem, to):
    return pltpu.make_async_remote_copy(src_ref=src, dst_ref=dst, send_sem=send_sem, recv_sem=recv_sem,
                                        device_id=to, device_id_type=MESH)


def cast_to_slab(name, shard, place, *, tr=256):
    l, r, c = shard.shape
    tr = _tile(r, tr, 16)

    def body(place_ref, x_ref, o_ref):
        o_ref[...] = x_ref[...].astype(BF16)

    grid_spec = pltpu.PrefetchScalarGridSpec(
        num_scalar_prefetch=1, grid=(l, r // tr),
        in_specs=[pl.BlockSpec((None, tr, c), lambda i, j, p: (i, j, 0))],
        out_specs=pl.BlockSpec((None, None, tr, c), lambda i, j, p: (i, p[0], j, 0)),
    )
    return pl.pallas_call(body, name=name, grid_spec=grid_spec,
                          out_shape=jax.ShapeDtypeStruct((l, N_CHIPS, r, c), BF16), compiler_params=_params(2))(place, shard)


def all_gather_weights(name, slabs):
    n = len(slabs)

    def body(*refs):
        outs = refs[n:2 * n]
        send_sems, recv_sems = refs[2 * n:]
        x, y, c, me, chips, chip_ids = _place()
        sibling = (x, y, 1 - c)

        def copy(t, k, chip, hc, to):
            rh = outs[t].shape[2] // 2
            blk = outs[t].at[:, chip, pl.ds(hc * rh, rh), :]
            return _remote(blk, blk, send_sems.at[6 * t + k], recv_sems.at[6 * t + k], to)

        first = [copy(t, k, me, c, (*chips[k], c)) for t in range(n) for k in range(3)]
        for cp in first:
            cp.start()
        passed = []
        for t in range(n):
            for k in range(3):
                copy(t, k, chip_ids[k], c, sibling).wait_recv()
                cp = copy(t, 3 + k, chip_ids[k], c, sibling)
                cp.start()
                passed.append(cp)
        for t in range(n):
            for k in range(3):
                copy(t, 3 + k, chip_ids[k], 1 - c, sibling).wait_recv()
        for cp in first + passed:
            cp.wait_send()

    any_spec = pl.BlockSpec(memory_space=pl.ANY)
    return pl.pallas_call(
        body, name=name, in_specs=[any_spec] * n, out_specs=[any_spec] * n,
        out_shape=[jax.ShapeDtypeStruct(a.shape, a.dtype) for a in slabs],
        scratch_shapes=[pltpu.SemaphoreType.DMA((6 * n,)), pltpu.SemaphoreType.DMA((6 * n,))],
        input_output_aliases={t: t for t in range(n)},
        compiler_params=pltpu.CompilerParams(has_side_effects=True),
    )(*slabs)


def exchange_halves(name, parts):
    n = len(parts)

    def body(*refs):
        ins, outs = refs[:n], refs[n:2 * n]
        send_sems, recv_sems = refs[2 * n:]
        x, y, c, me, chips, chip_ids = _place()
        copies = []
        for t in range(n):
            rh = ins[t].shape[2] // 2
            cp = _remote(ins[t].at[:, :, pl.ds((1 - c) * rh, rh), :], outs[t], send_sems.at[t], recv_sems.at[t],
                         (x, y, 1 - c))
            cp.start()
            copies.append(cp)
        for cp in copies:
            cp.wait()

    any_spec = pl.BlockSpec(memory_space=pl.ANY)
    return pl.pallas_call(
        body, name=name, in_specs=[any_spec] * n, out_specs=[any_spec] * n,
        out_shape=[jax.ShapeDtypeStruct((a.shape[0], a.shape[1], a.shape[2] // 2, a.shape[3]), a.dtype) for a in parts],
        scratch_shapes=[pltpu.SemaphoreType.DMA((n,)), pltpu.SemaphoreType.DMA((n,))],
        compiler_params=pltpu.CompilerParams(has_side_effects=True),
    )(*parts)


def add_halves(name, part, recv, place, *, tr=256):
    l, j, rh, c = recv.shape
    tr = _tile(rh, tr, 16)
    nr = rh // tr

    def body(place_ref, p_ref, r_ref, o_ref):
        o_ref[...] = (p_ref[...].astype(F32) + r_ref[...].astype(F32)).astype(BF16)

    blk = (None, None, tr, c)
    grid_spec = pltpu.PrefetchScalarGridSpec(
        num_scalar_prefetch=1, grid=(l, j, nr),
        in_specs=[pl.BlockSpec(blk, lambda a, b, i, p: (a, b, p[1] * nr + i, 0)),
                  pl.BlockSpec(blk, lambda a, b, i, p: (a, b, i, 0))],
        out_specs=pl.BlockSpec(blk, lambda a, b, i, p: (a, b, i, 0)),
    )
    return pl.pallas_call(body, name=name, grid_spec=grid_spec, out_shape=jax.ShapeDtypeStruct(recv.shape, BF16),
                          compiler_params=_params(3))(place, part, recv)


def scatter_to_chips(name, sums):
    n = len(sums)

    def body(*refs):
        ins, outs = refs[:n], refs[n:2 * n]
        send_sems, recv_sems = refs[2 * n:]
        x, y, c, me, chips, chip_ids = _place()
        copies = []
        for t in range(n):
            for k in range(3):
                cp = _remote(ins[t].at[:, chip_ids[k]], outs[t].at[k], send_sems.at[3 * t + k], recv_sems.at[3 * t + k],
                             (*chips[k], c))
                cp.start()
                copies.append(cp)
        for cp in copies:
            cp.wait()

    any_spec = pl.BlockSpec(memory_space=pl.ANY)
    return pl.pallas_call(
        body, name=name, in_specs=[any_spec] * n, out_specs=[any_spec] * n,
        out_shape=[jax.ShapeDtypeStruct((3, a.shape[0], a.shape[2], a.shape[3]), a.dtype) for a in sums],
        scratch_shapes=[pltpu.SemaphoreType.DMA((3 * n,)), pltpu.SemaphoreType.DMA((3 * n,))],
        compiler_params=pltpu.CompilerParams(has_side_effects=True),
    )(*sums)


def reduce_own(name, part, recv1, recv2, place, *, tr=256):
    l, j, rh, c = recv1.shape
    tr = _tile(rh, tr, 16)
    nr = rh // tr

    def body(place_ref, p_ref, r1_ref, a_ref, b_ref, c_ref, o_ref):
        acc = p_ref[...].astype(F32) + r1_ref[...].astype(F32)
        acc = acc + a_ref[...].astype(F32)
        acc = acc + b_ref[...].astype(F32)
        o_ref[...] = acc + c_ref[...].astype(F32)

    blk = (None, None, tr, c)
    grid_spec = pltpu.PrefetchScalarGridSpec(
        num_scalar_prefetch=1, grid=(l, nr),
        in_specs=[pl.BlockSpec(blk, lambda a, i, p: (a, p[0], p[1] * nr + i, 0)),
                  pl.BlockSpec(blk, lambda a, i, p: (a, p[0], i, 0)),
                  pl.BlockSpec(blk, lambda a, i, p: (0, a, i, 0)),
                  pl.BlockSpec(blk, lambda a, i, p: (1, a, i, 0)),
                  pl.BlockSpec(blk, lambda a, i, p: (2, a, i, 0))],
        out_specs=pl.BlockSpec((None, tr, c), lambda a, i, p: (a, p[1] * nr + i, 0)),
    )
    return pl.pallas_call(body, name=name, grid_spec=grid_spec, out_shape=jax.ShapeDtypeStruct((l, 2 * rh, c), F32),
                          compiler_params=_params(2))(place, part, recv1, recv2, recv2, recv2)


def share_halves(name, grads):
    n = len(grads)

    def body(*refs):
        outs = refs[n:2 * n]
        send_sems, recv_sems = refs[2 * n:]
        x, y, c, me, chips, chip_ids = _place()
        copies = []
        for t in range(n):
            rh = outs[t].shape[1] // 2
            mine = outs[t].at[:, pl.ds(c * rh, rh), :]
            cp = _remote(mine, mine, send_sems.at[t], recv_sems.at[t], (x, y, 1 - c))
            cp.start()
            copies.append(cp)
        for t, cp in enumerate(copies):
            rh = outs[t].shape[1] // 2
            theirs = outs[t].at[:, pl.ds((1 - c) * rh, rh), :]
            cp.wait_send()
            _remote(theirs, theirs, send_sems.at[t], recv_sems.at[t], (x, y, 1 - c)).wait_recv()

    any_spec = pl.BlockSpec(memory_space=pl.ANY)
    return pl.pallas_call(
        body, name=name, in_specs=[any_spec] * n, out_specs=[any_spec] * n,
        out_shape=[jax.ShapeDtypeStruct(a.shape, a.dtype) for a in grads],
        scratch_shapes=[pltpu.SemaphoreType.DMA((n,)), pltpu.SemaphoreType.DMA((n,))],
        input_output_aliases={t: t for t in range(n)},
        compiler_params=pltpu.CompilerParams(has_side_effects=True),
    )(*grads)


def all_reduce_small(name, packed):
    r = packed.shape[0]

    def body(x_ref, o_ref, buf, send_sems, recv_sems):
        x, y, c = lax.axis_index("x"), lax.axis_index("y"), lax.axis_index("c")
        me = 4 * x + 2 * y + c
        buf[me] = x_ref[...]
        copies = []
        for k in range(1, 8):
            to = (x ^ (k >> 2), y ^ ((k >> 1) & 1), c ^ (k & 1))
            cp = _remote(x_ref, buf.at[me], send_sems.at[k - 1], recv_sems.at[k - 1], to)
            cp.start()
            copies.append(cp)
        for k in range(1, 8):
            peer = me ^ k
            _remote(x_ref, buf.at[peer], send_sems.at[k - 1], recv_sems.at[k - 1], (x, y, c)).wait_recv()
        for cp in copies:
            cp.wait_send()
        acc = buf[0]
        for dev in range(1, 8):
            acc = acc + buf[dev]
        o_ref[...] = acc

    vmem = pl.BlockSpec(memory_space=pltpu.VMEM)
    return pl.pallas_call(
        body, name=name, in_specs=[vmem], out_specs=vmem, out_shape=jax.ShapeDtypeStruct(packed.shape, F32),
        scratch_shapes=[pltpu.VMEM((8, r, HEAD), F32), pltpu.SemaphoreType.DMA((7,)), pltpu.SemaphoreType.DMA((7,))],
    )(packed)


def adamw(name, w, g, m, v, *, tr=256):
    l, r, c = w.shape
    tr = _tile(r, tr, 8)

    def body(w_ref, g_ref, m_ref, v_ref, d_ref, nm_ref, nv_ref):
        gv = g_ref[...]
        nm = ADAM_B1 * m_ref[...] + (1.0 - ADAM_B1) * gv
        nv = ADAM_B2 * v_ref[...] + (1.0 - ADAM_B2) * (gv * gv)
        m_hat = nm / (1.0 - ADAM_B1 ** ADAM_STEP)
        v_hat = nv / (1.0 - ADAM_B2 ** ADAM_STEP)
        d_ref[...] = -ADAM_LR * (m_hat / (jnp.sqrt(v_hat) + ADAM_EPS) + ADAM_WD * w_ref[...])
        nm_ref[...] = nm
        nv_ref[...] = nv

    blk = pl.BlockSpec((None, tr, c), lambda i, j: (i, j, 0))
    shape = jax.ShapeDtypeStruct(w.shape, F32)
    return pl.pallas_call(body, name=name, grid=(l, r // tr), in_specs=[blk] * 4, out_specs=[blk] * 3,
                          out_shape=[shape] * 3, compiler_params=_params(2))(w, g, m, v)


SHARDED = ('ffn1_wg', 'ffn1_wu', 'ffn1_wd', 'ffn2_wg', 'ffn2_wu', 'ffn2_wd', 'a_wqkv', 'a_wo', 'b_wdkv', 'b_wkr', 'b_wuk',
           'b_wuv', 'b_wdq', 'b_wuq', 'b_wo')
COLUMN_SHARDED = ('ffn1_wg', 'ffn1_wu', 'ffn2_wg', 'ffn2_wu', 'a_wqkv')
GAINS = ('ffn_norm1', 'mix_norm', 'ffn_norm2', 'kv_norm', 'b_ckv_norm', 'b_cq_norm', 'final_norm')
WEIGHTS = ('ffn_norm1', 'ffn1_wg', 'ffn1_wu', 'ffn1_wd', 'mix_norm', 'ffn_norm2', 'ffn2_wg', 'ffn2_wu', 'ffn2_wd', 'a_wqkv',
           'a_wo', 'kv_norm', 'b_wdkv', 'b_ckv_norm', 'b_wkr', 'b_wuk', 'b_wuv', 'b_wdq', 'b_cq_norm', 'b_wuq', 'b_wo',
           'final_norm')


def _to_comm_shape(name, a):
    if name == 'b_wkr':
        return jnp.pad(a, ((0, 0), (0, HEAD - ROPE)))[None]
    if name == 'b_wuq':
        l, r, h, _ = a.shape
        return jnp.pad(a, ((0, 0), (0, 0), (0, 0), (0, QK_PAD - HEAD - ROPE))).reshape(l, r, h * QK_PAD)
    if name in ('b_wuk', 'b_wuv'):
        return a.reshape(1, a.shape[0], -1)
    return a[None] if a.ndim == 2 else a


def _from_comm_shape(name, g, like):
    if name == 'b_wkr':
        return g[0, :, :ROPE]
    if name == 'b_wuq':
        l, r, h, e = like.shape
        return g.reshape(l, r, h, QK_PAD)[..., :e]
    return g.reshape(like.shape)


def _as3d(a):
    if a.ndim == 1:
        return a.reshape(1, 1, -1)
    if a.ndim == 2:
        return a[None]
    return a.reshape(a.shape[0], a.shape[1], -1)


def kernel(x, ffn_norm1, ffn1_wg, ffn1_wu, ffn1_wd, mix_norm, ffn_norm2, ffn2_wg, ffn2_wu, ffn2_wd, a_wqkv, a_wo, kv_norm, b_wdkv, b_ckv_norm, b_wkr, b_wuk, b_wuv, b_wdq, b_cq_norm, b_wuq, b_wo, final_norm, loss_target, m_ffn_norm1, m_ffn1_wg, m_ffn1_wu, m_ffn1_wd, m_mix_norm, m_ffn_norm2, m_ffn2_wg, m_ffn2_wu, m_ffn2_wd, m_a_wqkv, m_a_wo, m_kv_norm, m_b_wdkv, m_b_ckv_norm, m_b_wkr, m_b_wuk, m_b_wuv, m_b_wdq, m_b_cq_norm, m_b_wuq, m_b_wo, m_final_norm, v_ffn_norm1, v_ffn1_wg, v_ffn1_wu, v_ffn1_wd, v_mix_norm, v_ffn_norm2, v_ffn2_wg, v_ffn2_wu, v_ffn2_wd, v_a_wqkv, v_a_wo, v_kv_norm, v_b_wdkv, v_b_ckv_norm, v_b_wkr, v_b_wuk, v_b_wuv, v_b_wdq, v_b_cq_norm, v_b_wuq, v_b_wo, v_final_norm):
    args = dict(locals())
    weights = {k: args[k] for k in WEIGHTS}
    place = jnp.stack([2 * lax.axis_index("x") + lax.axis_index("y"), lax.axis_index("c")]).astype(jnp.int32)

    slabs = [cast_to_slab(f'cast_{k}', _to_comm_shape(k, weights[k]), place) for k in SHARDED]
    gathered = all_gather_weights('gather_weights', slabs)
    w = {}
    for k, g in zip(SHARDED, gathered):
        l, j, r, c = g.shape
        w[k] = g if k in COLUMN_SHARDED else g.reshape(l, j * r, c)
    norms = {k: weights[k] if weights[k].ndim == 2 else weights[k][None] for k in GAINS}

    loss, grad_x, gw, gn = local_step(x[0], loss_target[0], w, norms)
    loss = lax.psum(loss[0, 0], ("x", "y", "c"))

    parts = [gw[k].reshape(g.shape) for k, g in zip(SHARDED, gathered)]
    recv1 = exchange_halves('grads_to_sibling', parts)
    sums = [add_halves(f'add_{k}', p, r1, place) for k, p, r1 in zip(SHARDED, parts, recv1)]
    recv2 = scatter_to_chips('grads_to_chips', sums)
    halves = [reduce_own(f'reduce_{k}', p, r1, r2, place) for k, p, r1, r2 in zip(SHARDED, parts, recv1, recv2)]
    reduced = share_halves('grads_share', halves)
    grads = {k: _from_comm_shape(k, g, weights[k]) for k, g in zip(SHARDED, reduced)}

    flat = jnp.concatenate([gn[k].reshape(-1) for k in GAINS])
    rows = -(-flat.shape[0] // (8 * HEAD)) * 8
    packed = jnp.pad(flat, (0, rows * HEAD - flat.shape[0])).reshape(rows, HEAD)
    total = all_reduce_small('gain_grads', packed).reshape(-1)
    off = 0
    for k in GAINS:
        grads[k] = total[off:off + weights[k].size].reshape(weights[k].shape)
        off += weights[k].size

    deltas, new_m, new_v = {}, {}, {}
    for k in WEIGHTS:
        shape = weights[k].shape
        d, nm, nv = adamw(f'adamw_{k}', _as3d(weights[k]), _as3d(grads[k]), _as3d(args['m_' + k]), _as3d(args['v_' + k]))
        deltas[k], new_m[k], new_v[k] = d.reshape(shape), nm.reshape(shape), nv.reshape(shape)
    return (loss, grad_x[None], *[grads[k] for k in WEIGHTS], *[deltas[k] for k in WEIGHTS],
            *[new_m[k] for k in WEIGHTS], *[new_v[k] for k in WEIGHTS])
```

```python
import math

import numpy as np
import jax
import jax.numpy as jnp
from jax import lax
from jax.experimental import pallas as pl
from jax.experimental.pallas import tpu as pltpu

F32 = jnp.float32
BF16 = jnp.bfloat16
NORM_EPS = 1e-6
MASK_VALUE = -1e30
HEAD = 128
ROPE = 64
QK_PAD = 256
DILATED_BRANCHES = ((128, 1), (512, 4), (2048, 16))
ROPE_THETA = 10000.0
ADAM_LR, ADAM_B1, ADAM_B2, ADAM_EPS, ADAM_WD, ADAM_STEP = 0.001, 0.9, 0.999, 1e-08, 0.01, 10
VMEM_LIMIT_BYTES = 56 * 1024 * 1024
MM_ROWS, MM_COLS, MM_WIDE, MM_SHALLOW, MM_DEPTH = 512, 512, 1024, 2048, 8192
DILATED_HEADS_PER_STEP, MLA_FWD_HEADS_PER_STEP, MLA_BWD_HEADS_PER_STEP = 8, 4, 2
N_CHIPS = 4
MESH = pl.DeviceIdType.MESH


def _params(n_axes, vmem=VMEM_LIMIT_BYTES):
    return pltpu.CompilerParams(dimension_semantics=("arbitrary",) * n_axes, vmem_limit_bytes=vmem)


def _tile(n, pref, mult=128):
    if n <= pref:
        return n
    t = (pref // mult) * mult
    while t >= mult:
        if n % t == 0:
            return t
        t -= mult
    return n


def _mm_call(name, grid, ins, in_specs, out_shape, out_specs, n_a, n_b, terms, dims, acc_shapes, epilogue, into=None):
    n_in, n_out, nk = len(ins), len(out_shape), grid[2]
    aliases = {}
    if into is not None:
        ins = list(ins) + [into]
        in_specs = list(in_specs) + [pl.BlockSpec(memory_space=pl.ANY)]
        aliases = {n_in: 0}

    def body(*refs):
        in_refs, refs = refs[:n_in], refs[n_in + len(aliases):]
        out_refs, accs = refs[:n_out], refs[n_out:]
        a_refs, b_refs, extra = in_refs[:n_a], in_refs[n_a:n_a + n_b], in_refs[n_a + n_b:]
        k = pl.program_id(2)
        sums = [None] * len(acc_shapes)
        for t, ia, ib in terms:
            a = a_refs[ia][...].astype(BF16)
            b = b_refs[ib][...].astype(BF16)
            part = lax.dot_general(a, b, (dims, ((), ())), preferred_element_type=F32)
            sums[t] = part if sums[t] is None else sums[t] + part
        if nk == 1:
            epilogue(sums, extra, out_refs)
            return

        @pl.when(k == 0)
        def _():
            for acc, part in zip(accs, sums):
                acc[...] = part

        @pl.when(k > 0)
        def _():
            for acc, part in zip(accs, sums):
                acc[...] += part

        @pl.when(k == nk - 1)
        def _():
            epilogue([acc[...] for acc in accs], extra, out_refs)

    return pl.pallas_call(
        body, name=name, grid=grid, in_specs=in_specs, out_specs=out_specs, out_shape=out_shape,
        scratch_shapes=[pltpu.VMEM(s, F32) for s in acc_shapes] if nk > 1 else [], input_output_aliases=aliases,
        compiler_params=_params(3),
    )(*ins)


def _b_spec_nn(w, tk, tn):
    b, layer = w
    if b.ndim == 3:
        return pl.BlockSpec((None, tk, tn), lambda i, j, k: (layer, k, j))
    per = b.shape[3] // tn
    return pl.BlockSpec((None, None, tk, tn), lambda i, j, k: (layer, j // per, k, j % per))


def _rope_tile(z, cs, s1, s2):
    return z * cs + pltpu.roll(z, 96, 1) * s1 + pltpu.roll(z, 32, 1) * s2


def mm_nn(name, a, w, *, out_dtype, res=None, alpha=1.0, rope=None, tm=MM_ROWS, tn=None, tk=MM_DEPTH):
    m, kk = a.shape
    b = w[0]
    ns = b.shape[-1]
    n = b.shape[1] * ns if b.ndim == 4 else ns
    tm, tk = _tile(m, tm, 8), _tile(kk, tk)
    tn = _tile(ns, tn or (MM_WIDE if kk <= MM_SHALLOW else MM_COLS))
    if rope is not None and rope[0] == 'q':
        assert tn % QK_PAD == 0
    grid = (m // tm, n // tn, kk // tk)
    ins = [a, b]
    in_specs = [pl.BlockSpec((tm, tk), lambda i, j, k: (i, k)), _b_spec_nn(w, tk, tn)]
    if res is not None:
        ins.append(res)
        in_specs.append(pl.BlockSpec((tm, tn), lambda i, j, k: (i, j)))
    if rope is not None:
        ins += list(rope[1:])
        in_specs += [pl.BlockSpec((tm, HEAD), lambda i, j, k: (i, 0))] * 3

    def epilogue(accs, extra, outs):
        y = accs[0]
        extra = list(extra)
        if alpha != 1.0:
            y = alpha * y
        if res is not None:
            y = extra.pop(0)[...] + y
        if rope is not None:
            cs, s1, s2 = (e[...] for e in extra)
            step = QK_PAD if rope[0] == 'q' else HEAD
            for c0 in range(0, tn, step):
                if rope[0] == 'q':
                    outs[0][:, c0:c0 + HEAD] = y[:, c0:c0 + HEAD].astype(out_dtype)
                    c0 += HEAD
                outs[0][:, c0:c0 + HEAD] = _rope_tile(y[:, c0:c0 + HEAD], cs, s1, s2).astype(out_dtype)
        else:
            outs[0][...] = y.astype(out_dtype)

    return _mm_call(name, grid, ins, in_specs, [jax.ShapeDtypeStruct((m, n), out_dtype)],
                    [pl.BlockSpec((tm, tn), lambda i, j, k: (i, j))], 1, 1, [(0, 0, 0)], ((1,), (0,)),
                    [(tm, tn)], epilogue)[0]


def ffn_up(name, h, wg, wu, *, tm=MM_ROWS, tk=MM_DEPTH):
    m, kk = h.shape
    fs = wg[0].shape[3]
    n = wg[0].shape[1] * fs
    tm, tk, tn = _tile(m, tm, 8), _tile(kk, tk), fs
    grid = (m // tm, n // tn, kk // tk)

    def epilogue(accs, extra, outs):
        g, u = accs
        outs[0][...] = g.astype(BF16)
        outs[1][...] = u.astype(BF16)
        outs[2][...] = (g * jax.nn.sigmoid(g) * u).astype(BF16)

    o_spec = pl.BlockSpec((tm, tn), lambda i, j, k: (i, j))
    return _mm_call(name, grid, [h, wg[0], wu[0]],
                    [pl.BlockSpec((tm, tk), lambda i, j, k: (i, k)), _b_spec_nn(wg, tk, tn), _b_spec_nn(wu, tk, tn)],
                    [jax.ShapeDtypeStruct((m, n), BF16)] * 3, [o_spec] * 3, 1, 2, [(0, 0, 0), (1, 0, 1)],
                    ((1,), (0,)), [(tm, tn)] * 2, epilogue)


def mm_nt(name, a_list, w_list, *, out_dtype, alpha=1.0, swiglu=None, res=None, tm=MM_ROWS, tn=None, tk=MM_DEPTH):
    m, n = a_list[0].shape
    b0 = w_list[0][0]
    slab = b0.ndim == 4
    kk = b0.shape[-2]
    ns = b0.shape[-1]
    tk = ns if slab else _tile(ns, tk)
    wide = swiglu is None and (slab or tk <= MM_SHALLOW)
    tm, tn = _tile(m, tm, 8), _tile(kk, tn or (MM_WIDE if wide else MM_COLS))
    grid = (m // tm, kk // tn, n // tk)

    def b_spec(w):
        layer = w[1]
        if slab:
            per = ns // tk
            return pl.BlockSpec((None, None, tn, tk), lambda i, j, k: (layer, k // per, j, k % per))
        return pl.BlockSpec((None, tn, tk), lambda i, j, k: (layer, j, k))

    p = len(a_list)
    ins = list(a_list) + [w[0] for w in w_list]
    in_specs = [pl.BlockSpec((tm, tk), lambda i, j, k: (i, k))] * p + [b_spec(w) for w in w_list]
    o_spec = pl.BlockSpec((tm, tn), lambda i, j, k: (i, j))
    if swiglu is not None:
        ins += list(swiglu)
        in_specs += [o_spec, o_spec]
        out_shape = [jax.ShapeDtypeStruct((m, kk), BF16)] * 2

        def epilogue(accs, extra, outs):
            da = alpha * accs[0]
            g = extra[0][...].astype(F32)
            u = extra[1][...].astype(F32)
            sg = jax.nn.sigmoid(g)
            silu = g * sg
            outs[0][...] = (da * u * (sg + silu * (1.0 - sg))).astype(BF16)
            outs[1][...] = (da * silu).astype(BF16)
    else:
        out_shape = [jax.ShapeDtypeStruct((m, kk), out_dtype)]
        if res is not None:
            ins.append(res)
            in_specs.append(o_spec)

        def epilogue(accs, extra, outs):
            y = alpha * accs[0]
            if res is not None:
                y = y + extra[0][...]
            outs[0][...] = y.astype(out_dtype)

    outs = _mm_call(name, grid, ins, in_specs, out_shape, [o_spec] * len(out_shape), p, p,
                    [(0, q, q) for q in range(p)], ((1,), (1,)), [(tm, tn)], epilogue)
    return outs if swiglu is not None else outs[0]


def mm_tn(name, a, b, into, *, alpha=1.0, tm=MM_ROWS, tn=MM_WIDE, tk=MM_SHALLOW):
    buf, layer = into
    m, kk = a.shape
    n = b.shape[1]
    ns = buf.shape[-1]
    tm, tk = _tile(kk, tm), _tile(m, tk, 8)
    tn = ns if ns % 512 else _tile(ns, tn)
    grid = (kk // tm, n // tn, m // tk)
    if buf.ndim == 4:
        per = ns // tn
        o_spec = pl.BlockSpec((None, None, tm, tn), lambda i, j, k: (layer, j // per, i, j % per))
    else:
        o_spec = pl.BlockSpec((None, tm, tn), lambda i, j, k: (layer, i, j))

    def epilogue(accs, extra, outs):
        outs[0][...] = (alpha * accs[0]).astype(buf.dtype)

    return _mm_call(name, grid, [a, b],
                    [pl.BlockSpec((tk, tm), lambda i, j, k: (k, i)), pl.BlockSpec((tk, tn), lambda i, j, k: (k, j))],
                    [jax.ShapeDtypeStruct(buf.shape, buf.dtype)], [o_spec], 1, 1, [(0, 0, 0)], ((0,), (0,)),
                    [(tm, tn)], epilogue, into=buf)[0]


def norm_fwd(name, x, gain, *, tm=256):
    m, d = x.shape
    tm = _tile(m, tm, 8)

    def body(x_ref, g_ref, o_ref):
        xv = x_ref[...]
        r = lax.rsqrt(jnp.mean(xv * xv, axis=-1, keepdims=True) + NORM_EPS)
        o_ref[...] = (xv * r * g_ref[...]).astype(BF16)

    return pl.pallas_call(
        body, name=name, grid=(m // tm,),
        in_specs=[pl.BlockSpec((tm, d), lambda i: (i, 0)), pl.BlockSpec((1, d), lambda i: (0, 0))],
        out_specs=pl.BlockSpec((tm, d), lambda i: (i, 0)), out_shape=jax.ShapeDtypeStruct((m, d), BF16),
        compiler_params=_params(1),
    )(x, gain)


def norm_bwd(name, x, gain, dh, dres=None, *, tm=256):
    m, d = x.shape
    tm = _tile(m, tm, 8)
    n_steps = m // tm
    has_res = dres is not None

    def body(*refs):
        if has_res:
            x_ref, g_ref, dh_ref, dres_ref, dx_ref, dxb_ref, dg_ref, acc = refs
        else:
            x_ref, g_ref, dh_ref, dx_ref, dxb_ref, dg_ref, acc = refs
        i = pl.program_id(0)
        xv = x_ref[...]
        dy = dh_ref[...].astype(F32)
        r = lax.rsqrt(jnp.mean(xv * xv, axis=-1, keepdims=True) + NORM_EPS)
        xhat = xv * r
        dxhat = dy * g_ref[...]
        dx = r * (dxhat - xhat * jnp.mean(dxhat * xhat, axis=-1, keepdims=True))
        if has_res:
            dx = dx + dres_ref[...]
        dx_ref[...] = dx
        dxb_ref[...] = dx.astype(BF16)

        @pl.when(i == 0)
        def _():
            acc[...] = jnp.zeros_like(acc)

        acc[...] += jnp.sum((dy * xhat).reshape(tm // 8, 8, d), axis=0)

        @pl.when(i == n_steps - 1)
        def _():
            dg_ref[...] = jnp.sum(acc[...], axis=0, keepdims=True)

    row = pl.BlockSpec((tm, d), lambda i: (i, 0))
    vec = pl.BlockSpec((1, d), lambda i: (0, 0))
    ins = [x, gain, dh] + ([dres] if has_res else [])
    return pl.pallas_call(
        body, name=name, grid=(n_steps,), in_specs=[row, vec, row] + ([row] if has_res else []),
        out_specs=[row, row, vec],
        out_shape=[jax.ShapeDtypeStruct((m, d), F32), jax.ShapeDtypeStruct((m, d), BF16), jax.ShapeDtypeStruct((1, d), F32)],
        scratch_shapes=[pltpu.VMEM((8, d), F32)], compiler_params=_params(1),
    )(*ins)


def loss_head(name, x, gain, target, *, tm=256):
    m, d = x.shape
    tm = _tile(m, tm, 8)
    n_steps = m // tm

    def body(x_ref, g_ref, t_ref, loss_ref, dx_ref, dxb_ref, dg_ref, acc, lacc):
        i = pl.program_id(0)
        xv = x_ref[...]
        g = g_ref[...]
        r = lax.rsqrt(jnp.mean(xv * xv, axis=-1, keepdims=True) + NORM_EPS)
        xhat = xv * r
        err = xhat * g - t_ref[...]
        dy = err * (1.0 / d)
        dxhat = dy * g
        dx = r * (dxhat - xhat * jnp.mean(dxhat * xhat, axis=-1, keepdims=True))
        dx_ref[...] = dx
        dxb_ref[...] = dx.astype(BF16)

        @pl.when(i == 0)
        def _():
            acc[...] = jnp.zeros_like(acc)
            lacc[...] = jnp.zeros_like(lacc)

        acc[...] += jnp.sum((dy * xhat).reshape(tm // 8, 8, d), axis=0)
        lacc[...] += jnp.sum((err * err).reshape(tm // 8, 8, d), axis=0)

        @pl.when(i == n_steps - 1)
        def _():
            dg_ref[...] = jnp.sum(acc[...], axis=0, keepdims=True)
            loss_ref[...] = (0.5 / d) * jnp.sum(jnp.sum(lacc[...], axis=0, keepdims=True), axis=1, keepdims=True)

    row = pl.BlockSpec((tm, d), lambda i: (i, 0))
    vec = pl.BlockSpec((1, d), lambda i: (0, 0))
    return pl.pallas_call(
        body, name=name, grid=(n_steps,), in_specs=[row, vec, row],
        out_specs=[pl.BlockSpec((1, 1), lambda i: (0, 0)), row, row, vec],
        out_shape=[jax.ShapeDtypeStruct((1, 1), F32), jax.ShapeDtypeStruct((m, d), F32),
                   jax.ShapeDtypeStruct((m, d), BF16), jax.ShapeDtypeStruct((1, d), F32)],
        scratch_shapes=[pltpu.VMEM((8, d), F32), pltpu.VMEM((8, d), F32)], compiler_params=_params(1),
    )(x, gain, target)


def _lane_is(h, shape):
    return lax.broadcasted_iota(jnp.int32, shape, 1) == h


def _take_lane(tile, h):
    return jnp.sum(jnp.where(_lane_is(h, tile.shape), tile, 0.0), axis=1, keepdims=True)


def _merge_lanes(dst, src, start, count):
    lane = lax.broadcasted_iota(jnp.int32, dst.shape, 1)
    return jnp.where((lane >= start) & (lane < start + count), src, dst)


def _band_scores(q, kp, kc, h, c, dilation, n_heads):
    nt = (((1,), (1,)), ((), ()))
    scale = HEAD ** -0.5
    i = lax.broadcasted_iota(jnp.int32, (HEAD, HEAD), 0)
    j = lax.broadcasted_iota(jnp.int32, (HEAD, HEAD), 1)
    slope = jnp.exp(jnp.full((HEAD, HEAD), -8.0 * math.log(2.0) / n_heads, F32) * (h + 1).astype(F32))
    dist = (i - j).astype(F32) * float(dilation)
    s_c = lax.dot_general(q, kc, nt, preferred_element_type=F32) * scale - slope * dist
    s_p = lax.dot_general(q, kp, nt, preferred_element_type=F32) * scale - slope * (dist + float(dilation * HEAD))
    s_c = jnp.where(j <= i, s_c, MASK_VALUE)
    s_p = jnp.where((j >= i) & (c > 0), s_p, MASK_VALUE)
    return s_p, s_c


def dilated_fwd(name, qkv, state, dilation, n_heads, last):
    s, w3 = qkv.shape
    w = w3 // 3
    d = dilation
    rows, nb = s // d, s // (d * HEAD)
    first = state is None
    qkv_v = qkv.reshape(rows, d * w3)
    hb = min(DILATED_HEADS_PER_STEP, n_heads)
    groups = n_heads // hb

    def body(*refs):
        q_ref, kp_ref, kc_ref, vp_ref, vc_ref = refs[:5]
        refs = refs[5:]
        if not first:
            acc_in, m_in, l_in = refs[:3]
            refs = refs[3:]
        if last:
            o_ref, lse_ref = refs
        else:
            acc_out, m_out, l_out = refs
        c, g = pl.program_id(1), pl.program_id(2)
        if first:
            m_tile = l_tile = jnp.zeros((HEAD, HEAD), F32)
        else:
            m_tile, l_tile = m_in[...], l_in[...]
        stat_a, stat_b = m_tile, l_tile
        for hh in range(hb):
            h = g * hb + hh
            cs = slice(hh * HEAD, (hh + 1) * HEAD)
            s_p, s_c = _band_scores(q_ref[:, cs], kp_ref[:, cs], kc_ref[:, cs], h, c, d, n_heads)
            m_new = jnp.maximum(jnp.max(s_c, axis=1, keepdims=True), jnp.max(s_p, axis=1, keepdims=True))
            if not first:
                m_prev = _take_lane(m_tile, h)
                m_new = jnp.maximum(m_new, m_prev)
            p_c = jnp.exp(s_c - m_new)
            p_p = jnp.exp(s_p - m_new)
            l_new = jnp.sum(p_c, axis=1, keepdims=True) + jnp.sum(p_p, axis=1, keepdims=True)
            nn = (((1,), (0,)), ((), ()))
            acc = (lax.dot_general(p_c.astype(BF16), vc_ref[:, cs], nn, preferred_element_type=F32)
                   + lax.dot_general(p_p.astype(BF16), vp_ref[:, cs], nn, preferred_element_type=F32))
            if not first:
                corr = jnp.exp(m_prev - m_new)
                l_new = l_new + corr * _take_lane(l_tile, h)
                acc = acc + corr * acc_in[:, cs]
            lane = _lane_is(h, (HEAD, HEAD))
            if last:
                o_ref[:, cs] = (acc / l_new).astype(BF16)
                stat_a = jnp.where(lane, m_new + jnp.log(l_new), stat_a)
            else:
                acc_out[:, cs] = acc
                stat_a = jnp.where(lane, m_new, stat_a)
                stat_b = jnp.where(lane, l_new, stat_b)
        if last:
            @pl.when(g == 0)
            def _():
                lse_ref[...] = stat_a

            @pl.when(g > 0)
            def _():
                lse_ref[...] = _merge_lanes(lse_ref[...], stat_a, g * hb, hb)
        else:
            @pl.when(g == 0)
            def _():
                m_out[...] = stat_a
                l_out[...] = stat_b

            @pl.when(g > 0)
            def _():
                m_out[...] = _merge_lanes(m_out[...], stat_a, g * hb, hb)
                l_out[...] = _merge_lanes(l_out[...], stat_b, g * hb, hb)

    blk = (HEAD, hb * HEAD)
    prev = lambda c: jnp.maximum(c - 1, 0)
    in_specs = [
        pl.BlockSpec(blk, lambda r, c, g: (c, r * 3 * groups + g)),
        pl.BlockSpec(blk, lambda r, c, g: (prev(c), r * 3 * groups + groups + g)),
        pl.BlockSpec(blk, lambda r, c, g: (c, r * 3 * groups + groups + g)),
        pl.BlockSpec(blk, lambda r, c, g: (prev(c), r * 3 * groups + 2 * groups + g)),
        pl.BlockSpec(blk, lambda r, c, g: (c, r * 3 * groups + 2 * groups + g)),
    ]
    head_spec = pl.BlockSpec(blk, lambda r, c, g: (c, r * groups + g))
    stat_spec = pl.BlockSpec((HEAD, HEAD), lambda r, c, g: (c, r))
    ins = [qkv_v] * 5
    if not first:
        acc0, m0, l0 = state
        ins += [acc0.reshape(rows, d * w), m0.reshape(rows, d * HEAD), l0.reshape(rows, d * HEAD)]
        in_specs += [head_spec, stat_spec, stat_spec]
    stat_shape = jax.ShapeDtypeStruct((rows, d * HEAD), F32)
    if last:
        out_shape = [jax.ShapeDtypeStruct((rows, d * w), BF16), stat_shape]
        out_specs = [head_spec, stat_spec]
    else:
        out_shape = [jax.ShapeDtypeStruct((rows, d * w), F32), stat_shape, stat_shape]
        out_specs = [head_spec, stat_spec, stat_spec]
    outs = pl.pallas_call(body, name=name, grid=(d, nb, groups), in_specs=in_specs, out_specs=out_specs,
                          out_shape=out_shape, compiler_params=_params(3))(*ins)
    if last:
        return outs[0].reshape(s, w), outs[1].reshape(s, HEAD)
    return outs[0].reshape(s, w), outs[1].reshape(s, HEAD), outs[2].reshape(s, HEAD)


def dilated_bwd(name, qkv, o, do, lse, dilation, n_heads):
    s, w3 = qkv.shape
    w = w3 // 3
    d = dilation
    rows, nb = s // d, s // (d * HEAD)
    hb = min(DILATED_HEADS_PER_STEP, n_heads)
    groups = n_heads // hb

    def body(q_ref, kp_ref, kc_ref, vp_ref, vc_ref, o_ref, do_ref, lse_ref, dq_ref, dk_ref, dv_ref, dk_c, dv_c):
        g, c = pl.program_id(1), pl.program_id(2)

        @pl.when(c == 0)
        def _():
            dk_c[...] = jnp.zeros_like(dk_c)
            dv_c[...] = jnp.zeros_like(dv_c)

        @pl.when(c < nb)
        def _():
            lse_tile = lse_ref[...]
            nt = (((1,), (1,)), ((), ()))
            nn = (((1,), (0,)), ((), ()))
            tn = (((0,), (0,)), ((), ()))
            scale = HEAD ** -0.5
            for hh in range(hb):
                h = g * hb + hh
                cs = slice(hh * HEAD, (hh + 1) * HEAD)
                q, kp, kc, vp, vc, dout = q_ref[:, cs], kp_ref[:, cs], kc_ref[:, cs], vp_ref[:, cs], vc_ref[:, cs], do_ref[:, cs]
                s_p, s_c = _band_scores(q, kp, kc, h, c, d, n_heads)
                lse_h = _take_lane(lse_tile, h)
                delta = jnp.sum(dout.astype(F32) * o_ref[:, cs].astype(F32), axis=1, keepdims=True)
                p_c = jnp.exp(s_c - lse_h)
                p_p = jnp.exp(s_p - lse_h)
                ds_c = (p_c * (lax.dot_general(dout, vc, nt, preferred_element_type=F32) - delta) * scale).astype(BF16)
                ds_p = (p_p * (lax.dot_general(dout, vp, nt, preferred_element_type=F32) - delta) * scale).astype(BF16)
                dq_ref[:, cs] = (lax.dot_general(ds_c, kc, nn, preferred_element_type=F32)
                                 + lax.dot_general(ds_p, kp, nn, preferred_element_type=F32)).astype(BF16)
                dk_ref[:, cs] = (dk_c[:, cs] + lax.dot_general(ds_p, q, tn, preferred_element_type=F32)).astype(BF16)
                dv_ref[:, cs] = (dv_c[:, cs]
                                 + lax.dot_general(p_p.astype(BF16), dout, tn, preferred_element_type=F32)).astype(BF16)
                dk_c[:, cs] = lax.dot_general(ds_c, q, tn, preferred_element_type=F32)
                dv_c[:, cs] = lax.dot_general(p_c.astype(BF16), dout, tn, preferred_element_type=F32)

        @pl.when(c == nb)
        def _():
            dk_ref[...] = dk_c[...].astype(BF16)
            dv_ref[...] = dv_c[...].astype(BF16)

    blk = (HEAD, hb * HEAD)
    cur = lambda c: jnp.minimum(c, nb - 1)
    prev = lambda c: jnp.clip(c - 1, 0, nb - 1)
    in_specs = [
        pl.BlockSpec(blk, lambda r, g, c: (cur(c), r * 3 * groups + g)),
        pl.BlockSpec(blk, lambda r, g, c: (prev(c), r * 3 * groups + groups + g)),
        pl.BlockSpec(blk, lambda r, g, c: (cur(c), r * 3 * groups + groups + g)),
        pl.BlockSpec(blk, lambda r, g, c: (prev(c), r * 3 * groups + 2 * groups + g)),
        pl.BlockSpec(blk, lambda r, g, c: (cur(c), r * 3 * groups + 2 * groups + g)),
        pl.BlockSpec(blk, lambda r, g, c: (cur(c), r * groups + g)),
        pl.BlockSpec(blk, lambda r, g, c: (cur(c), r * groups + g)),
        pl.BlockSpec((HEAD, HEAD), lambda r, g, c: (cur(c), r)),
    ]
    out_specs = [
        pl.BlockSpec(blk, lambda r, g, c: (cur(c), r * groups + g)),
        pl.BlockSpec(blk, lambda r, g, c: (prev(c), r * groups + g)),
        pl.BlockSpec(blk, lambda r, g, c: (prev(c), r * groups + g)),
    ]
    qkv_v = qkv.reshape(rows, d * w3)
    outs = pl.pallas_call(
        body, name=name, grid=(d, groups, nb + 1), in_specs=in_specs, out_specs=out_specs,
        out_shape=[jax.ShapeDtypeStruct((rows, d * w), BF16)] * 3,
        scratch_shapes=[pltpu.VMEM(blk, F32)] * 2, compiler_params=_params(3),
    )(qkv_v, qkv_v, qkv_v, qkv_v, qkv_v, o.reshape(rows, d * w), do.reshape(rows, d * w), lse.reshape(rows, d * HEAD))
    return [t.reshape(s, w) for t in outs]


def sum_branches(name, parts, *, tm=256):
    s, w = parts[0][0].shape
    tm = _tile(s, tm, 8)

    def body(*refs):
        out = refs[9]
        for t in range(3):
            acc = refs[t][...].astype(F32) + refs[3 + t][...].astype(F32) + refs[6 + t][...].astype(F32)
            out[:, t * w:(t + 1) * w] = acc.astype(BF16)

    row = pl.BlockSpec((tm, w), lambda i: (i, 0))
    return pl.pallas_call(
        body, name=name, grid=(s // tm,), in_specs=[row] * 9, out_specs=pl.BlockSpec((tm, 3 * w), lambda i: (i, 0)),
        out_shape=jax.ShapeDtypeStruct((s, 3 * w), BF16), compiler_params=_params(1),
    )(*[t for trip in parts for t in trip])


def _triangle(nq, n_heads, by_key):
    a, hh, b = [], [], []
    for outer in range(nq):
        for h in range(n_heads):
            inner = range(outer, nq) if by_key else range(outer + 1)
            for t in inner:
                a.append(outer)
                hh.append(h)
                b.append(t)
    return (jnp.asarray(np.array(a, np.int32)), jnp.asarray(np.array(hh, np.int32)), jnp.asarray(np.array(b, np.int32)))


def _causal_mask(qi, ki, t):
    row = lax.broadcasted_iota(jnp.int32, (t, t), 0) + qi * t
    col = lax.broadcasted_iota(jnp.int32, (t, t), 1) + ki * t
    return col <= row


def _causal_scores(q, kn, kr, mask, scale):
    nt = (((1,), (1,)), ((), ()))
    k = jnp.concatenate([kn, kr], axis=1)
    s = lax.dot_general(q, k, nt, preferred_element_type=F32) * scale
    return jnp.where(mask, s, MASK_VALUE)


def _heads(hh):
    return slice(hh * QK_PAD, (hh + 1) * QK_PAD), slice(hh * HEAD, (hh + 1) * HEAD)


def mla_fwd(name, q, kn, kr, v, n_heads, scale, *, t=512):
    s = q.shape[0]
    t = _tile(s, t)
    nq = s // t
    hb = min(MLA_FWD_HEADS_PER_STEP, n_heads)
    qt, gt, kt = _triangle(nq, n_heads // hb, by_key=False)

    def body(qt_ref, gt_ref, kt_ref, q_ref, kn_ref, kr_ref, v_ref, o_ref, lse_ref, m_sc, l_sc, acc_sc):
        step = pl.program_id(0)
        qi, g, ki = qt_ref[step], gt_ref[step], kt_ref[step]

        @pl.when(ki == 0)
        def _():
            m_sc[...] = jnp.full_like(m_sc, MASK_VALUE)
            l_sc[...] = jnp.zeros_like(l_sc)
            acc_sc[...] = jnp.zeros_like(acc_sc)

        mask = _causal_mask(qi, ki, t)
        kr_v = kr_ref[...]
        for hh in range(hb):
            wide, cs = _heads(hh)
            sc = _causal_scores(q_ref[:, wide], kn_ref[:, cs], kr_v, mask, scale)
            m_prev = m_sc[hh]
            m_new = jnp.maximum(m_prev, jnp.max(sc, axis=1, keepdims=True))
            p = jnp.exp(sc - m_new)
            corr = jnp.exp(m_prev - m_new)
            l_sc[hh] = corr * l_sc[hh] + jnp.sum(p, axis=1, keepdims=True)
            acc_sc[hh] = corr * acc_sc[hh] + lax.dot_general(p.astype(BF16), v_ref[:, cs], (((1,), (0,)), ((), ())),
                                                             preferred_element_type=F32)
            m_sc[hh] = m_new

        @pl.when(ki == qi)
        def _():
            @pl.when(g == 0)
            def _():
                lse_ref[...] = jnp.zeros_like(lse_ref)

            tile = lse_ref[...]
            for hh in range(hb):
                o_ref[:, _heads(hh)[1]] = (acc_sc[hh] / l_sc[hh]).astype(BF16)
                tile = jnp.where(_lane_is(g * hb + hh, (t, HEAD)), m_sc[hh] + jnp.log(l_sc[hh]), tile)
            lse_ref[...] = tile

    grid_spec = pltpu.PrefetchScalarGridSpec(
        num_scalar_prefetch=3, grid=(int(qt.shape[0]),),
        in_specs=[
            pl.BlockSpec((t, hb * QK_PAD), lambda i, a, b, c: (a[i], b[i])),
            pl.BlockSpec((t, hb * HEAD), lambda i, a, b, c: (c[i], b[i])),
            pl.BlockSpec((t, HEAD), lambda i, a, b, c: (c[i], 0)),
            pl.BlockSpec((t, hb * HEAD), lambda i, a, b, c: (c[i], b[i])),
        ],
        out_specs=[pl.BlockSpec((t, hb * HEAD), lambda i, a, b, c: (a[i], b[i])),
                   pl.BlockSpec((t, HEAD), lambda i, a, b, c: (a[i], 0))],
        scratch_shapes=[pltpu.VMEM((hb, t, 1), F32), pltpu.VMEM((hb, t, 1), F32), pltpu.VMEM((hb, t, HEAD), F32)],
    )
    return pl.pallas_call(
        body, name=name, grid_spec=grid_spec,
        out_shape=[jax.ShapeDtypeStruct((s, n_heads * HEAD), BF16), jax.ShapeDtypeStruct((s, HEAD), F32)],
        compiler_params=_params(1),
    )(qt, gt, kt, q, kn, kr, v)


def mla_bwd_dq(name, q, kn, kr, v, o, do, lse, tabs, n_heads, scale, *, t=512):
    s = q.shape[0]
    t = _tile(s, t)
    nq = s // t
    hb = min(MLA_BWD_HEADS_PER_STEP, n_heads)
    qt, gt, kt = _triangle(nq, n_heads // hb, by_key=False)

    def body(qt_ref, gt_ref, kt_ref, q_ref, kn_ref, kr_ref, v_ref, o_ref, do_ref, lse_ref, cs_ref, s1_ref, s2_ref,
             dq_ref, acc_sc, delta_sc):
        step = pl.program_id(0)
        qi, g, ki = qt_ref[step], gt_ref[step], kt_ref[step]

        @pl.when(ki == 0)
        def _():
            acc_sc[...] = jnp.zeros_like(acc_sc)
            for hh in range(hb):
                cs = _heads(hh)[1]
                delta_sc[hh] = jnp.sum(do_ref[:, cs].astype(F32) * o_ref[:, cs].astype(F32), axis=1, keepdims=True)

        mask = _causal_mask(qi, ki, t)
        kr_v, lse_tile = kr_ref[...], lse_ref[...]
        for hh in range(hb):
            wide, cs = _heads(hh)
            kn_v = kn_ref[:, cs]
            sc = _causal_scores(q_ref[:, wide], kn_v, kr_v, mask, scale)
            p = jnp.exp(sc - _take_lane(lse_tile, g * hb + hh))
            dp = lax.dot_general(do_ref[:, cs], v_ref[:, cs], (((1,), (1,)), ((), ())), preferred_element_type=F32)
            ds = (p * (dp - delta_sc[hh]) * scale).astype(BF16)
            k = jnp.concatenate([kn_v, kr_v], axis=1)
            acc_sc[hh] += lax.dot_general(ds, k, (((1,), (0,)), ((), ())), preferred_element_type=F32)

        @pl.when(ki == qi)
        def _():
            for hh in range(hb):
                dq = acc_sc[hh]
                c0 = hh * QK_PAD
                dq_ref[:, c0:c0 + HEAD] = dq[:, :HEAD].astype(BF16)
                dq_ref[:, c0 + HEAD:c0 + QK_PAD] = _rope_tile(dq[:, HEAD:], cs_ref[...], s1_ref[...],
                                                              s2_ref[...]).astype(BF16)

    tab = pl.BlockSpec((t, HEAD), lambda i, a, b, c: (a[i], 0))
    grid_spec = pltpu.PrefetchScalarGridSpec(
        num_scalar_prefetch=3, grid=(int(qt.shape[0]),),
        in_specs=[
            pl.BlockSpec((t, hb * QK_PAD), lambda i, a, b, c: (a[i], b[i])),
            pl.BlockSpec((t, hb * HEAD), lambda i, a, b, c: (c[i], b[i])),
            pl.BlockSpec((t, HEAD), lambda i, a, b, c: (c[i], 0)),
            pl.BlockSpec((t, hb * HEAD), lambda i, a, b, c: (c[i], b[i])),
            pl.BlockSpec((t, hb * HEAD), lambda i, a, b, c: (a[i], b[i])),
            pl.BlockSpec((t, hb * HEAD), lambda i, a, b, c: (a[i], b[i])),
            tab, tab, tab, tab,
        ],
        out_specs=pl.BlockSpec((t, hb * QK_PAD), lambda i, a, b, c: (a[i], b[i])),
        scratch_shapes=[pltpu.VMEM((hb, t, QK_PAD), F32), pltpu.VMEM((hb, t, 1), F32)],
    )
    return pl.pallas_call(
        body, name=name, grid_spec=grid_spec, out_shape=jax.ShapeDtypeStruct((s, n_heads * QK_PAD), BF16),
        compiler_params=_params(1),
    )(qt, gt, kt, q, kn, kr, v, o, do, lse, *tabs)


def mla_bwd_dkv(name, q, kn, kr, v, o, do, lse, prev, n_heads, scale, *, t=512):
    s = q.shape[0]
    t = _tile(s, t)
    nq = s // t
    hb = min(MLA_BWD_HEADS_PER_STEP, n_heads)
    kt, gt, qt = _triangle(nq, n_heads // hb, by_key=True)
    has_prev = prev is not None

    def body(*refs):
        kt_ref, gt_ref, qt_ref, q_ref, kn_ref, kr_ref, v_ref, o_ref, do_ref, lse_ref = refs[:10]
        refs = refs[10:]
        if has_prev:
            pkn_ref, pkr_ref, pv_ref = refs[:3]
            refs = refs[3:]
        dkn_ref, dkr_ref, dv_ref, dk_sc, dv_sc = refs
        step = pl.program_id(0)
        ki, g, qi = kt_ref[step], gt_ref[step], qt_ref[step]

        @pl.when(qi == ki)
        def _():
            dk_sc[...] = jnp.zeros_like(dk_sc)
            dv_sc[...] = jnp.zeros_like(dv_sc)

        mask = _causal_mask(qi, ki, t)
        kr_v, lse_tile = kr_ref[...], lse_ref[...]
        tn = (((0,), (0,)), ((), ()))
        for hh in range(hb):
            wide, cs = _heads(hh)
            q_v, dout = q_ref[:, wide], do_ref[:, cs]
            sc = _causal_scores(q_v, kn_ref[:, cs], kr_v, mask, scale)
            p = jnp.exp(sc - _take_lane(lse_tile, g * hb + hh))
            delta = jnp.sum(dout.astype(F32) * o_ref[:, cs].astype(F32), axis=1, keepdims=True)
            dp = lax.dot_general(dout, v_ref[:, cs], (((1,), (1,)), ((), ())), preferred_element_type=F32)
            ds = (p * (dp - delta) * scale).astype(BF16)
            dk_sc[hh] += lax.dot_general(ds, q_v, tn, preferred_element_type=F32)
            dv_sc[hh] += lax.dot_general(p.astype(BF16), dout, tn, preferred_element_type=F32)

        @pl.when(qi == nq - 1)
        def _():
            @pl.when(g == 0)
            def _():
                dkr_ref[...] = pkr_ref[...] if has_prev else jnp.zeros_like(dkr_ref)

            dkr = dkr_ref[...]
            for hh in range(hb):
                cs = _heads(hh)[1]
                dk = dk_sc[hh]
                dkn, dv = dk[:, :HEAD], dv_sc[hh]
                if has_prev:
                    dkn = dkn + pkn_ref[:, cs].astype(F32)
                    dv = dv + pv_ref[:, cs].astype(F32)
                dkn_ref[:, cs] = dkn.astype(BF16)
                dv_ref[:, cs] = dv.astype(BF16)
                dkr = dkr + dk[:, HEAD:]
            dkr_ref[...] = dkr

    head_q = pl.BlockSpec((t, hb * HEAD), lambda i, a, b, c: (c[i], b[i]))
    head_k = pl.BlockSpec((t, hb * HEAD), lambda i, a, b, c: (a[i], b[i]))
    shared_k = pl.BlockSpec((t, HEAD), lambda i, a, b, c: (a[i], 0))
    in_specs = [
        pl.BlockSpec((t, hb * QK_PAD), lambda i, a, b, c: (c[i], b[i])), head_k, shared_k, head_k, head_q, head_q,
        pl.BlockSpec((t, HEAD), lambda i, a, b, c: (c[i], 0)),
    ]
    ins = [q, kn, kr, v, o, do, lse]
    if has_prev:
        in_specs += [head_k, shared_k, head_k]
        ins += list(prev)
    grid_spec = pltpu.PrefetchScalarGridSpec(
        num_scalar_prefetch=3, grid=(int(kt.shape[0]),), in_specs=in_specs, out_specs=[head_k, shared_k, head_k],
        scratch_shapes=[pltpu.VMEM((hb, t, QK_PAD), F32), pltpu.VMEM((hb, t, HEAD), F32)],
    )
    return pl.pallas_call(
        body, name=name, grid_spec=grid_spec,
        out_shape=[jax.ShapeDtypeStruct((s, n_heads * HEAD), BF16), jax.ShapeDtypeStruct((s, HEAD), F32),
                   jax.ShapeDtypeStruct((s, n_heads * HEAD), BF16)],
        compiler_params=_params(1),
    )(kt, gt, qt, *ins)


def rope_rows(name, z, tabs, *, tm=512):
    s = z.shape[0]
    tm = _tile(s, tm, 8)

    def body(z_ref, cs_ref, s1_ref, s2_ref, o_ref):
        o_ref[...] = _rope_tile(z_ref[...], cs_ref[...], s1_ref[...], s2_ref[...]).astype(BF16)

    row = pl.BlockSpec((tm, HEAD), lambda i: (i, 0))
    return pl.pallas_call(body, name=name, grid=(s // tm,), in_specs=[row] * 4, out_specs=row,
                          out_shape=jax.ShapeDtypeStruct((s, HEAD), BF16), compiler_params=_params(1))(z, *tabs)


def rope_tables(s):
    inv = 1.0 / (ROPE_THETA ** (jnp.arange(0, ROPE, 2, dtype=F32) / ROPE))
    ang = jnp.arange(s, dtype=F32)[:, None] * inv[None, :]
    cos, sin, zero = jnp.cos(ang), jnp.sin(ang), jnp.zeros((s, ROPE // 2), F32)
    pad = jnp.zeros((s, HEAD - ROPE), F32)
    cs = jnp.concatenate([cos, cos, pad], axis=1)
    lo = jnp.concatenate([sin, zero, pad], axis=1)
    hi = jnp.concatenate([zero, sin, pad], axis=1)
    return (cs, -lo, hi), (cs, lo, -hi)


def local_step(x, target, w, norms):
    s, d = x.shape
    depth = norms['ffn_norm1'].shape[0]
    n_a = w['a_wqkv'].shape[0]
    gw = {k: lax.empty(v.shape, BF16) for k, v in w.items()}
    n_heads = d // HEAD
    fwd_tabs, bwd_tabs = rope_tables(s)
    mla_scale = (HEAD + ROPE) ** -0.5
    gain = lambda name, i: norms[name][i:i + 1]
    saved = []
    kv = None

    def ffn_fwd(tag, x, g_norm, which, layer):
        wg, wu, wd = ((w[f'{which}_{nm}'], layer) for nm in ('wg', 'wu', 'wd'))
        h = norm_fwd(f'{tag}_norm', x, g_norm)
        g, u, a = ffn_up(f'{tag}_up', h, wg, wu)
        y = mm_nn(f'{tag}_down', a, wd, out_dtype=F32, res=x, alpha=0.5)
        return y, (x, h, g, u, a)

    def ffn_bwd(tag, dx, dxb, g_norm, which, layer, sv):
        wg, wu, wd = ((w[f'{which}_{nm}'], layer) for nm in ('wg', 'wu', 'wd'))
        x, h, g, u, a = sv
        dg, du = mm_nt(f'{tag}_bwd_da', [dxb], [wd], out_dtype=BF16, alpha=0.5, swiglu=(g, u))
        gw[f'{which}_wd'] = mm_tn(f'{tag}_bwd_dwd', a, dxb, (gw[f'{which}_wd'], layer), alpha=0.5)
        gw[f'{which}_wg'] = mm_tn(f'{tag}_bwd_dwg', h, dg, (gw[f'{which}_wg'], layer))
        gw[f'{which}_wu'] = mm_tn(f'{tag}_bwd_dwu', h, du, (gw[f'{which}_wu'], layer))
        dh = mm_nt(f'{tag}_bwd_dh', [dg, du], [wg, wu], out_dtype=F32)
        return norm_bwd(f'{tag}_bwd_norm', x, g_norm, dh, dx)

    for layer in range(depth):
        if layer == n_a:
            hk = norm_fwd('kv_norm', x, norms['kv_norm'])
            ckv_raw = mm_nn('kv_down', hk, (w['b_wdkv'], 0), out_dtype=F32)
            ckv = norm_fwd('kv_cnorm', ckv_raw, norms['b_ckv_norm'])
            k_nope = mm_nn('kv_uk', ckv, (w['b_wuk'], 0), out_dtype=BF16)
            v_shared = mm_nn('kv_uv', ckv, (w['b_wuv'], 0), out_dtype=BF16)
            k_rope = mm_nn('kv_kr', hk, (w['b_wkr'], 0), out_dtype=BF16, rope=('k',) + fwd_tabs, tn=HEAD)
            kv = (x, hk, ckv_raw, ckv, k_nope, k_rope, v_shared)
        x, sv1 = ffn_fwd(f'l{layer}_ffn1', x, gain('ffn_norm1', layer), 'ffn1', layer)
        h = norm_fwd(f'l{layer}_mix_norm', x, gain('mix_norm', layer))
        if layer < n_a:
            qkv = mm_nn(f'l{layer}_qkv', h, (w['a_wqkv'], layer), out_dtype=BF16)
            state = None
            for bi, (window, dilation) in enumerate(DILATED_BRANCHES):
                last = bi == len(DILATED_BRANCHES) - 1
                state = dilated_fwd(f'l{layer}_dil{dilation}', qkv, state, dilation, n_heads, last)
            o, lse = state
            x_new = mm_nn(f'l{layer}_wo', o, (w['a_wo'], layer), out_dtype=F32, res=x)
            svm = (x, h, qkv, o, lse)
        else:
            jb = layer - n_a
            cq_raw = mm_nn(f'l{layer}_dq', h, (w['b_wdq'], jb), out_dtype=F32)
            cq = norm_fwd(f'l{layer}_cq_norm', cq_raw, gain('b_cq_norm', jb))
            q = mm_nn(f'l{layer}_uq', cq, (w['b_wuq'], jb), out_dtype=BF16, rope=('q',) + fwd_tabs)
            o, lse = mla_fwd(f'l{layer}_mla', q, kv[4], kv[5], kv[6], n_heads, mla_scale)
            x_new = mm_nn(f'l{layer}_wo', o, (w['b_wo'], jb), out_dtype=F32, res=x)
            svm = (x, h, cq_raw, cq, q, o, lse)
        x = x_new
        x, sv2 = ffn_fwd(f'l{layer}_ffn2', x, gain('ffn_norm2', layer), 'ffn2', layer)
        saved.append((sv1, svm, sv2))

    loss, dx, dxb, d_final = loss_head('loss_head', x, norms['final_norm'], target)

    gn = {k: [None] * v.shape[0] for k, v in norms.items()}
    gn['final_norm'] = [d_final]
    dkv = None
    for layer in reversed(range(depth)):
        sv1, svm, sv2 = saved[layer]
        dx, dxb, gn['ffn_norm2'][layer] = ffn_bwd(f'l{layer}_ffn2', dx, dxb, gain('ffn_norm2', layer), 'ffn2', layer, sv2)
        if layer < n_a:
            xm, h, qkv, o, lse = svm
            do = mm_nt(f'l{layer}_bwd_do', [dxb], [(w['a_wo'], layer)], out_dtype=BF16)
            gw['a_wo'] = mm_tn(f'l{layer}_bwd_dwo', o, dxb, (gw['a_wo'], layer))
            parts = [dilated_bwd(f'l{layer}_bwd_dil{dilation}', qkv, o, do, lse, dilation, n_heads)
                     for _, dilation in DILATED_BRANCHES]
            dqkv = sum_branches(f'l{layer}_bwd_sum', parts)
            gw['a_wqkv'] = mm_tn(f'l{layer}_bwd_dwqkv', h, dqkv, (gw['a_wqkv'], layer))
            dh = mm_nt(f'l{layer}_bwd_dh', [dqkv], [(w['a_wqkv'], layer)], out_dtype=F32)
        else:
            jb = layer - n_a
            xm, h, cq_raw, cq, q, o, lse = svm
            do = mm_nt(f'l{layer}_bwd_do', [dxb], [(w['b_wo'], jb)], out_dtype=BF16)
            gw['b_wo'] = mm_tn(f'l{layer}_bwd_dwo', o, dxb, (gw['b_wo'], jb))
            dq = mla_bwd_dq(f'l{layer}_bwd_mla_dq', q, kv[4], kv[5], kv[6], o, do, lse, bwd_tabs, n_heads, mla_scale)
            dkv = mla_bwd_dkv(f'l{layer}_bwd_mla_dkv', q, kv[4], kv[5], kv[6], o, do, lse, dkv, n_heads, mla_scale)
            gw['b_wuq'] = mm_tn(f'l{layer}_bwd_dwuq', cq, dq, (gw['b_wuq'], jb))
            dcq = mm_nt(f'l{layer}_bwd_dcq', [dq], [(w['b_wuq'], jb)], out_dtype=F32)
            _, dcq_raw, gn['b_cq_norm'][jb] = norm_bwd(f'l{layer}_bwd_cq_norm', cq_raw, gain('b_cq_norm', jb), dcq)
            gw['b_wdq'] = mm_tn(f'l{layer}_bwd_dwdq', h, dcq_raw, (gw['b_wdq'], jb))
            dh = mm_nt(f'l{layer}_bwd_dh', [dcq_raw], [(w['b_wdq'], jb)], out_dtype=F32)
        dx, dxb, gn['mix_norm'][layer] = norm_bwd(f'l{layer}_bwd_mix_norm', xm, gain('mix_norm', layer), dh, dx)
        dx, dxb, gn['ffn_norm1'][layer] = ffn_bwd(f'l{layer}_ffn1', dx, dxb, gain('ffn_norm1', layer), 'ffn1', layer, sv1)
        if layer == n_a:
            xk, hk, ckv_raw, ckv, k_nope, k_rope, v_shared = kv
            dkn, dkr_rot, dv = dkv
            dkr = rope_rows('kv_bwd_rope', dkr_rot, bwd_tabs)
            gw['b_wuk'] = mm_tn('kv_bwd_dwuk', ckv, dkn, (gw['b_wuk'], 0))
            gw['b_wuv'] = mm_tn('kv_bwd_dwuv', ckv, dv, (gw['b_wuv'], 0))
            dckv = mm_nt('kv_bwd_dckv', [dkn, dv], [(w['b_wuk'], 0), (w['b_wuv'], 0)], out_dtype=F32)
            _, dckv_raw, g_ckv = norm_bwd('kv_bwd_cnorm', ckv_raw, norms['b_ckv_norm'], dckv)
            gw['b_wdkv'] = mm_tn('kv_bwd_dwdkv', hk, dckv_raw, (gw['b_wdkv'], 0))
            gw['b_wkr'] = mm_tn('kv_bwd_dwkr', hk, dkr, (gw['b_wkr'], 0))
            dhk = mm_nt('kv_bwd_dhk_c', [dckv_raw], [(w['b_wdkv'], 0)], out_dtype=F32)
            dhk = mm_nt('kv_bwd_dhk_r', [dkr], [(w['b_wkr'], 0)], out_dtype=F32, res=dhk)
            dx, dxb, g_kv = norm_bwd('kv_bwd_norm', xk, norms['kv_norm'], dhk, dx)
            gn['kv_norm'], gn['b_ckv_norm'] = [g_kv], [g_ckv]
    grads_n = {k: jnp.concatenate(v, axis=0) for k, v in gn.items()}
    return loss, dx, gw, grads_n


def _place():
    x, y, c = lax.axis_index("x"), lax.axis_index("y"), lax.axis_index("c")
    chips = [(1 - x, y), (x, 1 - y), (1 - x, 1 - y)]
    return x, y, c, 2 * x + y, chips, [2 * cx + cy for cx, cy in chips]


def _remote(src, dst, send_sem, recv_sem, to):
    return pltpu.make_async_remote_copy(src_ref=src, dst_ref=dst, send_sem=send_sem, recv_sem=recv_sem,
                                        device_id=to, device_id_type=MESH)


def cast_to_slab(name, shard, place, *, tr=256):
    l, r, c = shard.shape
    tr = _tile(r, tr, 16)

    def body(place_ref, x_ref, o_ref):
        o_ref[...] = x_ref[...].astype(BF16)

    grid_spec = pltpu.PrefetchScalarGridSpec(
        num_scalar_prefetch=1, grid=(l, r // tr),
        in_specs=[pl.BlockSpec((None, tr, c), lambda i, j, p: (i, j, 0))],
        out_specs=pl.BlockSpec((None, None, tr, c), lambda i, j, p: (i, p[0], j, 0)),
    )
    return pl.pallas_call(body, name=name, grid_spec=grid_spec,
                          out_shape=jax.ShapeDtypeStruct((l, N_CHIPS, r, c), BF16), compiler_params=_params(2))(place, shard)


def all_gather_weights(name, slabs):
    n = len(slabs)

    def body(*refs):
        outs = refs[n:2 * n]
        send_sems, recv_sems = refs[2 * n:]
        x, y, c, me, chips, chip_ids = _place()
        sibling = (x, y, 1 - c)

        def copy(t, k, chip, hc, to):
            rh = outs[t].shape[2] // 2
            blk = outs[t].at[:, chip, pl.ds(hc * rh, rh), :]
            return _remote(blk, blk, send_sems.at[6 * t + k], recv_sems.at[6 * t + k], to)

        first = [copy(t, k, me, c, (*chips[k], c)) for t in range(n) for k in range(3)]
        for cp in first:
            cp.start()
        passed = []
        for t in range(n):
            for k in range(3):
                copy(t, k, chip_ids[k], c, sibling).wait_recv()
                cp = copy(t, 3 + k, chip_ids[k], c, sibling)
                cp.start()
                passed.append(cp)
        for t in range(n):
            for k in range(3):
                copy(t, 3 + k, chip_ids[k], 1 - c, sibling).wait_recv()
        for cp in first + passed:
            cp.wait_send()

    any_spec = pl.BlockSpec(memory_space=pl.ANY)
    return pl.pallas_call(
        body, name=name, in_specs=[any_spec] * n, out_specs=[any_spec] * n,
        out_shape=[jax.ShapeDtypeStruct(a.shape, a.dtype) for a in slabs],
        scratch_shapes=[pltpu.SemaphoreType.DMA((6 * n,)), pltpu.SemaphoreType.DMA((6 * n,))],
        input_output_aliases={t: t for t in range(n)},
        compiler_params=pltpu.CompilerParams(has_side_effects=True),
    )(*slabs)


def exchange_halves(name, parts):
    n = len(parts)

    def body(*refs):
        ins, outs = refs[:n], refs[n:2 * n]
        send_sems, recv_sems = refs[2 * n:]
        x, y, c, me, chips, chip_ids = _place()
        copies = []
        for t in range(n):
            rh = ins[t].shape[2] // 2
            cp = _remote(ins[t].at[:, :, pl.ds((1 - c) * rh, rh), :], outs[t], send_sems.at[t], recv_sems.at[t],
                         (x, y, 1 - c))
            cp.start()
            copies.append(cp)
        for cp in copies:
            cp.wait()

    any_spec = pl.BlockSpec(memory_space=pl.ANY)
    return pl.pallas_call(
        body, name=name, in_specs=[any_spec] * n, out_specs=[any_spec] * n,
        out_shape=[jax.ShapeDtypeStruct((a.shape[0], a.shape[1], a.shape[2] // 2, a.shape[3]), a.dtype) for a in parts],
        scratch_shapes=[pltpu.SemaphoreType.DMA((n,)), pltpu.SemaphoreType.DMA((n,))],
        compiler_params=pltpu.CompilerParams(has_side_effects=True),
    )(*parts)


def add_halves(name, part, recv, place, *, tr=256):
    l, j, rh, c = recv.shape
    tr = _tile(rh, tr, 16)
    nr = rh // tr

    def body(place_ref, p_ref, r_ref, o_ref):
        o_ref[...] = (p_ref[...].astype(F32) + r_ref[...].astype(F32)).astype(BF16)

    blk = (None, None, tr, c)
    grid_spec = pltpu.PrefetchScalarGridSpec(
        num_scalar_prefetch=1, grid=(l, j, nr),
        in_specs=[pl.BlockSpec(blk, lambda a, b, i, p: (a, b, p[1] * nr + i, 0)),
                  pl.BlockSpec(blk, lambda a, b, i, p: (a, b, i, 0))],
        out_specs=pl.BlockSpec(blk, lambda a, b, i, p: (a, b, i, 0)),
    )
    return pl.pallas_call(body, name=name, grid_spec=grid_spec, out_shape=jax.ShapeDtypeStruct(recv.shape, BF16),
                          compiler_params=_params(3))(place, part, recv)


def scatter_to_chips(name, sums):
    n = len(sums)

    def body(*refs):
        ins, outs = refs[:n], refs[n:2 * n]
        send_sems, recv_sems = refs[2 * n:]
        x, y, c, me, chips, chip_ids = _place()
        copies = []
        for t in range(n):
            for k in range(3):
                cp = _remote(ins[t].at[:, chip_ids[k]], outs[t].at[k], send_sems.at[3 * t + k], recv_sems.at[3 * t + k],
                             (*chips[k], c))
                cp.start()
                copies.append(cp)
        for cp in copies:
            cp.wait()

    any_spec = pl.BlockSpec(memory_space=pl.ANY)
    return pl.pallas_call(
        body, name=name, in_specs=[any_spec] * n, out_specs=[any_spec] * n,
        out_shape=[jax.ShapeDtypeStruct((3, a.shape[0], a.shape[2], a.shape[3]), a.dtype) for a in sums],
        scratch_shapes=[pltpu.SemaphoreType.DMA((3 * n,)), pltpu.SemaphoreType.DMA((3 * n,))],
        compiler_params=pltpu.CompilerParams(has_side_effects=True),
    )(*sums)


def reduce_own(name, part, recv1, recv2, place, *, tr=256):
    l, j, rh, c = recv1.shape
    tr = _tile(rh, tr, 16)
    nr = rh // tr

    def body(place_ref, p_ref, r1_ref, a_ref, b_ref, c_ref, o_ref):
        acc = p_ref[...].astype(F32) + r1_ref[...].astype(F32)
        acc = acc + a_ref[...].astype(F32)
        acc = acc + b_ref[...].astype(F32)
        o_ref[...] = acc + c_ref[...].astype(F32)

    blk = (None, None, tr, c)
    grid_spec = pltpu.PrefetchScalarGridSpec(
        num_scalar_prefetch=1, grid=(l, nr),
        in_specs=[pl.BlockSpec(blk, lambda a, i, p: (a, p[0], p[1] * nr + i, 0)),
                  pl.BlockSpec(blk, lambda a, i, p: (a, p[0], i, 0)),
                  pl.BlockSpec(blk, lambda a, i, p: (0, a, i, 0)),
                  pl.BlockSpec(blk, lambda a, i, p: (1, a, i, 0)),
                  pl.BlockSpec(blk, lambda a, i, p: (2, a, i, 0))],
        out_specs=pl.BlockSpec((None, tr, c), lambda a, i, p: (a, p[1] * nr + i, 0)),
    )
    return pl.pallas_call(body, name=name, grid_spec=grid_spec, out_shape=jax.ShapeDtypeStruct((l, 2 * rh, c), F32),
                          compiler_params=_params(2))(place, part, recv1, recv2, recv2, recv2)


def share_halves(name, grads):
    n = len(grads)

    def body(*refs):
        outs = refs[n:2 * n]
        send_sems, recv_sems = refs[2 * n:]
        x, y, c, me, chips, chip_ids = _place()
        copies = []
        for t in range(n):
            rh = outs[t].shape[1] // 2
            mine = outs[t].at[:, pl.ds(c * rh, rh), :]
            cp = _remote(mine, mine, send_sems.at[t], recv_sems.at[t], (x, y, 1 - c))
            cp.start()
            copies.append(cp)
        for t, cp in enumerate(copies):
            rh = outs[t].shape[1] // 2
            theirs = outs[t].at[:, pl.ds((1 - c) * rh, rh), :]
            cp.wait_send()
            _remote(theirs, theirs, send_sems.at[t], recv_sems.at[t], (x, y, 1 - c)).wait_recv()

    any_spec = pl.BlockSpec(memory_space=pl.ANY)
    return pl.pallas_call(
        body, name=name, in_specs=[any_spec] * n, out_specs=[any_spec] * n,
        out_shape=[jax.ShapeDtypeStruct(a.shape, a.dtype) for a in grads],
        scratch_shapes=[pltpu.SemaphoreType.DMA((n,)), pltpu.SemaphoreType.DMA((n,))],
        input_output_aliases={t: t for t in range(n)},
        compiler_params=pltpu.CompilerParams(has_side_effects=True),
    )(*grads)


def all_reduce_small(name, packed):
    r = packed.shape[0]

    def body(x_ref, o_ref, buf, send_sems, recv_sems):
        x, y, c = lax.axis_index("x"), lax.axis_index("y"), lax.axis_index("c")
        me = 4 * x + 2 * y + c
        buf[me] = x_ref[...]
        copies = []
        for k in range(1, 8):
            to = (x ^ (k >> 2), y ^ ((k >> 1) & 1), c ^ (k & 1))
            cp = _remote(x_ref, buf.at[me], send_sems.at[k - 1], recv_sems.at[k - 1], to)
            cp.start()
            copies.append(cp)
        for k in range(1, 8):
            peer = me ^ k
            _remote(x_ref, buf.at[peer], send_sems.at[k - 1], recv_sems.at[k - 1], (x, y, c)).wait_recv()
        for cp in copies:
            cp.wait_send()
        acc = buf[0]
        for dev in range(1, 8):
            acc = acc + buf[dev]
        o_ref[...] = acc

    vmem = pl.BlockSpec(memory_space=pltpu.VMEM)
    return pl.pallas_call(
        body, name=name, in_specs=[vmem], out_specs=vmem, out_shape=jax.ShapeDtypeStruct(packed.shape, F32),
        scratch_shapes=[pltpu.VMEM((8, r, HEAD), F32), pltpu.SemaphoreType.DMA((7,)), pltpu.SemaphoreType.DMA((7,))],
    )(packed)


def adamw(name, w, g, m, v, *, tr=256):
    l, r, c = w.shape
    tr = _tile(r, tr, 8)

    def body(w_ref, g_ref, m_ref, v_ref, d_ref, nm_ref, nv_ref):
        gv = g_ref[...]
        nm = ADAM_B1 * m_ref[...] + (1.0 - ADAM_B1) * gv
        nv = ADAM_B2 * v_ref[...] + (1.0 - ADAM_B2) * (gv * gv)
        m_hat = nm / (1.0 - ADAM_B1 ** ADAM_STEP)
        v_hat = nv / (1.0 - ADAM_B2 ** ADAM_STEP)
        d_ref[...] = -ADAM_LR * (m_hat / (jnp.sqrt(v_hat) + ADAM_EPS) + ADAM_WD * w_ref[...])
        nm_ref[...] = nm
        nv_ref[...] = nv

    blk = pl.BlockSpec((None, tr, c), lambda i, j: (i, j, 0))
    shape = jax.ShapeDtypeStruct(w.shape, F32)
    return pl.pallas_call(body, name=name, grid=(l, r // tr), in_specs=[blk] * 4, out_specs=[blk] * 3,
                          out_shape=[shape] * 3, compiler_params=_params(2))(w, g, m, v)


SHARDED = ('ffn1_wg', 'ffn1_wu', 'ffn1_wd', 'ffn2_wg', 'ffn2_wu', 'ffn2_wd', 'a_wqkv', 'a_wo', 'b_wdkv', 'b_wkr', 'b_wuk',
           'b_wuv', 'b_wdq', 'b_wuq', 'b_wo')
COLUMN_SHARDED = ('ffn1_wg', 'ffn1_wu', 'ffn2_wg', 'ffn2_wu', 'a_wqkv')
GAINS = ('ffn_norm1', 'mix_norm', 'ffn_norm2', 'kv_norm', 'b_ckv_norm', 'b_cq_norm', 'final_norm')
WEIGHTS = ('ffn_norm1', 'ffn1_wg', 'ffn1_wu', 'ffn1_wd', 'mix_norm', 'ffn_norm2', 'ffn2_wg', 'ffn2_wu', 'ffn2_wd', 'a_wqkv',
           'a_wo', 'kv_norm', 'b_wdkv', 'b_ckv_norm', 'b_wkr', 'b_wuk', 'b_wuv', 'b_wdq', 'b_cq_norm', 'b_wuq', 'b_wo',
           'final_norm')


def _to_comm_shape(name, a):
    if name == 'b_wkr':
        return jnp.pad(a, ((0, 0), (0, HEAD - ROPE)))[None]
    if name == 'b_wuq':
        l, r, h, _ = a.shape
        return jnp.pad(a, ((0, 0), (0, 0), (0, 0), (0, QK_PAD - HEAD - ROPE))).reshape(l, r, h * QK_PAD)
    if name in ('b_wuk', 'b_wuv'):
        return a.reshape(1, a.shape[0], -1)
    return a[None] if a.ndim == 2 else a


def _from_comm_shape(name, g, like):
    if name == 'b_wkr':
        return g[0, :, :ROPE]
    if name == 'b_wuq':
        l, r, h, e = like.shape
        return g.reshape(l, r, h, QK_PAD)[..., :e]
    return g.reshape(like.shape)


def _as3d(a):
    if a.ndim == 1:
        return a.reshape(1, 1, -1)
    if a.ndim == 2:
        return a[None]
    return a.reshape(a.shape[0], a.shape[1], -1)


def kernel(x, ffn_norm1, ffn1_wg, ffn1_wu, ffn1_wd, mix_norm, ffn_norm2, ffn2_wg, ffn2_wu, ffn2_wd, a_wqkv, a_wo, kv_norm, b_wdkv, b_ckv_norm, b_wkr, b_wuk, b_wuv, b_wdq, b_cq_norm, b_wuq, b_wo, final_norm, loss_target, m_ffn_norm1, m_ffn1_wg, m_ffn1_wu, m_ffn1_wd, m_mix_norm, m_ffn_norm2, m_ffn2_wg, m_ffn2_wu, m_ffn2_wd, m_a_wqkv, m_a_wo, m_kv_norm, m_b_wdkv, m_b_ckv_norm, m_b_wkr, m_b_wuk, m_b_wuv, m_b_wdq, m_b_cq_norm, m_b_wuq, m_b_wo, m_final_norm, v_ffn_norm1, v_ffn1_wg, v_ffn1_wu, v_ffn1_wd, v_mix_norm, v_ffn_norm2, v_ffn2_wg, v_ffn2_wu, v_ffn2_wd, v_a_wqkv, v_a_wo, v_kv_norm, v_b_wdkv, v_b_ckv_norm, v_b_wkr, v_b_wuk, v_b_wuv, v_b_wdq, v_b_cq_norm, v_b_wuq, v_b_wo, v_final_norm):
    args = dict(locals())
    weights = {k: args[k] for k in WEIGHTS}
    place = jnp.stack([2 * lax.axis_index("x") + lax.axis_index("y"), lax.axis_index("c")]).astype(jnp.int32)

    slabs = [cast_to_slab(f'cast_{k}', _to_comm_shape(k, weights[k]), place) for k in SHARDED]
    gathered = all_gather_weights('gather_weights', slabs)
    w = {}
    for k, g in zip(SHARDED, gathered):
        l, j, r, c = g.shape
        w[k] = g if k in COLUMN_SHARDED else g.reshape(l, j * r, c)
    norms = {k: weights[k] if weights[k].ndim == 2 else weights[k][None] for k in GAINS}

    loss, grad_x, gw, gn = local_step(x[0], loss_target[0], w, norms)
    loss = lax.psum(loss[0, 0], ("x", "y", "c"))

    parts = [gw[k].reshape(g.shape) for k, g in zip(SHARDED, gathered)]
    recv1 = exchange_halves('grads_to_sibling', parts)
    sums = [add_halves(f'add_{k}', p, r1, place) for k, p, r1 in zip(SHARDED, parts, recv1)]
    recv2 = scatter_to_chips('grads_to_chips', sums)
    halves = [reduce_own(f'reduce_{k}', p, r1, r2, place) for k, p, r1, r2 in zip(SHARDED, parts, recv1, recv2)]
    reduced = share_halves('grads_share', halves)
    grads = {k: _from_comm_shape(k, g, weights[k]) for k, g in zip(SHARDED, reduced)}

    flat = jnp.concatenate([gn[k].reshape(-1) for k in GAINS])
    rows = -(-flat.shape[0] // (8 * HEAD)) * 8
    packed = jnp.pad(flat, (0, rows * HEAD - flat.shape[0])).reshape(rows, HEAD)
    total = all_reduce_small('gain_grads', packed).reshape(-1)
    off = 0
    for k in GAINS:
        grads[k] = total[off:off + weights[k].size].reshape(weights[k].shape)
        off += weights[k].size

    deltas, new_m, new_v = {}, {}, {}
    for k in WEIGHTS:
        shape = weights[k].shape
        d, nm, nv = adamw(f'adamw_{k}', _as3d(weights[k]), _as3d(grads[k]), _as3d(args['m_' + k]), _as3d(args['v_' + k]))
        deltas[k], new_m[k], new_v[k] = d.reshape(shape), nm.reshape(shape), nv.reshape(shape)
    return (loss, grad_x[None], *[grads[k] for k in WEIGHTS], *[deltas[k] for k in WEIGHTS],
            *[new_m[k] for k in WEIGHTS], *[new_v[k] for k in WEIGHTS])
```

```python
import math

import numpy as np
import jax
import jax.numpy as jnp
from jax import lax
from jax.experimental import pallas as pl
from jax.experimental.pallas import tpu as pltpu

F32 = jnp.float32
BF16 = jnp.bfloat16
NORM_EPS = 1e-6
MASK_VALUE = -1e30
HEAD = 128
ROPE = 64
QK_PAD = 256
DILATED_BRANCHES = ((128, 1), (512, 4), (2048, 16))
ROPE_THETA = 10000.0
ADAM_LR, ADAM_B1, ADAM_B2, ADAM_EPS, ADAM_WD, ADAM_STEP = 0.001, 0.9, 0.999, 1e-08, 0.01, 10
VMEM_LIMIT_BYTES = 56 * 1024 * 1024
MM_ROWS, MM_COLS, MM_WIDE, MM_SHALLOW, MM_DEPTH = 512, 512, 1024, 2048, 8192
DILATED_HEADS_PER_STEP, MLA_FWD_HEADS_PER_STEP, MLA_BWD_HEADS_PER_STEP = 8, 4, 4
N_CHIPS = 4
MESH = pl.DeviceIdType.MESH


def _params(n_axes, vmem=VMEM_LIMIT_BYTES):
    return pltpu.CompilerParams(dimension_semantics=("arbitrary",) * n_axes, vmem_limit_bytes=vmem)


def _tile(n, pref, mult=128):
    if n <= pref:
        return n
    t = (pref // mult) * mult
    while t >= mult:
        if n % t == 0:
            return t
        t -= mult
    return n


def _mm_call(name, grid, ins, in_specs, out_shape, out_specs, n_a, n_b, terms, dims, acc_shapes, epilogue, into=None):
    n_in, n_out, nk = len(ins), len(out_shape), grid[2]
    aliases = {}
    if into is not None:
        ins = list(ins) + [into]
        in_specs = list(in_specs) + [pl.BlockSpec(memory_space=pl.ANY)]
        aliases = {n_in: 0}

    def body(*refs):
        in_refs, refs = refs[:n_in], refs[n_in + len(aliases):]
        out_refs, accs = refs[:n_out], refs[n_out:]
        a_refs, b_refs, extra = in_refs[:n_a], in_refs[n_a:n_a + n_b], in_refs[n_a + n_b:]
        k = pl.program_id(2)
        sums = [None] * len(acc_shapes)
        for t, ia, ib in terms:
            a = a_refs[ia][...].astype(BF16)
            b = b_refs[ib][...].astype(BF16)
            part = lax.dot_general(a, b, (dims, ((), ())), preferred_element_type=F32)
            sums[t] = part if sums[t] is None else sums[t] + part
        if nk == 1:
            epilogue(sums, extra, out_refs)
            return

        @pl.when(k == 0)
        def _():
            for acc, part in zip(accs, sums):
                acc[...] = part

        @pl.when(k > 0)
        def _():
            for acc, part in zip(accs, sums):
                acc[...] += part

        @pl.when(k == nk - 1)
        def _():
            epilogue([acc[...] for acc in accs], extra, out_refs)

    return pl.pallas_call(
        body, name=name, grid=grid, in_specs=in_specs, out_specs=out_specs, out_shape=out_shape,
        scratch_shapes=[pltpu.VMEM(s, F32) for s in acc_shapes] if nk > 1 else [], input_output_aliases=aliases,
        compiler_params=_params(3),
    )(*ins)


def _b_spec_nn(w, tk, tn):
    b, layer = w
    if b.ndim == 3:
        return pl.BlockSpec((None, tk, tn), lambda i, j, k: (layer, k, j))
    per = b.shape[3] // tn
    return pl.BlockSpec((None, None, tk, tn), lambda i, j, k: (layer, j // per, k, j % per))


def _rope_tile(z, cs, s1, s2):
    return z * cs + pltpu.roll(z, 96, 1) * s1 + pltpu.roll(z, 32, 1) * s2


def mm_nn(name, a, w, *, out_dtype, res=None, alpha=1.0, rope=None, tm=MM_ROWS, tn=None, tk=MM_DEPTH):
    m, kk = a.shape
    b = w[0]
    ns = b.shape[-1]
    n = b.shape[1] * ns if b.ndim == 4 else ns
    tm, tk = _tile(m, tm, 8), _tile(kk, tk)
    tn = _tile(ns, tn or (MM_WIDE if kk <= MM_SHALLOW else MM_COLS))
    if rope is not None and rope[0] == 'q':
        assert tn % QK_PAD == 0
    grid = (m // tm, n // tn, kk // tk)
    ins = [a, b]
    in_specs = [pl.BlockSpec((tm, tk), lambda i, j, k: (i, k)), _b_spec_nn(w, tk, tn)]
    if res is not None:
        ins.append(res)
        in_specs.append(pl.BlockSpec((tm, tn), lambda i, j, k: (i, j)))
    if rope is not None:
        ins += list(rope[1:])
        in_specs += [pl.BlockSpec((tm, HEAD), lambda i, j, k: (i, 0))] * 3

    def epilogue(accs, extra, outs):
        y = accs[0]
        extra = list(extra)
        if alpha != 1.0:
            y = alpha * y
        if res is not None:
            y = extra.pop(0)[...] + y
        if rope is not None:
            cs, s1, s2 = (e[...] for e in extra)
            step = QK_PAD if rope[0] == 'q' else HEAD
            for c0 in range(0, tn, step):
                if rope[0] == 'q':
                    outs[0][:, c0:c0 + HEAD] = y[:, c0:c0 + HEAD].astype(out_dtype)
                    c0 += HEAD
                outs[0][:, c0:c0 + HEAD] = _rope_tile(y[:, c0:c0 + HEAD], cs, s1, s2).astype(out_dtype)
        else:
            outs[0][...] = y.astype(out_dtype)

    return _mm_call(name, grid, ins, in_specs, [jax.ShapeDtypeStruct((m, n), out_dtype)],
                    [pl.BlockSpec((tm, tn), lambda i, j, k: (i, j))], 1, 1, [(0, 0, 0)], ((1,), (0,)),
                    [(tm, tn)], epilogue)[0]


def ffn_up(name, h, wg, wu, *, tm=MM_ROWS, tk=MM_DEPTH):
    m, kk = h.shape
    fs = wg[0].shape[3]
    n = wg[0].shape[1] * fs
    tm, tk, tn = _tile(m, tm, 8), _tile(kk, tk), fs
    grid = (m // tm, n // tn, kk // tk)

    def epilogue(accs, extra, outs):
        g, u = accs
        outs[0][...] = g.astype(BF16)
        outs[1][...] = u.astype(BF16)
        outs[2][...] = (g * jax.nn.sigmoid(g) * u).astype(BF16)

    o_spec = pl.BlockSpec((tm, tn), lambda i, j, k: (i, j))
    return _mm_call(name, grid, [h, wg[0], wu[0]],
                    [pl.BlockSpec((tm, tk), lambda i, j, k: (i, k)), _b_spec_nn(wg, tk, tn), _b_spec_nn(wu, tk, tn)],
                    [jax.ShapeDtypeStruct((m, n), BF16)] * 3, [o_spec] * 3, 1, 2, [(0, 0, 0), (1, 0, 1)],
                    ((1,), (0,)), [(tm, tn)] * 2, epilogue)


def mm_nt(name, a_list, w_list, *, out_dtype, alpha=1.0, swiglu=None, res=None, tm=MM_ROWS, tn=None, tk=MM_DEPTH):
    m, n = a_list[0].shape
    b0 = w_list[0][0]
    slab = b0.ndim == 4
    kk = b0.shape[-2]
    ns = b0.shape[-1]
    tk = ns if slab else _tile(ns, tk)
    wide = slab or tk <= MM_SHALLOW
    tm, tn = _tile(m, tm, 8), _tile(kk, tn or (MM_WIDE if wide else MM_COLS))
    grid = (m // tm, kk // tn, n // tk)

    def b_spec(w):
        layer = w[1]
        if slab:
            per = ns // tk
            return pl.BlockSpec((None, None, tn, tk), lambda i, j, k: (layer, k // per, j, k % per))
        return pl.BlockSpec((None, tn, tk), lambda i, j, k: (layer, j, k))

    p = len(a_list)
    ins = list(a_list) + [w[0] for w in w_list]
    in_specs = [pl.BlockSpec((tm, tk), lambda i, j, k: (i, k))] * p + [b_spec(w) for w in w_list]
    o_spec = pl.BlockSpec((tm, tn), lambda i, j, k: (i, j))
    if swiglu is not None:
        ins += list(swiglu)
        in_specs += [o_spec, o_spec]
        out_shape = [jax.ShapeDtypeStruct((m, kk), BF16)] * 2

        def epilogue(accs, extra, outs):
            da = alpha * accs[0]
            g = extra[0][...].astype(F32)
            u = extra[1][...].astype(F32)
            sg = jax.nn.sigmoid(g)
            silu = g * sg
            outs[0][...] = (da * u * (sg + silu * (1.0 - sg))).astype(BF16)
            outs[1][...] = (da * silu).astype(BF16)
    else:
        out_shape = [jax.ShapeDtypeStruct((m, kk), out_dtype)]
        if res is not None:
            ins.append(res)
            in_specs.append(o_spec)

        def epilogue(accs, extra, outs):
            y = alpha * accs[0]
            if res is not None:
                y = y + extra[0][...]
            outs[0][...] = y.astype(out_dtype)

    outs = _mm_call(name, grid, ins, in_specs, out_shape, [o_spec] * len(out_shape), p, p,
                    [(0, q, q) for q in range(p)], ((1,), (1,)), [(tm, tn)], epilogue)
    return outs if swiglu is not None else outs[0]


def mm_tn(name, a, b, into, *, alpha=1.0, tm=MM_ROWS, tn=MM_WIDE, tk=MM_SHALLOW):
    buf, layer = into
    m, kk = a.shape
    n = b.shape[1]
    ns = buf.shape[-1]
    tm, tk = _tile(kk, tm), _tile(m, tk, 8)
    tn = ns if ns % 512 else _tile(ns, tn)
    grid = (kk // tm, n // tn, m // tk)
    if buf.ndim == 4:
        per = ns // tn
        o_spec = pl.BlockSpec((None, None, tm, tn), lambda i, j, k: (layer, j // per, i, j % per))
    else:
        o_spec = pl.BlockSpec((None, tm, tn), lambda i, j, k: (layer, i, j))

    def epilogue(accs, extra, outs):
        outs[0][...] = (alpha * accs[0]).astype(buf.dtype)

    return _mm_call(name, grid, [a, b],
                    [pl.BlockSpec((tk, tm), lambda i, j, k: (k, i)), pl.BlockSpec((tk, tn), lambda i, j, k: (k, j))],
                    [jax.ShapeDtypeStruct(buf.shape, buf.dtype)], [o_spec], 1, 1, [(0, 0, 0)], ((0,), (0,)),
                    [(tm, tn)], epilogue, into=buf)[0]


def norm_fwd(name, x, gain, *, tm=256):
    m, d = x.shape
    tm = _tile(m, tm, 8)

    def body(x_ref, g_ref, o_ref):
        xv = x_ref[...]
        r = lax.rsqrt(jnp.mean(xv * xv, axis=-1, keepdims=True) + NORM_EPS)
        o_ref[...] = (xv * r * g_ref[...]).astype(BF16)

    return pl.pallas_call(
        body, name=name, grid=(m // tm,),
        in_specs=[pl.BlockSpec((tm, d), lambda i: (i, 0)), pl.BlockSpec((1, d), lambda i: (0, 0))],
        out_specs=pl.BlockSpec((tm, d), lambda i: (i, 0)), out_shape=jax.ShapeDtypeStruct((m, d), BF16),
        compiler_params=_params(1),
    )(x, gain)


def norm_bwd(name, x, gain, dh, dres=None, *, tm=256):
    m, d = x.shape
    tm = _tile(m, tm, 8)
    n_steps = m // tm
    has_res = dres is not None

    def body(*refs):
        if has_res:
            x_ref, g_ref, dh_ref, dres_ref, dx_ref, dxb_ref, dg_ref, acc = refs
        else:
            x_ref, g_ref, dh_ref, dx_ref, dxb_ref, dg_ref, acc = refs
        i = pl.program_id(0)
        xv = x_ref[...]
        dy = dh_ref[...].astype(F32)
        r = lax.rsqrt(jnp.mean(xv * xv, axis=-1, keepdims=True) + NORM_EPS)
        xhat = xv * r
        dxhat = dy * g_ref[...]
        dx = r * (dxhat - xhat * jnp.mean(dxhat * xhat, axis=-1, keepdims=True))
        if has_res:
            dx = dx + dres_ref[...]
        dx_ref[...] = dx
        dxb_ref[...] = dx.astype(BF16)

        @pl.when(i == 0)
        def _():
            acc[...] = jnp.zeros_like(acc)

        acc[...] += jnp.sum((dy * xhat).reshape(tm // 8, 8, d), axis=0)

        @pl.when(i == n_steps - 1)
        def _():
            dg_ref[...] = jnp.sum(acc[...], axis=0, keepdims=True)

    row = pl.BlockSpec((tm, d), lambda i: (i, 0))
    vec = pl.BlockSpec((1, d), lambda i: (0, 0))
    ins = [x, gain, dh] + ([dres] if has_res else [])
    return pl.pallas_call(
        body, name=name, grid=(n_steps,), in_specs=[row, vec, row] + ([row] if has_res else []),
        out_specs=[row, row, vec],
        out_shape=[jax.ShapeDtypeStruct((m, d), F32), jax.ShapeDtypeStruct((m, d), BF16), jax.ShapeDtypeStruct((1, d), F32)],
        scratch_shapes=[pltpu.VMEM((8, d), F32)], compiler_params=_params(1),
    )(*ins)


def loss_head(name, x, gain, target, *, tm=256):
    m, d = x.shape
    tm = _tile(m, tm, 8)
    n_steps = m // tm

    def body(x_ref, g_ref, t_ref, loss_ref, dx_ref, dxb_ref, dg_ref, acc, lacc):
        i = pl.program_id(0)
        xv = x_ref[...]
        g = g_ref[...]
        r = lax.rsqrt(jnp.mean(xv * xv, axis=-1, keepdims=True) + NORM_EPS)
        xhat = xv * r
        err = xhat * g - t_ref[...]
        dy = err * (1.0 / d)
        dxhat = dy * g
        dx = r * (dxhat - xhat * jnp.mean(dxhat * xhat, axis=-1, keepdims=True))
        dx_ref[...] = dx
        dxb_ref[...] = dx.astype(BF16)

        @pl.when(i == 0)
        def _():
            acc[...] = jnp.zeros_like(acc)
            lacc[...] = jnp.zeros_like(lacc)

        acc[...] += jnp.sum((dy * xhat).reshape(tm // 8, 8, d), axis=0)
        lacc[...] += jnp.sum((err * err).reshape(tm // 8, 8, d), axis=0)

        @pl.when(i == n_steps - 1)
        def _():
            dg_ref[...] = jnp.sum(acc[...], axis=0, keepdims=True)
            loss_ref[...] = (0.5 / d) * jnp.sum(jnp.sum(lacc[...], axis=0, keepdims=True), axis=1, keepdims=True)

    row = pl.BlockSpec((tm, d), lambda i: (i, 0))
    vec = pl.BlockSpec((1, d), lambda i: (0, 0))
    return pl.pallas_call(
        body, name=name, grid=(n_steps,), in_specs=[row, vec, row],
        out_specs=[pl.BlockSpec((1, 1), lambda i: (0, 0)), row, row, vec],
        out_shape=[jax.ShapeDtypeStruct((1, 1), F32), jax.ShapeDtypeStruct((m, d), F32),
                   jax.ShapeDtypeStruct((m, d), BF16), jax.ShapeDtypeStruct((1, d), F32)],
        scratch_shapes=[pltpu.VMEM((8, d), F32), pltpu.VMEM((8, d), F32)], compiler_params=_params(1),
    )(x, gain, target)


def _lane_is(h, shape):
    return lax.broadcasted_iota(jnp.int32, shape, 1) == h


def _take_lane(tile, h):
    return jnp.sum(jnp.where(_lane_is(h, tile.shape), tile, 0.0), axis=1, keepdims=True)


def _merge_lanes(dst, src, start, count):
    lane = lax.broadcasted_iota(jnp.int32, dst.shape, 1)
    return jnp.where((lane >= start) & (lane < start + count), src, dst)


def _band_scores(q, kp, kc, h, c, dilation, n_heads):
    nt = (((1,), (1,)), ((), ()))
    scale = HEAD ** -0.5
    i = lax.broadcasted_iota(jnp.int32, (HEAD, HEAD), 0)
    j = lax.broadcasted_iota(jnp.int32, (HEAD, HEAD), 1)
    slope = jnp.exp(jnp.full((HEAD, HEAD), -8.0 * math.log(2.0) / n_heads, F32) * (h + 1).astype(F32))
    dist = (i - j).astype(F32) * float(dilation)
    s_c = lax.dot_general(q, kc, nt, preferred_element_type=F32) * scale - slope * dist
    s_p = lax.dot_general(q, kp, nt, preferred_element_type=F32) * scale - slope * (dist + float(dilation * HEAD))
    s_c = jnp.where(j <= i, s_c, MASK_VALUE)
    s_p = jnp.where((j >= i) & (c > 0), s_p, MASK_VALUE)
    return s_p, s_c


def dilated_fwd(name, qkv, state, dilation, n_heads, last):
    s, w3 = qkv.shape
    w = w3 // 3
    d = dilation
    rows, nb = s // d, s // (d * HEAD)
    first = state is None
    qkv_v = qkv.reshape(rows, d * w3)
    hb = min(DILATED_HEADS_PER_STEP, n_heads)
    groups = n_heads // hb

    def body(*refs):
        q_ref, kp_ref, kc_ref, vp_ref, vc_ref = refs[:5]
        refs = refs[5:]
        if not first:
            acc_in, m_in, l_in = refs[:3]
            refs = refs[3:]
        if last:
            o_ref, lse_ref = refs
        else:
            acc_out, m_out, l_out = refs
        c, g = pl.program_id(1), pl.program_id(2)
        if first:
            m_tile = l_tile = jnp.zeros((HEAD, HEAD), F32)
        else:
            m_tile, l_tile = m_in[...], l_in[...]
        stat_a, stat_b = m_tile, l_tile
        for hh in range(hb):
            h = g * hb + hh
            cs = slice(hh * HEAD, (hh + 1) * HEAD)
            s_p, s_c = _band_scores(q_ref[:, cs], kp_ref[:, cs], kc_ref[:, cs], h, c, d, n_heads)
            m_new = jnp.maximum(jnp.max(s_c, axis=1, keepdims=True), jnp.max(s_p, axis=1, keepdims=True))
            if not first:
                m_prev = _take_lane(m_tile, h)
                m_new = jnp.maximum(m_new, m_prev)
            p_c = jnp.exp(s_c - m_new)
            p_p = jnp.exp(s_p - m_new)
            l_new = jnp.sum(p_c, axis=1, keepdims=True) + jnp.sum(p_p, axis=1, keepdims=True)
            nn = (((1,), (0,)), ((), ()))
            acc = (lax.dot_general(p_c.astype(BF16), vc_ref[:, cs], nn, preferred_element_type=F32)
                   + lax.dot_general(p_p.astype(BF16), vp_ref[:, cs], nn, preferred_element_type=F32))
            if not first:
                corr = jnp.exp(m_prev - m_new)
                l_new = l_new + corr * _take_lane(l_tile, h)
                acc = acc + corr * acc_in[:, cs]
            lane = _lane_is(h, (HEAD, HEAD))
            if last:
                o_ref[:, cs] = (acc / l_new).astype(BF16)
                stat_a = jnp.where(lane, m_new + jnp.log(l_new), stat_a)
            else:
                acc_out[:, cs] = acc
                stat_a = jnp.where(lane, m_new, stat_a)
                stat_b = jnp.where(lane, l_new, stat_b)
        if last:
            @pl.when(g == 0)
            def _():
                lse_ref[...] = stat_a

            @pl.when(g > 0)
            def _():
                lse_ref[...] = _merge_lanes(lse_ref[...], stat_a, g * hb, hb)
        else:
            @pl.when(g == 0)
            def _():
                m_out[...] = stat_a
                l_out[...] = stat_b

            @pl.when(g > 0)
            def _():
                m_out[...] = _merge_lanes(m_out[...], stat_a, g * hb, hb)
                l_out[...] = _merge_lanes(l_out[...], stat_b, g * hb, hb)

    blk = (HEAD, hb * HEAD)
    prev = lambda c: jnp.maximum(c - 1, 0)
    in_specs = [
        pl.BlockSpec(blk, lambda r, c, g: (c, r * 3 * groups + g)),
        pl.BlockSpec(blk, lambda r, c, g: (prev(c), r * 3 * groups + groups + g)),
        pl.BlockSpec(blk, lambda r, c, g: (c, r * 3 * groups + groups + g)),
        pl.BlockSpec(blk, lambda r, c, g: (prev(c), r * 3 * groups + 2 * groups + g)),
        pl.BlockSpec(blk, lambda r, c, g: (c, r * 3 * groups + 2 * groups + g)),
    ]
    head_spec = pl.BlockSpec(blk, lambda r, c, g: (c, r * groups + g))
    stat_spec = pl.BlockSpec((HEAD, HEAD), lambda r, c, g: (c, r))
    ins = [qkv_v] * 5
    if not first:
        acc0, m0, l0 = state
        ins += [acc0.reshape(rows, d * w), m0.reshape(rows, d * HEAD), l0.reshape(rows, d * HEAD)]
        in_specs += [head_spec, stat_spec, stat_spec]
    stat_shape = jax.ShapeDtypeStruct((rows, d * HEAD), F32)
    if last:
        out_shape = [jax.ShapeDtypeStruct((rows, d * w), BF16), stat_shape]
        out_specs = [head_spec, stat_spec]
    else:
        out_shape = [jax.ShapeDtypeStruct((rows, d * w), F32), stat_shape, stat_shape]
        out_specs = [head_spec, stat_spec, stat_spec]
    outs = pl.pallas_call(body, name=name, grid=(d, nb, groups), in_specs=in_specs, out_specs=out_specs,
                          out_shape=out_shape, compiler_params=_params(3))(*ins)
    if last:
        return outs[0].reshape(s, w), outs[1].reshape(s, HEAD)
    return outs[0].reshape(s, w), outs[1].reshape(s, HEAD), outs[2].reshape(s, HEAD)


def dilated_bwd(name, qkv, o, do, lse, dilation, n_heads):
    s, w3 = qkv.shape
    w = w3 // 3
    d = dilation
    rows, nb = s // d, s // (d * HEAD)
    hb = min(DILATED_HEADS_PER_STEP, n_heads)
    groups = n_heads // hb

    def body(q_ref, kp_ref, kc_ref, vp_ref, vc_ref, o_ref, do_ref, lse_ref, dq_ref, dk_ref, dv_ref, dk_c, dv_c):
        g, c = pl.program_id(1), pl.program_id(2)

        @pl.when(c == 0)
        def _():
            dk_c[...] = jnp.zeros_like(dk_c)
            dv_c[...] = jnp.zeros_like(dv_c)

        @pl.when(c < nb)
        def _():
            lse_tile = lse_ref[...]
            nt = (((1,), (1,)), ((), ()))
            nn = (((1,), (0,)), ((), ()))
            tn = (((0,), (0,)), ((), ()))
            scale = HEAD ** -0.5
            for hh in range(hb):
                h = g * hb + hh
                cs = slice(hh * HEAD, (hh + 1) * HEAD)
                q, kp, kc, vp, vc, dout = q_ref[:, cs], kp_ref[:, cs], kc_ref[:, cs], vp_ref[:, cs], vc_ref[:, cs], do_ref[:, cs]
                s_p, s_c = _band_scores(q, kp, kc, h, c, d, n_heads)
                lse_h = _take_lane(lse_tile, h)
                delta = jnp.sum(dout.astype(F32) * o_ref[:, cs].astype(F32), axis=1, keepdims=True)
                p_c = jnp.exp(s_c - lse_h)
                p_p = jnp.exp(s_p - lse_h)
                ds_c = (p_c * (lax.dot_general(dout, vc, nt, preferred_element_type=F32) - delta) * scale).astype(BF16)
                ds_p = (p_p * (lax.dot_general(dout, vp, nt, preferred_element_type=F32) - delta) * scale).astype(BF16)
                dq_ref[:, cs] = (lax.dot_general(ds_c, kc, nn, preferred_element_type=F32)
                                 + lax.dot_general(ds_p, kp, nn, preferred_element_type=F32)).astype(BF16)
                dk_ref[:, cs] = (dk_c[:, cs] + lax.dot_general(ds_p, q, tn, preferred_element_type=F32)).astype(BF16)
                dv_ref[:, cs] = (dv_c[:, cs]
                                 + lax.dot_general(p_p.astype(BF16), dout, tn, preferred_element_type=F32)).astype(BF16)
                dk_c[:, cs] = lax.dot_general(ds_c, q, tn, preferred_element_type=F32)
                dv_c[:, cs] = lax.dot_general(p_c.astype(BF16), dout, tn, preferred_element_type=F32)

        @pl.when(c == nb)
        def _():
            dk_ref[...] = dk_c[...].astype(BF16)
            dv_ref[...] = dv_c[...].astype(BF16)

    blk = (HEAD, hb * HEAD)
    cur = lambda c: jnp.minimum(c, nb - 1)
    prev = lambda c: jnp.clip(c - 1, 0, nb - 1)
    in_specs = [
        pl.BlockSpec(blk, lambda r, g, c: (cur(c), r * 3 * groups + g)),
        pl.BlockSpec(blk, lambda r, g, c: (prev(c), r * 3 * groups + groups + g)),
        pl.BlockSpec(blk, lambda r, g, c: (cur(c), r * 3 * groups + groups + g)),
        pl.BlockSpec(blk, lambda r, g, c: (prev(c), r * 3 * groups + 2 * groups + g)),
        pl.BlockSpec(blk, lambda r, g, c: (cur(c), r * 3 * groups + 2 * groups + g)),
        pl.BlockSpec(blk, lambda r, g, c: (cur(c), r * groups + g)),
        pl.BlockSpec(blk, lambda r, g, c: (cur(c), r * groups + g)),
        pl.BlockSpec((HEAD, HEAD), lambda r, g, c: (cur(c), r)),
    ]
    out_specs = [
        pl.BlockSpec(blk, lambda r, g, c: (cur(c), r * groups + g)),
        pl.BlockSpec(blk, lambda r, g, c: (prev(c), r * groups + g)),
        pl.BlockSpec(blk, lambda r, g, c: (prev(c), r * groups + g)),
    ]
    qkv_v = qkv.reshape(rows, d * w3)
    outs = pl.pallas_call(
        body, name=name, grid=(d, groups, nb + 1), in_specs=in_specs, out_specs=out_specs,
        out_shape=[jax.ShapeDtypeStruct((rows, d * w), BF16)] * 3,
        scratch_shapes=[pltpu.VMEM(blk, F32)] * 2, compiler_params=_params(3),
    )(qkv_v, qkv_v, qkv_v, qkv_v, qkv_v, o.reshape(rows, d * w), do.reshape(rows, d * w), lse.reshape(rows, d * HEAD))
    return [t.reshape(s, w) for t in outs]


def sum_branches(name, parts, *, tm=256):
    s, w = parts[0][0].shape
    tm = _tile(s, tm, 8)

    def body(*refs):
        out = refs[9]
        for t in range(3):
            acc = refs[t][...].astype(F32) + refs[3 + t][...].astype(F32) + refs[6 + t][...].astype(F32)
            out[:, t * w:(t + 1) * w] = acc.astype(BF16)

    row = pl.BlockSpec((tm, w), lambda i: (i, 0))
    return pl.pallas_call(
        body, name=name, grid=(s // tm,), in_specs=[row] * 9, out_specs=pl.BlockSpec((tm, 3 * w), lambda i: (i, 0)),
        out_shape=jax.ShapeDtypeStruct((s, 3 * w), BF16), compiler_params=_params(1),
    )(*[t for trip in parts for t in trip])


def _triangle(nq, n_heads, by_key):
    a, hh, b = [], [], []
    for outer in range(nq):
        for h in range(n_heads):
            inner = range(outer, nq) if by_key else range(outer + 1)
            for t in inner:
                a.append(outer)
                hh.append(h)
                b.append(t)
    return (jnp.asarray(np.array(a, np.int32)), jnp.asarray(np.array(hh, np.int32)), jnp.asarray(np.array(b, np.int32)))


def _causal_mask(qi, ki, t):
    row = lax.broadcasted_iota(jnp.int32, (t, t), 0) + qi * t
    col = lax.broadcasted_iota(jnp.int32, (t, t), 1) + ki * t
    return col <= row


def _causal_scores(q, kn, kr, mask):
    nt = (((1,), (1,)), ((), ()))
    k = jnp.concatenate([kn, kr], axis=1)
    s = lax.dot_general(q, k, nt, preferred_element_type=F32)
    return s if mask is None else jnp.where(mask, s, MASK_VALUE)


def _on_and_off_diagonal(qi, ki, t, step):
    @pl.when(ki == qi)
    def _():
        step(_causal_mask(qi, ki, t))

    @pl.when(ki != qi)
    def _():
        step(None)


def _heads(hh):
    return slice(hh * QK_PAD, (hh + 1) * QK_PAD), slice(hh * HEAD, (hh + 1) * HEAD)


def mla_fwd(name, q, kn, kr, v, n_heads, *, t=512):
    s = q.shape[0]
    t = _tile(s, t)
    nq = s // t
    hb = min(MLA_FWD_HEADS_PER_STEP, n_heads)
    qt, gt, kt = _triangle(nq, n_heads // hb, by_key=False)

    def body(qt_ref, gt_ref, kt_ref, q_ref, kn_ref, kr_ref, v_ref, o_ref, lse_ref, m_sc, acc_sc):
        step = pl.program_id(0)
        qi, g, ki = qt_ref[step], gt_ref[step], kt_ref[step]

        @pl.when(ki == 0)
        def _():
            m_sc[...] = jnp.full_like(m_sc, MASK_VALUE)
            acc_sc[...] = jnp.zeros_like(acc_sc)

        def scores_step(mask):
            kr_v = kr_ref[...]
            ones = jnp.ones((t, HEAD), BF16)
            for hh in range(hb):
                wide, cs = _heads(hh)
                sc = _causal_scores(q_ref[:, wide], kn_ref[:, cs], kr_v, mask)
                m_prev = m_sc[hh]
                m_new = jnp.maximum(m_prev, jnp.max(sc, axis=1, keepdims=True))
                p = jnp.exp2(sc - m_new)
                v_ext = jnp.concatenate([v_ref[:, cs], ones], axis=1)
                acc_sc[hh] = jnp.exp2(m_prev - m_new) * acc_sc[hh] + lax.dot_general(
                    p.astype(BF16), v_ext, (((1,), (0,)), ((), ())), preferred_element_type=F32)
                m_sc[hh] = m_new

        _on_and_off_diagonal(qi, ki, t, scores_step)

        @pl.when(ki == qi)
        def _():
            @pl.when(g == 0)
            def _():
                lse_ref[...] = jnp.zeros_like(lse_ref)

            tile = lse_ref[...]
            for hh in range(hb):
                acc = acc_sc[hh]
                l = acc[:, HEAD:]
                o_ref[:, _heads(hh)[1]] = (acc[:, :HEAD] / l).astype(BF16)
                tile = jnp.where(_lane_is(g * hb + hh, (t, HEAD)), m_sc[hh] + jnp.log2(l), tile)
            lse_ref[...] = tile

    grid_spec = pltpu.PrefetchScalarGridSpec(
        num_scalar_prefetch=3, grid=(int(qt.shape[0]),),
        in_specs=[
            pl.BlockSpec((t, hb * QK_PAD), lambda i, a, b, c: (a[i], b[i])),
            pl.BlockSpec((t, hb * HEAD), lambda i, a, b, c: (c[i], b[i])),
            pl.BlockSpec((t, HEAD), lambda i, a, b, c: (c[i], 0)),
            pl.BlockSpec((t, hb * HEAD), lambda i, a, b, c: (c[i], b[i])),
        ],
        out_specs=[pl.BlockSpec((t, hb * HEAD), lambda i, a, b, c: (a[i], b[i])),
                   pl.BlockSpec((t, HEAD), lambda i, a, b, c: (a[i], 0))],
        scratch_shapes=[pltpu.VMEM((hb, t, 1), F32), pltpu.VMEM((hb, t, 2 * HEAD), F32)],
    )
    return pl.pallas_call(
        body, name=name, grid_spec=grid_spec,
        out_shape=[jax.ShapeDtypeStruct((s, n_heads * HEAD), BF16), jax.ShapeDtypeStruct((s, HEAD), F32)],
        compiler_params=_params(1),
    )(qt, gt, kt, q, kn, kr, v)


def mla_bwd_dq(name, q, kn, kr, v, o, do, lse, tabs, n_heads, *, t=512):
    s = q.shape[0]
    t = _tile(s, t)
    nq = s // t
    hb = min(MLA_BWD_HEADS_PER_STEP, n_heads)
    qt, gt, kt = _triangle(nq, n_heads // hb, by_key=False)

    def body(qt_ref, gt_ref, kt_ref, q_ref, kn_ref, kr_ref, v_ref, o_ref, do_ref, lse_ref, cs_ref, s1_ref, s2_ref,
             dq_ref, delta_ref, acc_sc, delta_sc):
        step = pl.program_id(0)
        qi, g, ki = qt_ref[step], gt_ref[step], kt_ref[step]

        @pl.when(ki == 0)
        def _():
            acc_sc[...] = jnp.zeros_like(acc_sc)

            @pl.when(g == 0)
            def _():
                delta_ref[...] = jnp.zeros_like(delta_ref)

            tile = delta_ref[...]
            for hh in range(hb):
                cs = _heads(hh)[1]
                delta = jnp.sum(do_ref[:, cs].astype(F32) * o_ref[:, cs].astype(F32), axis=1, keepdims=True)
                delta_sc[hh] = delta
                tile = jnp.where(_lane_is(g * hb + hh, (t, HEAD)), delta, tile)
            delta_ref[...] = tile

        def scores_step(mask):
            kr_v, lse_tile = kr_ref[...], lse_ref[...]
            for hh in range(hb):
                wide, cs = _heads(hh)
                kn_v = kn_ref[:, cs]
                sc = _causal_scores(q_ref[:, wide], kn_v, kr_v, mask)
                p = jnp.exp2(sc - _take_lane(lse_tile, g * hb + hh))
                dp = lax.dot_general(do_ref[:, cs], v_ref[:, cs], (((1,), (1,)), ((), ())), preferred_element_type=F32)
                ds = (p * (dp - delta_sc[hh])).astype(BF16)
                k = jnp.concatenate([kn_v, kr_v], axis=1)
                acc_sc[hh] += lax.dot_general(ds, k, (((1,), (0,)), ((), ())), preferred_element_type=F32)

        _on_and_off_diagonal(qi, ki, t, scores_step)

        @pl.when(ki == qi)
        def _():
            for hh in range(hb):
                dq = acc_sc[hh]
                c0 = hh * QK_PAD
                dq_ref[:, c0:c0 + HEAD] = dq[:, :HEAD].astype(BF16)
                dq_ref[:, c0 + HEAD:c0 + QK_PAD] = _rope_tile(dq[:, HEAD:], cs_ref[...], s1_ref[...],
                                                              s2_ref[...]).astype(BF16)

    tab = pl.BlockSpec((t, HEAD), lambda i, a, b, c: (a[i], 0))
    grid_spec = pltpu.PrefetchScalarGridSpec(
        num_scalar_prefetch=3, grid=(int(qt.shape[0]),),
        in_specs=[
            pl.BlockSpec((t, hb * QK_PAD), lambda i, a, b, c: (a[i], b[i])),
            pl.BlockSpec((t, hb * HEAD), lambda i, a, b, c: (c[i], b[i])),
            pl.BlockSpec((t, HEAD), lambda i, a, b, c: (c[i], 0)),
            pl.BlockSpec((t, hb * HEAD), lambda i, a, b, c: (c[i], b[i])),
            pl.BlockSpec((t, hb * HEAD), lambda i, a, b, c: (a[i], b[i])),
            pl.BlockSpec((t, hb * HEAD), lambda i, a, b, c: (a[i], b[i])),
            tab, tab, tab, tab,
        ],
        out_specs=[pl.BlockSpec((t, hb * QK_PAD), lambda i, a, b, c: (a[i], b[i])), tab],
        scratch_shapes=[pltpu.VMEM((hb, t, QK_PAD), F32), pltpu.VMEM((hb, t, 1), F32)],
    )
    return pl.pallas_call(
        body, name=name, grid_spec=grid_spec,
        out_shape=[jax.ShapeDtypeStruct((s, n_heads * QK_PAD), BF16), jax.ShapeDtypeStruct((s, HEAD), F32)],
        compiler_params=_params(1),
    )(qt, gt, kt, q, kn, kr, v, o, do, lse, *tabs)


def mla_bwd_dkv(name, q, kn, kr, v, do, lse, delta, prev, n_heads, inv_q_scale, inv_do_scale, *, t=512):
    s = q.shape[0]
    t = _tile(s, t)
    nq = s // t
    hb = min(MLA_BWD_HEADS_PER_STEP, n_heads)
    kt, gt, qt = _triangle(nq, n_heads // hb, by_key=True)
    has_prev = prev is not None

    def body(*refs):
        kt_ref, gt_ref, qt_ref, q_ref, kn_ref, kr_ref, v_ref, do_ref, lse_ref, delta_ref = refs[:10]
        refs = refs[10:]
        if has_prev:
            pkn_ref, pkr_ref, pv_ref = refs[:3]
            refs = refs[3:]
        dkn_ref, dkr_ref, dv_ref, dk_sc, dv_sc = refs
        step = pl.program_id(0)
        ki, g, qi = kt_ref[step], gt_ref[step], qt_ref[step]

        @pl.when(qi == ki)
        def _():
            dk_sc[...] = jnp.zeros_like(dk_sc)
            dv_sc[...] = jnp.zeros_like(dv_sc)

        def scores_step(mask):
            kr_v, lse_tile, delta_tile = kr_ref[...], lse_ref[...], delta_ref[...]
            tn = (((0,), (0,)), ((), ()))
            for hh in range(hb):
                wide, cs = _heads(hh)
                q_v, dout = q_ref[:, wide], do_ref[:, cs]
                sc = _causal_scores(q_v, kn_ref[:, cs], kr_v, mask)
                p = jnp.exp2(sc - _take_lane(lse_tile, g * hb + hh))
                dp = lax.dot_general(dout, v_ref[:, cs], (((1,), (1,)), ((), ())), preferred_element_type=F32)
                ds = (p * (dp - _take_lane(delta_tile, g * hb + hh))).astype(BF16)
                dk_sc[hh] += lax.dot_general(ds, q_v, tn, preferred_element_type=F32)
                dv_sc[hh] += lax.dot_general(p.astype(BF16), dout, tn, preferred_element_type=F32)

        _on_and_off_diagonal(qi, ki, t, scores_step)

        @pl.when(qi == nq - 1)
        def _():
            @pl.when(g == 0)
            def _():
                dkr_ref[...] = pkr_ref[...] if has_prev else jnp.zeros_like(dkr_ref)

            dkr = dkr_ref[...]
            for hh in range(hb):
                cs = _heads(hh)[1]
                dk = dk_sc[hh] * inv_q_scale
                dkn, dv = dk[:, :HEAD], dv_sc[hh] * inv_do_scale
                if has_prev:
                    dkn = dkn + pkn_ref[:, cs].astype(F32)
                    dv = dv + pv_ref[:, cs].astype(F32)
                dkn_ref[:, cs] = dkn.astype(BF16)
                dv_ref[:, cs] = dv.astype(BF16)
                dkr = dkr + dk[:, HEAD:]
            dkr_ref[...] = dkr

    head_q = pl.BlockSpec((t, hb * HEAD), lambda i, a, b, c: (c[i], b[i]))
    head_k = pl.BlockSpec((t, hb * HEAD), lambda i, a, b, c: (a[i], b[i]))
    shared_k = pl.BlockSpec((t, HEAD), lambda i, a, b, c: (a[i], 0))
    in_specs = [
        pl.BlockSpec((t, hb * QK_PAD), lambda i, a, b, c: (c[i], b[i])), head_k, shared_k, head_k, head_q,
        pl.BlockSpec((t, HEAD), lambda i, a, b, c: (c[i], 0)), pl.BlockSpec((t, HEAD), lambda i, a, b, c: (c[i], 0)),
    ]
    ins = [q, kn, kr, v, do, lse, delta]
    if has_prev:
        in_specs += [head_k, shared_k, head_k]
        ins += list(prev)
    grid_spec = pltpu.PrefetchScalarGridSpec(
        num_scalar_prefetch=3, grid=(int(kt.shape[0]),), in_specs=in_specs, out_specs=[head_k, shared_k, head_k],
        scratch_shapes=[pltpu.VMEM((hb, t, QK_PAD), F32), pltpu.VMEM((hb, t, HEAD), F32)],
    )
    return pl.pallas_call(
        body, name=name, grid_spec=grid_spec,
        out_shape=[jax.ShapeDtypeStruct((s, n_heads * HEAD), BF16), jax.ShapeDtypeStruct((s, HEAD), F32),
                   jax.ShapeDtypeStruct((s, n_heads * HEAD), BF16)],
        compiler_params=_params(1),
    )(kt, gt, qt, *ins)


def rope_rows(name, z, tabs, *, tm=512):
    s = z.shape[0]
    tm = _tile(s, tm, 8)

    def body(z_ref, cs_ref, s1_ref, s2_ref, o_ref):
        o_ref[...] = _rope_tile(z_ref[...], cs_ref[...], s1_ref[...], s2_ref[...]).astype(BF16)

    row = pl.BlockSpec((tm, HEAD), lambda i: (i, 0))
    return pl.pallas_call(body, name=name, grid=(s // tm,), in_specs=[row] * 4, out_specs=row,
                          out_shape=jax.ShapeDtypeStruct((s, HEAD), BF16), compiler_params=_params(1))(z, *tabs)


def rope_tables(s):
    inv = 1.0 / (ROPE_THETA ** (jnp.arange(0, ROPE, 2, dtype=F32) / ROPE))
    ang = jnp.arange(s, dtype=F32)[:, None] * inv[None, :]
    cos, sin, zero = jnp.cos(ang), jnp.sin(ang), jnp.zeros((s, ROPE // 2), F32)
    pad = jnp.zeros((s, HEAD - ROPE), F32)
    cs = jnp.concatenate([cos, cos, pad], axis=1)
    lo = jnp.concatenate([sin, zero, pad], axis=1)
    hi = jnp.concatenate([zero, sin, pad], axis=1)
    return (cs, -lo, hi), (cs, lo, -hi)


def local_step(x, target, w, norms):
    s, d = x.shape
    depth = norms['ffn_norm1'].shape[0]
    n_a = w['a_wqkv'].shape[0]
    gw = {k: lax.empty(v.shape, BF16) for k, v in w.items()}
    n_heads = d // HEAD
    fwd_tabs, bwd_tabs = rope_tables(s)
    mla_scale = (HEAD + ROPE) ** -0.5
    q_scale = mla_scale * math.log2(math.e)
    gain = lambda name, i: norms[name][i:i + 1]
    saved = []
    kv = None

    def ffn_fwd(tag, x, g_norm, which, layer):
        wg, wu, wd = ((w[f'{which}_{nm}'], layer) for nm in ('wg', 'wu', 'wd'))
        h = norm_fwd(f'{tag}_norm', x, g_norm)
        g, u, a = ffn_up(f'{tag}_up', h, wg, wu)
        y = mm_nn(f'{tag}_down', a, wd, out_dtype=F32, res=x, alpha=0.5)
        return y, (x, h, g, u, a)

    def ffn_bwd(tag, dx, dxb, g_norm, which, layer, sv):
        wg, wu, wd = ((w[f'{which}_{nm}'], layer) for nm in ('wg', 'wu', 'wd'))
        x, h, g, u, a = sv
        dg, du = mm_nt(f'{tag}_bwd_da', [dxb], [wd], out_dtype=BF16, alpha=0.5, swiglu=(g, u))
        gw[f'{which}_wd'] = mm_tn(f'{tag}_bwd_dwd', a, dxb, (gw[f'{which}_wd'], layer), alpha=0.5)
        gw[f'{which}_wg'] = mm_tn(f'{tag}_bwd_dwg', h, dg, (gw[f'{which}_wg'], layer))
        gw[f'{which}_wu'] = mm_tn(f'{tag}_bwd_dwu', h, du, (gw[f'{which}_wu'], layer))
        dh = mm_nt(f'{tag}_bwd_dh', [dg, du], [wg, wu], out_dtype=F32)
        return norm_bwd(f'{tag}_bwd_norm', x, g_norm, dh, dx)

    for layer in range(depth):
        if layer == n_a:
            hk = norm_fwd('kv_norm', x, norms['kv_norm'])
            ckv_raw = mm_nn('kv_down', hk, (w['b_wdkv'], 0), out_dtype=F32)
            ckv = norm_fwd('kv_cnorm', ckv_raw, norms['b_ckv_norm'])
            k_nope = mm_nn('kv_uk', ckv, (w['b_wuk'], 0), out_dtype=BF16)
            v_shared = mm_nn('kv_uv', ckv, (w['b_wuv'], 0), out_dtype=BF16)
            k_rope = mm_nn('kv_kr', hk, (w['b_wkr'], 0), out_dtype=BF16, rope=('k',) + fwd_tabs, tn=HEAD)
            kv = (x, hk, ckv_raw, ckv, k_nope, k_rope, v_shared)
        x, sv1 = ffn_fwd(f'l{layer}_ffn1', x, gain('ffn_norm1', layer), 'ffn1', layer)
        h = norm_fwd(f'l{layer}_mix_norm', x, gain('mix_norm', layer))
        if layer < n_a:
            qkv = mm_nn(f'l{layer}_qkv', h, (w['a_wqkv'], layer), out_dtype=BF16)
            state = None
            for bi, (window, dilation) in enumerate(DILATED_BRANCHES):
                last = bi == len(DILATED_BRANCHES) - 1
                state = dilated_fwd(f'l{layer}_dil{dilation}', qkv, state, dilation, n_heads, last)
            o, lse = state
            x_new = mm_nn(f'l{layer}_wo', o, (w['a_wo'], layer), out_dtype=F32, res=x)
            svm = (x, h, qkv, o, lse)
        else:
            jb = layer - n_a
            cq_raw = mm_nn(f'l{layer}_dq', h, (w['b_wdq'], jb), out_dtype=F32)
            cq = norm_fwd(f'l{layer}_cq_norm', cq_raw, gain('b_cq_norm', jb))
            q = mm_nn(f'l{layer}_uq', cq, (w['b_wuq'], jb), out_dtype=BF16, alpha=q_scale, rope=('q',) + fwd_tabs)
            o, lse = mla_fwd(f'l{layer}_mla', q, kv[4], kv[5], kv[6], n_heads)
            x_new = mm_nn(f'l{layer}_wo', o, (w['b_wo'], jb), out_dtype=F32, res=x)
            svm = (x, h, cq_raw, cq, q, o, lse)
        x = x_new
        x, sv2 = ffn_fwd(f'l{layer}_ffn2', x, gain('ffn_norm2', layer), 'ffn2', layer)
        saved.append((sv1, svm, sv2))

    loss, dx, dxb, d_final = loss_head('loss_head', x, norms['final_norm'], target)

    gn = {k: [None] * v.shape[0] for k, v in norms.items()}
    gn['final_norm'] = [d_final]
    dkv = None
    for layer in reversed(range(depth)):
        sv1, svm, sv2 = saved[layer]
        dx, dxb, gn['ffn_norm2'][layer] = ffn_bwd(f'l{layer}_ffn2', dx, dxb, gain('ffn_norm2', layer), 'ffn2', layer, sv2)
        if layer < n_a:
            xm, h, qkv, o, lse = svm
            do = mm_nt(f'l{layer}_bwd_do', [dxb], [(w['a_wo'], layer)], out_dtype=BF16)
            gw['a_wo'] = mm_tn(f'l{layer}_bwd_dwo', o, dxb, (gw['a_wo'], layer))
            parts = [dilated_bwd(f'l{layer}_bwd_dil{dilation}', qkv, o, do, lse, dilation, n_heads)
                     for _, dilation in DILATED_BRANCHES]
            dqkv = sum_branches(f'l{layer}_bwd_sum', parts)
            gw['a_wqkv'] = mm_tn(f'l{layer}_bwd_dwqkv', h, dqkv, (gw['a_wqkv'], layer))
            dh = mm_nt(f'l{layer}_bwd_dh', [dqkv], [(w['a_wqkv'], layer)], out_dtype=F32)
        else:
            jb = layer - n_a
            xm, h, cq_raw, cq, q, o, lse = svm
            do = mm_nt(f'l{layer}_bwd_do', [dxb], [(w['b_wo'], jb)], out_dtype=BF16, alpha=mla_scale)
            gw['b_wo'] = mm_tn(f'l{layer}_bwd_dwo', o, dxb, (gw['b_wo'], jb))
            dq, delta = mla_bwd_dq(f'l{layer}_bwd_mla_dq', q, kv[4], kv[5], kv[6], o, do, lse, bwd_tabs, n_heads)
            dkv = mla_bwd_dkv(f'l{layer}_bwd_mla_dkv', q, kv[4], kv[5], kv[6], do, lse, delta, dkv, n_heads,
                              1.0 / q_scale, 1.0 / mla_scale)
            gw['b_wuq'] = mm_tn(f'l{layer}_bwd_dwuq', cq, dq, (gw['b_wuq'], jb))
            dcq = mm_nt(f'l{layer}_bwd_dcq', [dq], [(w['b_wuq'], jb)], out_dtype=F32)
            _, dcq_raw, gn['b_cq_norm'][jb] = norm_bwd(f'l{layer}_bwd_cq_norm', cq_raw, gain('b_cq_norm', jb), dcq)
            gw['b_wdq'] = mm_tn(f'l{layer}_bwd_dwdq', h, dcq_raw, (gw['b_wdq'], jb))
            dh = mm_nt(f'l{layer}_bwd_dh', [dcq_raw], [(w['b_wdq'], jb)], out_dtype=F32)
        dx, dxb, gn['mix_norm'][layer] = norm_bwd(f'l{layer}_bwd_mix_norm', xm, gain('mix_norm', layer), dh, dx)
        dx, dxb, gn['ffn_norm1'][layer] = ffn_bwd(f'l{layer}_ffn1', dx, dxb, gain('ffn_norm1', layer), 'ffn1', layer, sv1)
        if layer == n_a:
            xk, hk, ckv_raw, ckv, k_nope, k_rope, v_shared = kv
            dkn, dkr_rot, dv = dkv
            dkr = rope_rows('kv_bwd_rope', dkr_rot, bwd_tabs)
            gw['b_wuk'] = mm_tn('kv_bwd_dwuk', ckv, dkn, (gw['b_wuk'], 0))
            gw['b_wuv'] = mm_tn('kv_bwd_dwuv', ckv, dv, (gw['b_wuv'], 0))
            dckv = mm_nt('kv_bwd_dckv', [dkn, dv], [(w['b_wuk'], 0), (w['b_wuv'], 0)], out_dtype=F32)
            _, dckv_raw, g_ckv = norm_bwd('kv_bwd_cnorm', ckv_raw, norms['b_ckv_norm'], dckv)
            gw['b_wdkv'] = mm_tn('kv_bwd_dwdkv', hk, dckv_raw, (gw['b_wdkv'], 0))
            gw['b_wkr'] = mm_tn('kv_bwd_dwkr', hk, dkr, (gw['b_wkr'], 0))
            dhk = mm_nt('kv_bwd_dhk_c', [dckv_raw], [(w['b_wdkv'], 0)], out_dtype=F32)
            dhk = mm_nt('kv_bwd_dhk_r', [dkr], [(w['b_wkr'], 0)], out_dtype=F32, res=dhk)
            dx, dxb, g_kv = norm_bwd('kv_bwd_norm', xk, norms['kv_norm'], dhk, dx)
            gn['kv_norm'], gn['b_ckv_norm'] = [g_kv], [g_ckv]
    grads_n = {k: jnp.concatenate(v, axis=0) for k, v in gn.items()}
    return loss, dx, gw, grads_n


def _place():
    x, y, c = lax.axis_index("x"), lax.axis_index("y"), lax.axis_index("c")
    chips = [(1 - x, y), (x, 1 - y), (1 - x, 1 - y)]
    return x, y, c, 2 * x + y, chips, [2 * cx + cy for cx, cy in chips]


def _remote(src, dst, send_sem, recv_sem, to):
    return pltpu.make_async_remote_copy(src_ref=src, dst_ref=dst, send_sem=send_sem, recv_sem=recv_sem,
                                        device_id=to, device_id_type=MESH)


def cast_to_slab(name, shard, place, *, tr=256):
    l, r, c = shard.shape
    tr = _tile(r, tr, 16)

    def body(place_ref, x_ref, o_ref):
        o_ref[...] = x_ref[...].astype(BF16)

    grid_spec = pltpu.PrefetchScalarGridSpec(
        num_scalar_prefetch=1, grid=(l, r // tr),
        in_specs=[pl.BlockSpec((None, tr, c), lambda i, j, p: (i, j, 0))],
        out_specs=pl.BlockSpec((None, None, tr, c), lambda i, j, p: (i, p[0], j, 0)),
    )
    return pl.pallas_call(body, name=name, grid_spec=grid_spec,
                          out_shape=jax.ShapeDtypeStruct((l, N_CHIPS, r, c), BF16), compiler_params=_params(2))(place, shard)


def all_gather_weights(name, slabs):
    n = len(slabs)

    def body(*refs):
        outs = refs[n:2 * n]
        send_sems, recv_sems = refs[2 * n:]
        x, y, c, me, chips, chip_ids = _place()
        sibling = (x, y, 1 - c)

        def copy(t, k, chip, hc, to):
            rh = outs[t].shape[2] // 2
            blk = outs[t].at[:, chip, pl.ds(hc * rh, rh), :]
            return _remote(blk, blk, send_sems.at[6 * t + k], recv_sems.at[6 * t + k], to)

        first = [copy(t, k, me, c, (*chips[k], c)) for t in range(n) for k in range(3)]
        for cp in first:
            cp.start()
        passed = []
        for t in range(n):
            for k in range(3):
                copy(t, k, chip_ids[k], c, sibling).wait_recv()
                cp = copy(t, 3 + k, chip_ids[k], c, sibling)
                cp.start()
                passed.append(cp)
        for t in range(n):
            for k in range(3):
                copy(t, 3 + k, chip_ids[k], 1 - c, sibling).wait_recv()
        for cp in first + passed:
            cp.wait_send()

    any_spec = pl.BlockSpec(memory_space=pl.ANY)
    return pl.pallas_call(
        body, name=name, in_specs=[any_spec] * n, out_specs=[any_spec] * n,
        out_shape=[jax.ShapeDtypeStruct(a.shape, a.dtype) for a in slabs],
        scratch_shapes=[pltpu.SemaphoreType.DMA((6 * n,)), pltpu.SemaphoreType.DMA((6 * n,))],
        input_output_aliases={t: t for t in range(n)},
        compiler_params=pltpu.CompilerParams(has_side_effects=True),
    )(*slabs)


def exchange_halves(name, parts):
    n = len(parts)

    def body(*refs):
        ins, outs = refs[:n], refs[n:2 * n]
        send_sems, recv_sems = refs[2 * n:]
        x, y, c, me, chips, chip_ids = _place()
        copies = []
        for t in range(n):
            rh = ins[t].shape[2] // 2
            cp = _remote(ins[t].at[:, :, pl.ds((1 - c) * rh, rh), :], outs[t], send_sems.at[t], recv_sems.at[t],
                         (x, y, 1 - c))
            cp.start()
            copies.append(cp)
        for cp in copies:
            cp.wait()

    any_spec = pl.BlockSpec(memory_space=pl.ANY)
    return pl.pallas_call(
        body, name=name, in_specs=[any_spec] * n, out_specs=[any_spec] * n,
        out_shape=[jax.ShapeDtypeStruct((a.shape[0], a.shape[1], a.shape[2] // 2, a.shape[3]), a.dtype) for a in parts],
        scratch_shapes=[pltpu.SemaphoreType.DMA((n,)), pltpu.SemaphoreType.DMA((n,))],
        compiler_params=pltpu.CompilerParams(has_side_effects=True),
    )(*parts)


def add_halves(name, part, recv, place, *, tr=256):
    l, j, rh, c = recv.shape
    tr = _tile(rh, tr, 16)
    nr = rh // tr

    def body(place_ref, p_ref, r_ref, o_ref):
        o_ref[...] = (p_ref[...].astype(F32) + r_ref[...].astype(F32)).astype(BF16)

    blk = (None, None, tr, c)
    grid_spec = pltpu.PrefetchScalarGridSpec(
        num_scalar_prefetch=1, grid=(l, j, nr),
        in_specs=[pl.BlockSpec(blk, lambda a, b, i, p: (a, b, p[1] * nr + i, 0)),
                  pl.BlockSpec(blk, lambda a, b, i, p: (a, b, i, 0))],
        out_specs=pl.BlockSpec(blk, lambda a, b, i, p: (a, b, i, 0)),
    )
    return pl.pallas_call(body, name=name, grid_spec=grid_spec, out_shape=jax.ShapeDtypeStruct(recv.shape, BF16),
                          compiler_params=_params(3))(place, part, recv)


def scatter_to_chips(name, sums):
    n = len(sums)

    def body(*refs):
        ins, outs = refs[:n], refs[n:2 * n]
        send_sems, recv_sems = refs[2 * n:]
        x, y, c, me, chips, chip_ids = _place()
        copies = []
        for t in range(n):
            for k in range(3):
                cp = _remote(ins[t].at[:, chip_ids[k]], outs[t].at[k], send_sems.at[3 * t + k], recv_sems.at[3 * t + k],
                             (*chips[k], c))
                cp.start()
                copies.append(cp)
        for cp in copies:
            cp.wait()

    any_spec = pl.BlockSpec(memory_space=pl.ANY)
    return pl.pallas_call(
        body, name=name, in_specs=[any_spec] * n, out_specs=[any_spec] * n,
        out_shape=[jax.ShapeDtypeStruct((3, a.shape[0], a.shape[2], a.shape[3]), a.dtype) for a in sums],
        scratch_shapes=[pltpu.SemaphoreType.DMA((3 * n,)), pltpu.SemaphoreType.DMA((3 * n,))],
        compiler_params=pltpu.CompilerParams(has_side_effects=True),
    )(*sums)


def reduce_own(name, part, recv1, recv2, place, *, tr=256):
    l, j, rh, c = recv1.shape
    tr = _tile(rh, tr, 16)
    nr = rh // tr

    def body(place_ref, p_ref, r1_ref, a_ref, b_ref, c_ref, o_ref):
        acc = p_ref[...].astype(F32) + r1_ref[...].astype(F32)
        acc = acc + a_ref[...].astype(F32)
        acc = acc + b_ref[...].astype(F32)
        o_ref[...] = acc + c_ref[...].astype(F32)

    blk = (None, None, tr, c)
    grid_spec = pltpu.PrefetchScalarGridSpec(
        num_scalar_prefetch=1, grid=(l, nr),
        in_specs=[pl.BlockSpec(blk, lambda a, i, p: (a, p[0], p[1] * nr + i, 0)),
                  pl.BlockSpec(blk, lambda a, i, p: (a, p[0], i, 0)),
                  pl.BlockSpec(blk, lambda a, i, p: (0, a, i, 0)),
                  pl.BlockSpec(blk, lambda a, i, p: (1, a, i, 0)),
                  pl.BlockSpec(blk, lambda a, i, p: (2, a, i, 0))],
        out_specs=pl.BlockSpec((None, tr, c), lambda a, i, p: (a, p[1] * nr + i, 0)),
    )
    return pl.pallas_call(body, name=name, grid_spec=grid_spec, out_shape=jax.ShapeDtypeStruct((l, 2 * rh, c), F32),
                          compiler_params=_params(2))(place, part, recv1, recv2, recv2, recv2)


def share_halves(name, grads):
    n = len(grads)

    def body(*refs):
        outs = refs[n:2 * n]
        send_sems, recv_sems = refs[2 * n:]
        x, y, c, me, chips, chip_ids = _place()
        copies = []
        for t in range(n):
            rh = outs[t].shape[1] // 2
            mine = outs[t].at[:, pl.ds(c * rh, rh), :]
            cp = _remote(mine, mine, send_sems.at[t], recv_sems.at[t], (x, y, 1 - c))
            cp.start()
            copies.append(cp)
        for t, cp in enumerate(copies):
            rh = outs[t].shape[1] // 2
            theirs = outs[t].at[:, pl.ds((1 - c) * rh, rh), :]
            cp.wait_send()
            _remote(theirs, theirs, send_sems.at[t], recv_sems.at[t], (x, y, 1 - c)).wait_recv()

    any_spec = pl.BlockSpec(memory_space=pl.ANY)
    return pl.pallas_call(
        body, name=name, in_specs=[any_spec] * n, out_specs=[any_spec] * n,
        out_shape=[jax.ShapeDtypeStruct(a.shape, a.dtype) for a in grads],
        scratch_shapes=[pltpu.SemaphoreType.DMA((n,)), pltpu.SemaphoreType.DMA((n,))],
        input_output_aliases={t: t for t in range(n)},
        compiler_params=pltpu.CompilerParams(has_side_effects=True),
    )(*grads)


def all_reduce_small(name, packed):
    r = packed.shape[0]

    def body(x_ref, o_ref, buf, send_sems, recv_sems):
        x, y, c = lax.axis_index("x"), lax.axis_index("y"), lax.axis_index("c")
        me = 4 * x + 2 * y + c
        buf[me] = x_ref[...]
        copies = []
        for k in range(1, 8):
            to = (x ^ (k >> 2), y ^ ((k >> 1) & 1), c ^ (k & 1))
            cp = _remote(x_ref, buf.at[me], send_sems.at[k - 1], recv_sems.at[k - 1], to)
            cp.start()
            copies.append(cp)
        for k in range(1, 8):
            peer = me ^ k
            _remote(x_ref, buf.at[peer], send_sems.at[k - 1], recv_sems.at[k - 1], (x, y, c)).wait_recv()
        for cp in copies:
            cp.wait_send()
        acc = buf[0]
        for dev in range(1, 8):
            acc = acc + buf[dev]
        o_ref[...] = acc

    vmem = pl.BlockSpec(memory_space=pltpu.VMEM)
    return pl.pallas_call(
        body, name=name, in_specs=[vmem], out_specs=vmem, out_shape=jax.ShapeDtypeStruct(packed.shape, F32),
        scratch_shapes=[pltpu.VMEM((8, r, HEAD), F32), pltpu.SemaphoreType.DMA((7,)), pltpu.SemaphoreType.DMA((7,))],
    )(packed)


def adamw(name, w, g, m, v, *, tr=256):
    l, r, c = w.shape
    tr = _tile(r, tr, 8)

    def body(w_ref, g_ref, m_ref, v_ref, d_ref, nm_ref, nv_ref):
        gv = g_ref[...]
        nm = ADAM_B1 * m_ref[...] + (1.0 - ADAM_B1) * gv
        nv = ADAM_B2 * v_ref[...] + (1.0 - ADAM_B2) * (gv * gv)
        m_hat = nm / (1.0 - ADAM_B1 ** ADAM_STEP)
        v_hat = nv / (1.0 - ADAM_B2 ** ADAM_STEP)
        d_ref[...] = -ADAM_LR * (m_hat / (jnp.sqrt(v_hat) + ADAM_EPS) + ADAM_WD * w_ref[...])
        nm_ref[...] = nm
        nv_ref[...] = nv

    blk = pl.BlockSpec((None, tr, c), lambda i, j: (i, j, 0))
    shape = jax.ShapeDtypeStruct(w.shape, F32)
    return pl.pallas_call(body, name=name, grid=(l, r // tr), in_specs=[blk] * 4, out_specs=[blk] * 3,
                          out_shape=[shape] * 3, compiler_params=_params(2))(w, g, m, v)


SHARDED = ('ffn1_wg', 'ffn1_wu', 'ffn1_wd', 'ffn2_wg', 'ffn2_wu', 'ffn2_wd', 'a_wqkv', 'a_wo', 'b_wdkv', 'b_wkr', 'b_wuk',
           'b_wuv', 'b_wdq', 'b_wuq', 'b_wo')
COLUMN_SHARDED = ('ffn1_wg', 'ffn1_wu', 'ffn2_wg', 'ffn2_wu', 'a_wqkv')
GAINS = ('ffn_norm1', 'mix_norm', 'ffn_norm2', 'kv_norm', 'b_ckv_norm', 'b_cq_norm', 'final_norm')
WEIGHTS = ('ffn_norm1', 'ffn1_wg', 'ffn1_wu', 'ffn1_wd', 'mix_norm', 'ffn_norm2', 'ffn2_wg', 'ffn2_wu', 'ffn2_wd', 'a_wqkv',
           'a_wo', 'kv_norm', 'b_wdkv', 'b_ckv_norm', 'b_wkr', 'b_wuk', 'b_wuv', 'b_wdq', 'b_cq_norm', 'b_wuq', 'b_wo',
           'final_norm')


def _to_comm_shape(name, a):
    if name == 'b_wkr':
        return jnp.pad(a, ((0, 0), (0, HEAD - ROPE)))[None]
    if name == 'b_wuq':
        l, r, h, _ = a.shape
        return jnp.pad(a, ((0, 0), (0, 0), (0, 0), (0, QK_PAD - HEAD - ROPE))).reshape(l, r, h * QK_PAD)
    if name in ('b_wuk', 'b_wuv'):
        return a.reshape(1, a.shape[0], -1)
    return a[None] if a.ndim == 2 else a


def _from_comm_shape(name, g, like):
    if name == 'b_wkr':
        return g[0, :, :ROPE]
    if name == 'b_wuq':
        l, r, h, e = like.shape
        return g.reshape(l, r, h, QK_PAD)[..., :e]
    return g.reshape(like.shape)


def _as3d(a):
    if a.ndim == 1:
        return a.reshape(1, 1, -1)
    if a.ndim == 2:
        return a[None]
    return a.reshape(a.shape[0], a.shape[1], -1)


def kernel(x, ffn_norm1, ffn1_wg, ffn1_wu, ffn1_wd, mix_norm, ffn_norm2, ffn2_wg, ffn2_wu, ffn2_wd, a_wqkv, a_wo, kv_norm, b_wdkv, b_ckv_norm, b_wkr, b_wuk, b_wuv, b_wdq, b_cq_norm, b_wuq, b_wo, final_norm, loss_target, m_ffn_norm1, m_ffn1_wg, m_ffn1_wu, m_ffn1_wd, m_mix_norm, m_ffn_norm2, m_ffn2_wg, m_ffn2_wu, m_ffn2_wd, m_a_wqkv, m_a_wo, m_kv_norm, m_b_wdkv, m_b_ckv_norm, m_b_wkr, m_b_wuk, m_b_wuv, m_b_wdq, m_b_cq_norm, m_b_wuq, m_b_wo, m_final_norm, v_ffn_norm1, v_ffn1_wg, v_ffn1_wu, v_ffn1_wd, v_mix_norm, v_ffn_norm2, v_ffn2_wg, v_ffn2_wu, v_ffn2_wd, v_a_wqkv, v_a_wo, v_kv_norm, v_b_wdkv, v_b_ckv_norm, v_b_wkr, v_b_wuk, v_b_wuv, v_b_wdq, v_b_cq_norm, v_b_wuq, v_b_wo, v_final_norm):
    args = dict(locals())
    weights = {k: args[k] for k in WEIGHTS}
    place = jnp.stack([2 * lax.axis_index("x") + lax.axis_index("y"), lax.axis_index("c")]).astype(jnp.int32)

    slabs = [cast_to_slab(f'cast_{k}', _to_comm_shape(k, weights[k]), place) for k in SHARDED]
    gathered = all_gather_weights('gather_weights', slabs)
    w = {}
    for k, g in zip(SHARDED, gathered):
        l, j, r, c = g.shape
        w[k] = g if k in COLUMN_SHARDED else g.reshape(l, j * r, c)
    norms = {k: weights[k] if weights[k].ndim == 2 else weights[k][None] for k in GAINS}

    loss, grad_x, gw, gn = local_step(x[0], loss_target[0], w, norms)
    loss = lax.psum(loss[0, 0], ("x", "y", "c"))

    parts = [gw[k].reshape(g.shape) for k, g in zip(SHARDED, gathered)]
    recv1 = exchange_halves('grads_to_sibling', parts)
    sums = [add_halves(f'add_{k}', p, r1, place) for k, p, r1 in zip(SHARDED, parts, recv1)]
    recv2 = scatter_to_chips('grads_to_chips', sums)
    halves = [reduce_own(f'reduce_{k}', p, r1, r2, place) for k, p, r1, r2 in zip(SHARDED, parts, recv1, recv2)]
    reduced = share_halves('grads_share', halves)
    grads = {k: _from_comm_shape(k, g, weights[k]) for k, g in zip(SHARDED, reduced)}

    flat = jnp.concatenate([gn[k].reshape(-1) for k in GAINS])
    rows = -(-flat.shape[0] // (8 * HEAD)) * 8
    packed = jnp.pad(flat, (0, rows * HEAD - flat.shape[0])).reshape(rows, HEAD)
    total = all_reduce_small('gain_grads', packed).reshape(-1)
    off = 0
    for k in GAINS:
        grads[k] = total[off:off + weights[k].size].reshape(weights[k].shape)
        off += weights[k].size

    deltas, new_m, new_v = {}, {}, {}
    for k in WEIGHTS:
        shape = weights[k].shape
        d, nm, nv = adamw(f'adamw_{k}', _as3d(weights[k]), _as3d(grads[k]), _as3d(args['m_' + k]), _as3d(args['v_' + k]))
        deltas[k], new_m[k], new_v[k] = d.reshape(shape), nm.reshape(shape), nv.reshape(shape)
    return (loss, grad_x[None], *[grads[k] for k in WEIGHTS], *[deltas[k] for k in WEIGHTS],
            *[new_m[k] for k in WEIGHTS], *[new_v[k] for k in WEIGHTS])
```

```python
import math

import numpy as np
import jax
import jax.numpy as jnp
from jax import lax
from jax.experimental import pallas as pl
from jax.experimental.pallas import tpu as pltpu

F32 = jnp.float32
BF16 = jnp.bfloat16
NORM_EPS = 1e-6
MASK_VALUE = -1e30
HEAD = 128
ROPE = 64
QK_PAD = 256
DILATED_BRANCHES = ((128, 1), (512, 4), (2048, 16))
ROPE_THETA = 10000.0
ADAM_LR, ADAM_B1, ADAM_B2, ADAM_EPS, ADAM_WD, ADAM_STEP = 0.001, 0.9, 0.999, 1e-08, 0.01, 10
VMEM_LIMIT_BYTES = 56 * 1024 * 1024
MM_ROWS, MM_COLS, MM_WIDE, MM_SHALLOW, MM_DEPTH = 512, 512, 1024, 2048, 8192
MM_SUB = 256
WGRAD_DEPTH = 4096
DILATED_HEADS_PER_STEP, MLA_FWD_HEADS_PER_STEP, MLA_BWD_HEADS_PER_STEP = 16, 4, 4
N_CHIPS = 4
MESH = pl.DeviceIdType.MESH


def _params(n_axes, vmem=VMEM_LIMIT_BYTES):
    return pltpu.CompilerParams(dimension_semantics=("arbitrary",) * n_axes, vmem_limit_bytes=vmem)


def _tile(n, pref, mult=128):
    if n <= pref:
        return n
    t = (pref // mult) * mult
    while t >= mult:
        if n % t == 0:
            return t
        t -= mult
    return n


def _mm_call(name, grid, ins, in_specs, out_shape, out_specs, n_a, n_b, terms, dims, acc_shapes, epilogue, into=None,
             nt_sub=None):
    n_in, n_out, nk = len(ins), len(out_shape), grid[2]
    aliases = {}
    if into is not None:
        ins = list(ins) + [into]
        in_specs = list(in_specs) + [pl.BlockSpec(memory_space=pl.ANY)]
        aliases = {n_in: 0}

    def body(*refs):
        in_refs, refs = refs[:n_in], refs[n_in + len(aliases):]
        out_refs, accs = refs[:n_out], refs[n_out:]
        a_refs, b_refs, extra = in_refs[:n_a], in_refs[n_a:n_a + n_b], in_refs[n_a + n_b:]
        k = pl.program_id(2)
        if nt_sub is not None:
            for c0 in range(0, acc_shapes[0][1], nt_sub):
                cols = slice(c0, c0 + nt_sub)
                part = None
                for t, ia, ib in terms:
                    one = lax.dot_general(a_refs[ia][...], b_refs[ib][cols, :], (dims, ((), ())), preferred_element_type=F32)
                    part = one if part is None else part + one
                epilogue([part], extra, out_refs, cols)
            return
        sums = [None] * len(acc_shapes)
        for t, ia, ib in terms:
            a = a_refs[ia][...].astype(BF16)
            b = b_refs[ib][...].astype(BF16)
            part = lax.dot_general(a, b, (dims, ((), ())), preferred_element_type=F32)
            sums[t] = part if sums[t] is None else sums[t] + part
        if nk == 1:
            epilogue(sums, extra, out_refs)
            return

        @pl.when(k == 0)
        def _():
            for acc, part in zip(accs, sums):
                acc[...] = part

        @pl.when(k > 0)
        def _():
            for acc, part in zip(accs, sums):
                acc[...] += part

        @pl.when(k == nk - 1)
        def _():
            epilogue([acc[...] for acc in accs], extra, out_refs)

    return pl.pallas_call(
        body, name=name, grid=grid, in_specs=in_specs, out_specs=out_specs, out_shape=out_shape,
        scratch_shapes=[pltpu.VMEM(s, F32) for s in acc_shapes] if nk > 1 else [], input_output_aliases=aliases,
        compiler_params=_params(3),
    )(*ins)


def _b_spec_nn(w, tk, tn):
    b, layer = w
    if b.ndim == 3:
        return pl.BlockSpec((None, tk, tn), lambda i, j, k: (layer, k, j))
    per = b.shape[3] // tn
    return pl.BlockSpec((None, None, tk, tn), lambda i, j, k: (layer, j // per, k, j % per))


def _rope_tile(z, cs, s1, s2):
    return z * cs + pltpu.roll(z, 96, 1) * s1 + pltpu.roll(z, 32, 1) * s2


def mm_nn(name, a, w, *, out_dtype, res=None, alpha=1.0, rope=None, tm=MM_ROWS, tn=None, tk=MM_DEPTH):
    m, kk = a.shape
    b = w[0]
    ns = b.shape[-1]
    n = b.shape[1] * ns if b.ndim == 4 else ns
    tm, tk = _tile(m, tm, 8), _tile(kk, tk)
    tn = _tile(ns, tn or (MM_WIDE if kk <= MM_SHALLOW else MM_COLS))
    if rope is not None and rope[0] == 'q':
        assert tn % QK_PAD == 0
    grid = (m // tm, n // tn, kk // tk)
    ins = [a, b]
    in_specs = [pl.BlockSpec((tm, tk), lambda i, j, k: (i, k)), _b_spec_nn(w, tk, tn)]
    if res is not None:
        ins.append(res)
        in_specs.append(pl.BlockSpec((tm, tn), lambda i, j, k: (i, j)))
    if rope is not None:
        ins += list(rope[1:])
        in_specs += [pl.BlockSpec((tm, HEAD), lambda i, j, k: (i, 0))] * 3

    def epilogue(accs, extra, outs):
        y = accs[0]
        extra = list(extra)
        if alpha != 1.0:
            y = alpha * y
        if res is not None:
            y = extra.pop(0)[...] + y
        if rope is not None:
            cs, s1, s2 = (e[...] for e in extra)
            step = QK_PAD if rope[0] == 'q' else HEAD
            for c0 in range(0, tn, step):
                if rope[0] == 'q':
                    outs[0][:, c0:c0 + HEAD] = y[:, c0:c0 + HEAD].astype(out_dtype)
                    c0 += HEAD
                outs[0][:, c0:c0 + HEAD] = _rope_tile(y[:, c0:c0 + HEAD], cs, s1, s2).astype(out_dtype)
        else:
            outs[0][...] = y.astype(out_dtype)

    return _mm_call(name, grid, ins, in_specs, [jax.ShapeDtypeStruct((m, n), out_dtype)],
                    [pl.BlockSpec((tm, tn), lambda i, j, k: (i, j))], 1, 1, [(0, 0, 0)], ((1,), (0,)),
                    [(tm, tn)], epilogue)[0]


def ffn_up(name, h, wg, wu, *, tm=MM_ROWS, tk=MM_DEPTH):
    m, kk = h.shape
    fs = wg[0].shape[3]
    n = wg[0].shape[1] * fs
    tm, tk, tn = _tile(m, tm, 8), _tile(kk, tk), fs
    grid = (m // tm, n // tn, kk // tk)

    def epilogue(accs, extra, outs):
        g, u = accs
        outs[0][...] = g.astype(BF16)
        outs[1][...] = u.astype(BF16)
        outs[2][...] = (g * jax.nn.sigmoid(g) * u).astype(BF16)

    o_spec = pl.BlockSpec((tm, tn), lambda i, j, k: (i, j))
    return _mm_call(name, grid, [h, wg[0], wu[0]],
                    [pl.BlockSpec((tm, tk), lambda i, j, k: (i, k)), _b_spec_nn(wg, tk, tn), _b_spec_nn(wu, tk, tn)],
                    [jax.ShapeDtypeStruct((m, n), BF16)] * 3, [o_spec] * 3, 1, 2, [(0, 0, 0), (1, 0, 1)],
                    ((1,), (0,)), [(tm, tn)] * 2, epilogue)


def mm_nt(name, a_list, w_list, *, out_dtype, alpha=1.0, swiglu=None, res=None, tm=MM_ROWS, tn=None, tk=MM_DEPTH):
    m, n = a_list[0].shape
    b0 = w_list[0][0]
    slab = b0.ndim == 4
    kk = b0.shape[-2]
    ns = b0.shape[-1]
    tk = ns if slab else _tile(ns, tk)
    wide = slab or tk <= MM_SHALLOW
    tm, tn = _tile(m, tm, 8), _tile(kk, tn or (MM_WIDE if wide else MM_COLS))
    grid = (m // tm, kk // tn, n // tk)

    def b_spec(w):
        layer = w[1]
        if slab:
            per = ns // tk
            return pl.BlockSpec((None, None, tn, tk), lambda i, j, k: (layer, k // per, j, k % per))
        return pl.BlockSpec((None, tn, tk), lambda i, j, k: (layer, j, k))

    p = len(a_list)
    ins = list(a_list) + [w[0] for w in w_list]
    in_specs = [pl.BlockSpec((tm, tk), lambda i, j, k: (i, k))] * p + [b_spec(w) for w in w_list]
    o_spec = pl.BlockSpec((tm, tn), lambda i, j, k: (i, j))
    if swiglu is not None:
        ins += list(swiglu)
        in_specs += [o_spec, o_spec]
        out_shape = [jax.ShapeDtypeStruct((m, kk), BF16)] * 2

        def epilogue(accs, extra, outs, cols=slice(None)):
            da = alpha * accs[0]
            g = extra[0][:, cols].astype(F32)
            u = extra[1][:, cols].astype(F32)
            sg = jax.nn.sigmoid(g)
            silu = g * sg
            outs[0][:, cols] = (da * u * (sg + silu * (1.0 - sg))).astype(BF16)
            outs[1][:, cols] = (da * silu).astype(BF16)
    else:
        out_shape = [jax.ShapeDtypeStruct((m, kk), out_dtype)]
        if res is not None:
            ins.append(res)
            in_specs.append(o_spec)

        def epilogue(accs, extra, outs):
            y = alpha * accs[0]
            if res is not None:
                y = y + extra[0][...]
            outs[0][...] = y.astype(out_dtype)

    nt_sub = MM_SUB if swiglu is not None and grid[2] == 1 and tn % MM_SUB == 0 and not slab else None
    outs = _mm_call(name, grid, ins, in_specs, out_shape, [o_spec] * len(out_shape), p, p,
                    [(0, q, q) for q in range(p)], ((1,), (1,)), [(tm, tn)], epilogue, nt_sub=nt_sub)
    return outs if swiglu is not None else outs[0]


def mm_tn(name, a, b, into, *, alpha=1.0, tm=MM_ROWS, tn=MM_WIDE, tk=WGRAD_DEPTH):
    buf, layer = into
    m, kk = a.shape
    n = b.shape[1]
    ns = buf.shape[-1]
    tm, tk = _tile(kk, tm), _tile(m, tk, 8)
    tn = ns if ns % 512 else _tile(ns, tn)
    grid = (kk // tm, n // tn, m // tk)
    if buf.ndim == 4:
        per = ns // tn
        o_spec = pl.BlockSpec((None, None, tm, tn), lambda i, j, k: (layer, j // per, i, j % per))
    else:
        o_spec = pl.BlockSpec((None, tm, tn), lambda i, j, k: (layer, i, j))

    def epilogue(accs, extra, outs):
        outs[0][...] = (alpha * accs[0]).astype(buf.dtype)

    return _mm_call(name, grid, [a, b],
                    [pl.BlockSpec((tk, tm), lambda i, j, k: (k, i)), pl.BlockSpec((tk, tn), lambda i, j, k: (k, j))],
                    [jax.ShapeDtypeStruct(buf.shape, buf.dtype)], [o_spec], 1, 1, [(0, 0, 0)], ((0,), (0,)),
                    [(tm, tn)], epilogue, into=buf)[0]


def norm_fwd(name, x, gain, *, tm=256):
    m, d = x.shape
    tm = _tile(m, tm, 8)

    def body(x_ref, g_ref, o_ref):
        xv = x_ref[...]
        r = lax.rsqrt(jnp.mean(xv * xv, axis=-1, keepdims=True) + NORM_EPS)
        o_ref[...] = (xv * r * g_ref[...]).astype(BF16)

    return pl.pallas_call(
        body, name=name, grid=(m // tm,),
        in_specs=[pl.BlockSpec((tm, d), lambda i: (i, 0)), pl.BlockSpec((1, d), lambda i: (0, 0))],
        out_specs=pl.BlockSpec((tm, d), lambda i: (i, 0)), out_shape=jax.ShapeDtypeStruct((m, d), BF16),
        compiler_params=_params(1),
    )(x, gain)


def norm_bwd(name, x, gain, dh, dres=None, *, tm=256):
    m, d = x.shape
    tm = _tile(m, tm, 8)
    n_steps = m // tm
    has_res = dres is not None

    def body(*refs):
        if has_res:
            x_ref, g_ref, dh_ref, dres_ref, dx_ref, dxb_ref, dg_ref, acc = refs
        else:
            x_ref, g_ref, dh_ref, dx_ref, dxb_ref, dg_ref, acc = refs
        i = pl.program_id(0)
        xv = x_ref[...]
        dy = dh_ref[...].astype(F32)
        r = lax.rsqrt(jnp.mean(xv * xv, axis=-1, keepdims=True) + NORM_EPS)
        xhat = xv * r
        dxhat = dy * g_ref[...]
        dx = r * (dxhat - xhat * jnp.mean(dxhat * xhat, axis=-1, keepdims=True))
        if has_res:
            dx = dx + dres_ref[...]
        dx_ref[...] = dx
        dxb_ref[...] = dx.astype(BF16)

        @pl.when(i == 0)
        def _():
            acc[...] = jnp.zeros_like(acc)

        acc[...] += jnp.sum((dy * xhat).reshape(tm // 8, 8, d), axis=0)

        @pl.when(i == n_steps - 1)
        def _():
            dg_ref[...] = jnp.sum(acc[...], axis=0, keepdims=True)

    row = pl.BlockSpec((tm, d), lambda i: (i, 0))
    vec = pl.BlockSpec((1, d), lambda i: (0, 0))
    ins = [x, gain, dh] + ([dres] if has_res else [])
    return pl.pallas_call(
        body, name=name, grid=(n_steps,), in_specs=[row, vec, row] + ([row] if has_res else []),
        out_specs=[row, row, vec],
        out_shape=[jax.ShapeDtypeStruct((m, d), F32), jax.ShapeDtypeStruct((m, d), BF16), jax.ShapeDtypeStruct((1, d), F32)],
        scratch_shapes=[pltpu.VMEM((8, d), F32)], compiler_params=_params(1),
    )(*ins)


def loss_head(name, x, gain, target, *, tm=256):
    m, d = x.shape
    tm = _tile(m, tm, 8)
    n_steps = m // tm

    def body(x_ref, g_ref, t_ref, loss_ref, dx_ref, dxb_ref, dg_ref, acc, lacc):
        i = pl.program_id(0)
        xv = x_ref[...]
        g = g_ref[...]
        r = lax.rsqrt(jnp.mean(xv * xv, axis=-1, keepdims=True) + NORM_EPS)
        xhat = xv * r
        err = xhat * g - t_ref[...]
        dy = err * (1.0 / d)
        dxhat = dy * g
        dx = r * (dxhat - xhat * jnp.mean(dxhat * xhat, axis=-1, keepdims=True))
        dx_ref[...] = dx
        dxb_ref[...] = dx.astype(BF16)

        @pl.when(i == 0)
        def _():
            acc[...] = jnp.zeros_like(acc)
            lacc[...] = jnp.zeros_like(lacc)

        acc[...] += jnp.sum((dy * xhat).reshape(tm // 8, 8, d), axis=0)
        lacc[...] += jnp.sum((err * err).reshape(tm // 8, 8, d), axis=0)

        @pl.when(i == n_steps - 1)
        def _():
            dg_ref[...] = jnp.sum(acc[...], axis=0, keepdims=True)
            loss_ref[...] = (0.5 / d) * jnp.sum(jnp.sum(lacc[...], axis=0, keepdims=True), axis=1, keepdims=True)

    row = pl.BlockSpec((tm, d), lambda i: (i, 0))
    vec = pl.BlockSpec((1, d), lambda i: (0, 0))
    return pl.pallas_call(
        body, name=name, grid=(n_steps,), in_specs=[row, vec, row],
        out_specs=[pl.BlockSpec((1, 1), lambda i: (0, 0)), row, row, vec],
        out_shape=[jax.ShapeDtypeStruct((1, 1), F32), jax.ShapeDtypeStruct((m, d), F32),
                   jax.ShapeDtypeStruct((m, d), BF16), jax.ShapeDtypeStruct((1, d), F32)],
        scratch_shapes=[pltpu.VMEM((8, d), F32), pltpu.VMEM((8, d), F32)], compiler_params=_params(1),
    )(x, gain, target)


def _lane_is(h, shape):
    return lax.broadcasted_iota(jnp.int32, shape, 1) == h


def _take_lane(tile, h):
    return jnp.sum(jnp.where(_lane_is(h, tile.shape), tile, 0.0), axis=1, keepdims=True)


def _merge_lanes(dst, src, start, count):
    lane = lax.broadcasted_iota(jnp.int32, dst.shape, 1)
    return jnp.where((lane >= start) & (lane < start + count), src, dst)


def _band_scores(q, kp, kc, h, c, dilation, n_heads):
    nt = (((1,), (1,)), ((), ()))
    scale = HEAD ** -0.5
    i = lax.broadcasted_iota(jnp.int32, (HEAD, HEAD), 0)
    j = lax.broadcasted_iota(jnp.int32, (HEAD, HEAD), 1)
    slope = jnp.exp(jnp.full((HEAD, HEAD), -8.0 * math.log(2.0) / n_heads, F32) * (h + 1).astype(F32))
    dist = (i - j).astype(F32) * float(dilation)
    s_c = lax.dot_general(q, kc, nt, preferred_element_type=F32) * scale - slope * dist
    s_p = lax.dot_general(q, kp, nt, preferred_element_type=F32) * scale - slope * (dist + float(dilation * HEAD))
    s_c = jnp.where(j <= i, s_c, MASK_VALUE)
    s_p = jnp.where((j >= i) & (c > 0), s_p, MASK_VALUE)
    return s_p, s_c


def dilated_fwd(name, qkv, state, dilation, n_heads, last):
    s, w3 = qkv.shape
    w = w3 // 3
    d = dilation
    rows, nb = s // d, s // (d * HEAD)
    first = state is None
    qkv_v = qkv.reshape(rows, d * w3)
    hb = min(DILATED_HEADS_PER_STEP, n_heads)
    groups = n_heads // hb

    def body(*refs):
        q_ref, kp_ref, kc_ref, vp_ref, vc_ref = refs[:5]
        refs = refs[5:]
        if not first:
            acc_in, m_in, l_in = refs[:3]
            refs = refs[3:]
        if last:
            o_ref, lse_ref = refs
        else:
            acc_out, m_out, l_out = refs
        c, g = pl.program_id(1), pl.program_id(2)
        if first:
            m_tile = l_tile = jnp.zeros((HEAD, HEAD), F32)
        else:
            m_tile, l_tile = m_in[...], l_in[...]
        stat_a, stat_b = m_tile, l_tile
        for hh in range(hb):
            h = g * hb + hh
            cs = slice(hh * HEAD, (hh + 1) * HEAD)
            s_p, s_c = _band_scores(q_ref[:, cs], kp_ref[:, cs], kc_ref[:, cs], h, c, d, n_heads)
            m_new = jnp.maximum(jnp.max(s_c, axis=1, keepdims=True), jnp.max(s_p, axis=1, keepdims=True))
            if not first:
                m_prev = _take_lane(m_tile, h)
                m_new = jnp.maximum(m_new, m_prev)
            p_c = jnp.exp(s_c - m_new)
            p_p = jnp.exp(s_p - m_new)
            l_new = jnp.sum(p_c, axis=1, keepdims=True) + jnp.sum(p_p, axis=1, keepdims=True)
            nn = (((1,), (0,)), ((), ()))
            acc = (lax.dot_general(p_c.astype(BF16), vc_ref[:, cs], nn, preferred_element_type=F32)
                   + lax.dot_general(p_p.astype(BF16), vp_ref[:, cs], nn, preferred_element_type=F32))
            if not first:
                corr = jnp.exp(m_prev - m_new)
                l_new = l_new + corr * _take_lane(l_tile, h)
                acc = acc + corr * acc_in[:, cs]
            lane = _lane_is(h, (HEAD, HEAD))
            if last:
                o_ref[:, cs] = (acc / l_new).astype(BF16)
                stat_a = jnp.where(lane, m_new + jnp.log(l_new), stat_a)
            else:
                acc_out[:, cs] = acc
                stat_a = jnp.where(lane, m_new, stat_a)
                stat_b = jnp.where(lane, l_new, stat_b)
        if last:
            @pl.when(g == 0)
            def _():
                lse_ref[...] = stat_a

            @pl.when(g > 0)
            def _():
                lse_ref[...] = _merge_lanes(lse_ref[...], stat_a, g * hb, hb)
        else:
            @pl.when(g == 0)
            def _():
                m_out[...] = stat_a
                l_out[...] = stat_b

            @pl.when(g > 0)
            def _():
                m_out[...] = _merge_lanes(m_out[...], stat_a, g * hb, hb)
                l_out[...] = _merge_lanes(l_out[...], stat_b, g * hb, hb)

    blk = (HEAD, hb * HEAD)
    prev = lambda c: jnp.maximum(c - 1, 0)
    in_specs = [
        pl.BlockSpec(blk, lambda r, c, g: (c, r * 3 * groups + g)),
        pl.BlockSpec(blk, lambda r, c, g: (prev(c), r * 3 * groups + groups + g)),
        pl.BlockSpec(blk, lambda r, c, g: (c, r * 3 * groups + groups + g)),
        pl.BlockSpec(blk, lambda r, c, g: (prev(c), r * 3 * groups + 2 * groups + g)),
        pl.BlockSpec(blk, lambda r, c, g: (c, r * 3 * groups + 2 * groups + g)),
    ]
    head_spec = pl.BlockSpec(blk, lambda r, c, g: (c, r * groups + g))
    stat_spec = pl.BlockSpec((HEAD, HEAD), lambda r, c, g: (c, r))
    ins = [qkv_v] * 5
    if not first:
        acc0, m0, l0 = state
        ins += [acc0.reshape(rows, d * w), m0.reshape(rows, d * HEAD), l0.reshape(rows, d * HEAD)]
        in_specs += [head_spec, stat_spec, stat_spec]
    stat_shape = jax.ShapeDtypeStruct((rows, d * HEAD), F32)
    if last:
        out_shape = [jax.ShapeDtypeStruct((rows, d * w), BF16), stat_shape]
        out_specs = [head_spec, stat_spec]
    else:
        out_shape = [jax.ShapeDtypeStruct((rows, d * w), F32), stat_shape, stat_shape]
        out_specs = [head_spec, stat_spec, stat_spec]
    outs = pl.pallas_call(body, name=name, grid=(d, nb, groups), in_specs=in_specs, out_specs=out_specs,
                          out_shape=out_shape, compiler_params=_params(3))(*ins)
    if last:
        return outs[0].reshape(s, w), outs[1].reshape(s, HEAD)
    return outs[0].reshape(s, w), outs[1].reshape(s, HEAD), outs[2].reshape(s, HEAD)


def dilated_bwd(name, qkv, o, do, lse, dilation, n_heads):
    s, w3 = qkv.shape
    w = w3 // 3
    d = dilation
    rows, nb = s // d, s // (d * HEAD)
    hb = min(DILATED_HEADS_PER_STEP, n_heads)
    groups = n_heads // hb

    def body(q_ref, kp_ref, kc_ref, vp_ref, vc_ref, o_ref, do_ref, lse_ref, dq_ref, dk_ref, dv_ref, dk_c, dv_c):
        g, c = pl.program_id(1), pl.program_id(2)

        @pl.when(c == 0)
        def _():
            dk_c[...] = jnp.zeros_like(dk_c)
            dv_c[...] = jnp.zeros_like(dv_c)

        @pl.when(c < nb)
        def _():
            lse_tile = lse_ref[...]
            nt = (((1,), (1,)), ((), ()))
            nn = (((1,), (0,)), ((), ()))
            tn = (((0,), (0,)), ((), ()))
            scale = HEAD ** -0.5
            for hh in range(hb):
                h = g * hb + hh
                cs = slice(hh * HEAD, (hh + 1) * HEAD)
                q, kp, kc, vp, vc, dout = q_ref[:, cs], kp_ref[:, cs], kc_ref[:, cs], vp_ref[:, cs], vc_ref[:, cs], do_ref[:, cs]
                s_p, s_c = _band_scores(q, kp, kc, h, c, d, n_heads)
                lse_h = _take_lane(lse_tile, h)
                delta = jnp.sum(dout.astype(F32) * o_ref[:, cs].astype(F32), axis=1, keepdims=True)
                p_c = jnp.exp(s_c - lse_h)
                p_p = jnp.exp(s_p - lse_h)
                ds_c = (p_c * (lax.dot_general(dout, vc, nt, preferred_element_type=F32) - delta) * scale).astype(BF16)
                ds_p = (p_p * (lax.dot_general(dout, vp, nt, preferred_element_type=F32) - delta) * scale).astype(BF16)
                dq_ref[:, cs] = (lax.dot_general(ds_c, kc, nn, preferred_element_type=F32)
                                 + lax.dot_general(ds_p, kp, nn, preferred_element_type=F32)).astype(BF16)
                dk_ref[:, cs] = (dk_c[:, cs] + lax.dot_general(ds_p, q, tn, preferred_element_type=F32)).astype(BF16)
                dv_ref[:, cs] = (dv_c[:, cs]
                                 + lax.dot_general(p_p.astype(BF16), dout, tn, preferred_element_type=F32)).astype(BF16)
                dk_c[:, cs] = lax.dot_general(ds_c, q, tn, preferred_element_type=F32)
                dv_c[:, cs] = lax.dot_general(p_c.astype(BF16), dout, tn, preferred_element_type=F32)

        @pl.when(c == nb)
        def _():
            dk_ref[...] = dk_c[...].astype(BF16)
            dv_ref[...] = dv_c[...].astype(BF16)

    blk = (HEAD, hb * HEAD)
    cur = lambda c: jnp.minimum(c, nb - 1)
    prev = lambda c: jnp.clip(c - 1, 0, nb - 1)
    in_specs = [
        pl.BlockSpec(blk, lambda r, g, c: (cur(c), r * 3 * groups + g)),
        pl.BlockSpec(blk, lambda r, g, c: (prev(c), r * 3 * groups + groups + g)),
        pl.BlockSpec(blk, lambda r, g, c: (cur(c), r * 3 * groups + groups + g)),
        pl.BlockSpec(blk, lambda r, g, c: (prev(c), r * 3 * groups + 2 * groups + g)),
        pl.BlockSpec(blk, lambda r, g, c: (cur(c), r * 3 * groups + 2 * groups + g)),
        pl.BlockSpec(blk, lambda r, g, c: (cur(c), r * groups + g)),
        pl.BlockSpec(blk, lambda r, g, c: (cur(c), r * groups + g)),
        pl.BlockSpec((HEAD, HEAD), lambda r, g, c: (cur(c), r)),
    ]
    out_specs = [
        pl.BlockSpec(blk, lambda r, g, c: (cur(c), r * groups + g)),
        pl.BlockSpec(blk, lambda r, g, c: (prev(c), r * groups + g)),
        pl.BlockSpec(blk, lambda r, g, c: (prev(c), r * groups + g)),
    ]
    qkv_v = qkv.reshape(rows, d * w3)
    outs = pl.pallas_call(
        body, name=name, grid=(d, groups, nb + 1), in_specs=in_specs, out_specs=out_specs,
        out_shape=[jax.ShapeDtypeStruct((rows, d * w), BF16)] * 3,
        scratch_shapes=[pltpu.VMEM(blk, F32)] * 2, compiler_params=_params(3),
    )(qkv_v, qkv_v, qkv_v, qkv_v, qkv_v, o.reshape(rows, d * w), do.reshape(rows, d * w), lse.reshape(rows, d * HEAD))
    return [t.reshape(s, w) for t in outs]


def sum_branches(name, parts, *, tm=256):
    s, w = parts[0][0].shape
    tm = _tile(s, tm, 8)

    def body(*refs):
        out = refs[9]
        for t in range(3):
            acc = refs[t][...].astype(F32) + refs[3 + t][...].astype(F32) + refs[6 + t][...].astype(F32)
            out[:, t * w:(t + 1) * w] = acc.astype(BF16)

    row = pl.BlockSpec((tm, w), lambda i: (i, 0))
    return pl.pallas_call(
        body, name=name, grid=(s // tm,), in_specs=[row] * 9, out_specs=pl.BlockSpec((tm, 3 * w), lambda i: (i, 0)),
        out_shape=jax.ShapeDtypeStruct((s, 3 * w), BF16), compiler_params=_params(1),
    )(*[t for trip in parts for t in trip])


def _triangle(nq, n_heads, by_key):
    a, hh, b = [], [], []
    for outer in range(nq):
        for h in range(n_heads):
            inner = range(outer, nq) if by_key else range(outer + 1)
            for t in inner:
                a.append(outer)
                hh.append(h)
                b.append(t)
    return (jnp.asarray(np.array(a, np.int32)), jnp.asarray(np.array(hh, np.int32)), jnp.asarray(np.array(b, np.int32)))


def _causal_mask(qi, ki, t):
    row = lax.broadcasted_iota(jnp.int32, (t, t), 0) + qi * t
    col = lax.broadcasted_iota(jnp.int32, (t, t), 1) + ki * t
    return col <= row


def _causal_scores(q, kn, kr, mask):
    nt = (((1,), (1,)), ((), ()))
    k = jnp.concatenate([kn, kr], axis=1)
    s = lax.dot_general(q, k, nt, preferred_element_type=F32)
    return s if mask is None else jnp.where(mask, s, MASK_VALUE)


def _on_and_off_diagonal(qi, ki, t, step):
    @pl.when(ki == qi)
    def _():
        step(_causal_mask(qi, ki, t))

    @pl.when(ki != qi)
    def _():
        step(None)


def _heads(hh):
    return slice(hh * QK_PAD, (hh + 1) * QK_PAD), slice(hh * HEAD, (hh + 1) * HEAD)


def mla_fwd(name, q, kn, kr, v, n_heads, *, t=512):
    s = q.shape[0]
    t = _tile(s, t)
    nq = s // t
    hb = min(MLA_FWD_HEADS_PER_STEP, n_heads)
    qt, gt, kt = _triangle(nq, n_heads // hb, by_key=False)

    def body(qt_ref, gt_ref, kt_ref, q_ref, kn_ref, kr_ref, v_ref, o_ref, lse_ref, m_sc, acc_sc):
        step = pl.program_id(0)
        qi, g, ki = qt_ref[step], gt_ref[step], kt_ref[step]

        @pl.when(ki == 0)
        def _():
            m_sc[...] = jnp.full_like(m_sc, MASK_VALUE)
            acc_sc[...] = jnp.zeros_like(acc_sc)

        def scores_step(mask):
            kr_v = kr_ref[...]
            ones = jnp.ones((t, HEAD), BF16)
            for hh in range(hb):
                wide, cs = _heads(hh)
                sc = _causal_scores(q_ref[:, wide], kn_ref[:, cs], kr_v, mask)
                m_prev = m_sc[hh]
                m_new = jnp.maximum(m_prev, jnp.max(sc, axis=1, keepdims=True))
                p = jnp.exp2(sc - m_new)
                v_ext = jnp.concatenate([v_ref[:, cs], ones], axis=1)
                acc_sc[hh] = jnp.exp2(m_prev - m_new) * acc_sc[hh] + lax.dot_general(
                    p.astype(BF16), v_ext, (((1,), (0,)), ((), ())), preferred_element_type=F32)
                m_sc[hh] = m_new

        _on_and_off_diagonal(qi, ki, t, scores_step)

        @pl.when(ki == qi)
        def _():
            @pl.when(g == 0)
            def _():
                lse_ref[...] = jnp.zeros_like(lse_ref)

            tile = lse_ref[...]
            for hh in range(hb):
                acc = acc_sc[hh]
                l = acc[:, HEAD:]
                o_ref[:, _heads(hh)[1]] = (acc[:, :HEAD] / l).astype(BF16)
                tile = jnp.where(_lane_is(g * hb + hh, (t, HEAD)), m_sc[hh] + jnp.log2(l), tile)
            lse_ref[...] = tile

    grid_spec = pltpu.PrefetchScalarGridSpec(
        num_scalar_prefetch=3, grid=(int(qt.shape[0]),),
        in_specs=[
            pl.BlockSpec((t, hb * QK_PAD), lambda i, a, b, c: (a[i], b[i])),
            pl.BlockSpec((t, hb * HEAD), lambda i, a, b, c: (c[i], b[i])),
            pl.BlockSpec((t, HEAD), lambda i, a, b, c: (c[i], 0)),
            pl.BlockSpec((t, hb * HEAD), lambda i, a, b, c: (c[i], b[i])),
        ],
        out_specs=[pl.BlockSpec((t, hb * HEAD), lambda i, a, b, c: (a[i], b[i])),
                   pl.BlockSpec((t, HEAD), lambda i, a, b, c: (a[i], 0))],
        scratch_shapes=[pltpu.VMEM((hb, t, 1), F32), pltpu.VMEM((hb, t, 2 * HEAD), F32)],
    )
    return pl.pallas_call(
        body, name=name, grid_spec=grid_spec,
        out_shape=[jax.ShapeDtypeStruct((s, n_heads * HEAD), BF16), jax.ShapeDtypeStruct((s, HEAD), F32)],
        compiler_params=_params(1),
    )(qt, gt, kt, q, kn, kr, v)


def mla_bwd_dq(name, q, kn, kr, v, o, do, lse, tabs, n_heads, *, t=512):
    s = q.shape[0]
    t = _tile(s, t)
    nq = s // t
    hb = min(MLA_BWD_HEADS_PER_STEP, n_heads)
    qt, gt, kt = _triangle(nq, n_heads // hb, by_key=False)

    def body(qt_ref, gt_ref, kt_ref, q_ref, kn_ref, kr_ref, v_ref, o_ref, do_ref, lse_ref, cs_ref, s1_ref, s2_ref,
             dq_ref, delta_ref, acc_sc, delta_sc):
        step = pl.program_id(0)
        qi, g, ki = qt_ref[step], gt_ref[step], kt_ref[step]

        @pl.when(ki == 0)
        def _():
            acc_sc[...] = jnp.zeros_like(acc_sc)

            @pl.when(g == 0)
            def _():
                delta_ref[...] = jnp.zeros_like(delta_ref)

            tile = delta_ref[...]
            for hh in range(hb):
                cs = _heads(hh)[1]
                delta = jnp.sum(do_ref[:, cs].astype(F32) * o_ref[:, cs].astype(F32), axis=1, keepdims=True)
                delta_sc[hh] = delta
                tile = jnp.where(_lane_is(g * hb + hh, (t, HEAD)), delta, tile)
            delta_ref[...] = tile

        def scores_step(mask):
            kr_v, lse_tile = kr_ref[...], lse_ref[...]
            for hh in range(hb):
                wide, cs = _heads(hh)
                kn_v = kn_ref[:, cs]
                sc = _causal_scores(q_ref[:, wide], kn_v, kr_v, mask)
                p = jnp.exp2(sc - _take_lane(lse_tile, g * hb + hh))
                dp = lax.dot_general(do_ref[:, cs], v_ref[:, cs], (((1,), (1,)), ((), ())), preferred_element_type=F32)
                ds = (p * (dp - delta_sc[hh])).astype(BF16)
                k = jnp.concatenate([kn_v, kr_v], axis=1)
                acc_sc[hh] += lax.dot_general(ds, k, (((1,), (0,)), ((), ())), preferred_element_type=F32)

        _on_and_off_diagonal(qi, ki, t, scores_step)

        @pl.when(ki == qi)
        def _():
            for hh in range(hb):
                dq = acc_sc[hh]
                c0 = hh * QK_PAD
                dq_ref[:, c0:c0 + HEAD] = dq[:, :HEAD].astype(BF16)
                dq_ref[:, c0 + HEAD:c0 + QK_PAD] = _rope_tile(dq[:, HEAD:], cs_ref[...], s1_ref[...],
                                                              s2_ref[...]).astype(BF16)

    tab = pl.BlockSpec((t, HEAD), lambda i, a, b, c: (a[i], 0))
    grid_spec = pltpu.PrefetchScalarGridSpec(
        num_scalar_prefetch=3, grid=(int(qt.shape[0]),),
        in_specs=[
            pl.BlockSpec((t, hb * QK_PAD), lambda i, a, b, c: (a[i], b[i])),
            pl.BlockSpec((t, hb * HEAD), lambda i, a, b, c: (c[i], b[i])),
            pl.BlockSpec((t, HEAD), lambda i, a, b, c: (c[i], 0)),
            pl.BlockSpec((t, hb * HEAD), lambda i, a, b, c: (c[i], b[i])),
            pl.BlockSpec((t, hb * HEAD), lambda i, a, b, c: (a[i], b[i])),
            pl.BlockSpec((t, hb * HEAD), lambda i, a, b, c: (a[i], b[i])),
            tab, tab, tab, tab,
        ],
        out_specs=[pl.BlockSpec((t, hb * QK_PAD), lambda i, a, b, c: (a[i], b[i])), tab],
        scratch_shapes=[pltpu.VMEM((hb, t, QK_PAD), F32), pltpu.VMEM((hb, t, 1), F32)],
    )
    return pl.pallas_call(
        body, name=name, grid_spec=grid_spec,
        out_shape=[jax.ShapeDtypeStruct((s, n_heads * QK_PAD), BF16), jax.ShapeDtypeStruct((s, HEAD), F32)],
        compiler_params=_params(1),
    )(qt, gt, kt, q, kn, kr, v, o, do, lse, *tabs)


def mla_bwd_dkv(name, q, kn, kr, v, do, lse, delta, prev, n_heads, inv_q_scale, inv_do_scale, *, t=512):
    s = q.shape[0]
    t = _tile(s, t)
    nq = s // t
    hb = min(MLA_BWD_HEADS_PER_STEP, n_heads)
    kt, gt, qt = _triangle(nq, n_heads // hb, by_key=True)
    has_prev = prev is not None

    def body(*refs):
        kt_ref, gt_ref, qt_ref, q_ref, kn_ref, kr_ref, v_ref, do_ref, lse_ref, delta_ref = refs[:10]
        refs = refs[10:]
        if has_prev:
            pkn_ref, pkr_ref, pv_ref = refs[:3]
            refs = refs[3:]
        dkn_ref, dkr_ref, dv_ref, dk_sc, dv_sc = refs
        step = pl.program_id(0)
        ki, g, qi = kt_ref[step], gt_ref[step], qt_ref[step]

        @pl.when(qi == ki)
        def _():
            dk_sc[...] = jnp.zeros_like(dk_sc)
            dv_sc[...] = jnp.zeros_like(dv_sc)

        def scores_step(mask):
            kr_v, lse_tile, delta_tile = kr_ref[...], lse_ref[...], delta_ref[...]
            tn = (((0,), (0,)), ((), ()))
            for hh in range(hb):
                wide, cs = _heads(hh)
                q_v, dout = q_ref[:, wide], do_ref[:, cs]
                sc = _causal_scores(q_v, kn_ref[:, cs], kr_v, mask)
                p = jnp.exp2(sc - _take_lane(lse_tile, g * hb + hh))
                dp = lax.dot_general(dout, v_ref[:, cs], (((1,), (1,)), ((), ())), preferred_element_type=F32)
                ds = (p * (dp - _take_lane(delta_tile, g * hb + hh))).astype(BF16)
                dk_sc[hh] += lax.dot_general(ds, q_v, tn, preferred_element_type=F32)
                dv_sc[hh] += lax.dot_general(p.astype(BF16), dout, tn, preferred_element_type=F32)

        _on_and_off_diagonal(qi, ki, t, scores_step)

        @pl.when(qi == nq - 1)
        def _():
            @pl.when(g == 0)
            def _():
                dkr_ref[...] = pkr_ref[...] if has_prev else jnp.zeros_like(dkr_ref)

            dkr = dkr_ref[...]
            for hh in range(hb):
                cs = _heads(hh)[1]
                dk = dk_sc[hh] * inv_q_scale
                dkn, dv = dk[:, :HEAD], dv_sc[hh] * inv_do_scale
                if has_prev:
                    dkn = dkn + pkn_ref[:, cs].astype(F32)
                    dv = dv + pv_ref[:, cs].astype(F32)
                dkn_ref[:, cs] = dkn.astype(BF16)
                dv_ref[:, cs] = dv.astype(BF16)
                dkr = dkr + dk[:, HEAD:]
            dkr_ref[...] = dkr

    head_q = pl.BlockSpec((t, hb * HEAD), lambda i, a, b, c: (c[i], b[i]))
    head_k = pl.BlockSpec((t, hb * HEAD), lambda i, a, b, c: (a[i], b[i]))
    shared_k = pl.BlockSpec((t, HEAD), lambda i, a, b, c: (a[i], 0))
    in_specs = [
        pl.BlockSpec((t, hb * QK_PAD), lambda i, a, b, c: (c[i], b[i])), head_k, shared_k, head_k, head_q,
        pl.BlockSpec((t, HEAD), lambda i, a, b, c: (c[i], 0)), pl.BlockSpec((t, HEAD), lambda i, a, b, c: (c[i], 0)),
    ]
    ins = [q, kn, kr, v, do, lse, delta]
    if has_prev:
        in_specs += [head_k, shared_k, head_k]
        ins += list(prev)
    grid_spec = pltpu.PrefetchScalarGridSpec(
        num_scalar_prefetch=3, grid=(int(kt.shape[0]),), in_specs=in_specs, out_specs=[head_k, shared_k, head_k],
        scratch_shapes=[pltpu.VMEM((hb, t, QK_PAD), F32), pltpu.VMEM((hb, t, HEAD), F32)],
    )
    return pl.pallas_call(
        body, name=name, grid_spec=grid_spec,
        out_shape=[jax.ShapeDtypeStruct((s, n_heads * HEAD), BF16), jax.ShapeDtypeStruct((s, HEAD), F32),
                   jax.ShapeDtypeStruct((s, n_heads * HEAD), BF16)],
        compiler_params=_params(1),
    )(kt, gt, qt, *ins)


def rope_rows(name, z, tabs, *, tm=512):
    s = z.shape[0]
    tm = _tile(s, tm, 8)

    def body(z_ref, cs_ref, s1_ref, s2_ref, o_ref):
        o_ref[...] = _rope_tile(z_ref[...], cs_ref[...], s1_ref[...], s2_ref[...]).astype(BF16)

    row = pl.BlockSpec((tm, HEAD), lambda i: (i, 0))
    return pl.pallas_call(body, name=name, grid=(s // tm,), in_specs=[row] * 4, out_specs=row,
                          out_shape=jax.ShapeDtypeStruct((s, HEAD), BF16), compiler_params=_params(1))(z, *tabs)


def rope_tables(s):
    inv = 1.0 / (ROPE_THETA ** (jnp.arange(0, ROPE, 2, dtype=F32) / ROPE))
    ang = jnp.arange(s, dtype=F32)[:, None] * inv[None, :]
    cos, sin, zero = jnp.cos(ang), jnp.sin(ang), jnp.zeros((s, ROPE // 2), F32)
    pad = jnp.zeros((s, HEAD - ROPE), F32)
    cs = jnp.concatenate([cos, cos, pad], axis=1)
    lo = jnp.concatenate([sin, zero, pad], axis=1)
    hi = jnp.concatenate([zero, sin, pad], axis=1)
    return (cs, -lo, hi), (cs, lo, -hi)


def local_step(x, target, w, norms):
    s, d = x.shape
    depth = norms['ffn_norm1'].shape[0]
    n_a = w['a_wqkv'].shape[0]
    gw = {k: lax.empty(v.shape, BF16) for k, v in w.items()}
    n_heads = d // HEAD
    fwd_tabs, bwd_tabs = rope_tables(s)
    mla_scale = (HEAD + ROPE) ** -0.5
    q_scale = mla_scale * math.log2(math.e)
    gain = lambda name, i: norms[name][i:i + 1]
    saved = []
    kv = None

    def ffn_fwd(tag, x, g_norm, which, layer):
        wg, wu, wd = ((w[f'{which}_{nm}'], layer) for nm in ('wg', 'wu', 'wd'))
        h = norm_fwd(f'{tag}_norm', x, g_norm)
        g, u, a = ffn_up(f'{tag}_up', h, wg, wu)
        y = mm_nn(f'{tag}_down', a, wd, out_dtype=F32, res=x, alpha=0.5)
        return y, (x, h, g, u, a)

    def ffn_bwd(tag, dx, dxb, g_norm, which, layer, sv):
        wg, wu, wd = ((w[f'{which}_{nm}'], layer) for nm in ('wg', 'wu', 'wd'))
        x, h, g, u, a = sv
        dg, du = mm_nt(f'{tag}_bwd_da', [dxb], [wd], out_dtype=BF16, alpha=0.5, swiglu=(g, u))
        gw[f'{which}_wd'] = mm_tn(f'{tag}_bwd_dwd', a, dxb, (gw[f'{which}_wd'], layer), alpha=0.5)
        gw[f'{which}_wg'] = mm_tn(f'{tag}_bwd_dwg', h, dg, (gw[f'{which}_wg'], layer))
        gw[f'{which}_wu'] = mm_tn(f'{tag}_bwd_dwu', h, du, (gw[f'{which}_wu'], layer))
        dh = mm_nt(f'{tag}_bwd_dh', [dg, du], [wg, wu], out_dtype=F32)
        return norm_bwd(f'{tag}_bwd_norm', x, g_norm, dh, dx)

    for layer in range(depth):
        if layer == n_a:
            hk = norm_fwd('kv_norm', x, norms['kv_norm'])
            ckv_raw = mm_nn('kv_down', hk, (w['b_wdkv'], 0), out_dtype=F32)
            ckv = norm_fwd('kv_cnorm', ckv_raw, norms['b_ckv_norm'])
            k_nope = mm_nn('kv_uk', ckv, (w['b_wuk'], 0), out_dtype=BF16)
            v_shared = mm_nn('kv_uv', ckv, (w['b_wuv'], 0), out_dtype=BF16)
            k_rope = mm_nn('kv_kr', hk, (w['b_wkr'], 0), out_dtype=BF16, rope=('k',) + fwd_tabs, tn=HEAD)
            kv = (x, hk, ckv_raw, ckv, k_nope, k_rope, v_shared)
        x, sv1 = ffn_fwd(f'l{layer}_ffn1', x, gain('ffn_norm1', layer), 'ffn1', layer)
        h = norm_fwd(f'l{layer}_mix_norm', x, gain('mix_norm', layer))
        if layer < n_a:
            qkv = mm_nn(f'l{layer}_qkv', h, (w['a_wqkv'], layer), out_dtype=BF16)
            state = None
            for bi, (window, dilation) in enumerate(DILATED_BRANCHES):
                last = bi == len(DILATED_BRANCHES) - 1
                state = dilated_fwd(f'l{layer}_dil{dilation}', qkv, state, dilation, n_heads, last)
            o, lse = state
            x_new = mm_nn(f'l{layer}_wo', o, (w['a_wo'], layer), out_dtype=F32, res=x)
            svm = (x, h, qkv, o, lse)
        else:
            jb = layer - n_a
            cq_raw = mm_nn(f'l{layer}_dq', h, (w['b_wdq'], jb), out_dtype=F32)
            cq = norm_fwd(f'l{layer}_cq_norm', cq_raw, gain('b_cq_norm', jb))
            q = mm_nn(f'l{layer}_uq', cq, (w['b_wuq'], jb), out_dtype=BF16, alpha=q_scale, rope=('q',) + fwd_tabs)
            o, lse = mla_fwd(f'l{layer}_mla', q, kv[4], kv[5], kv[6], n_heads)
            x_new = mm_nn(f'l{layer}_wo', o, (w['b_wo'], jb), out_dtype=F32, res=x)
            svm = (x, h, cq_raw, cq, q, o, lse)
        x = x_new
        x, sv2 = ffn_fwd(f'l{layer}_ffn2', x, gain('ffn_norm2', layer), 'ffn2', layer)
        saved.append((sv1, svm, sv2))

    loss, dx, dxb, d_final = loss_head('loss_head', x, norms['final_norm'], target)

    gn = {k: [None] * v.shape[0] for k, v in norms.items()}
    gn['final_norm'] = [d_final]
    dkv = None
    for layer in reversed(range(depth)):
        sv1, svm, sv2 = saved[layer]
        dx, dxb, gn['ffn_norm2'][layer] = ffn_bwd(f'l{layer}_ffn2', dx, dxb, gain('ffn_norm2', layer), 'ffn2', layer, sv2)
        if layer < n_a:
            xm, h, qkv, o, lse = svm
            do = mm_nt(f'l{layer}_bwd_do', [dxb], [(w['a_wo'], layer)], out_dtype=BF16)
            gw['a_wo'] = mm_tn(f'l{layer}_bwd_dwo', o, dxb, (gw['a_wo'], layer))
            parts = [dilated_bwd(f'l{layer}_bwd_dil{dilation}', qkv, o, do, lse, dilation, n_heads)
                     for _, dilation in DILATED_BRANCHES]
            dqkv = sum_branches(f'l{layer}_bwd_sum', parts)
            gw['a_wqkv'] = mm_tn(f'l{layer}_bwd_dwqkv', h, dqkv, (gw['a_wqkv'], layer))
            dh = mm_nt(f'l{layer}_bwd_dh', [dqkv], [(w['a_wqkv'], layer)], out_dtype=F32)
        else:
            jb = layer - n_a
            xm, h, cq_raw, cq, q, o, lse = svm
            do = mm_nt(f'l{layer}_bwd_do', [dxb], [(w['b_wo'], jb)], out_dtype=BF16, alpha=mla_scale)
            gw['b_wo'] = mm_tn(f'l{layer}_bwd_dwo', o, dxb, (gw['b_wo'], jb))
            dq, delta = mla_bwd_dq(f'l{layer}_bwd_mla_dq', q, kv[4], kv[5], kv[6], o, do, lse, bwd_tabs, n_heads)
            dkv = mla_bwd_dkv(f'l{layer}_bwd_mla_dkv', q, kv[4], kv[5], kv[6], do, lse, delta, dkv, n_heads,
                              1.0 / q_scale, 1.0 / mla_scale)
            gw['b_wuq'] = mm_tn(f'l{layer}_bwd_dwuq', cq, dq, (gw['b_wuq'], jb))
            dcq = mm_nt(f'l{layer}_bwd_dcq', [dq], [(w['b_wuq'], jb)], out_dtype=F32)
            _, dcq_raw, gn['b_cq_norm'][jb] = norm_bwd(f'l{layer}_bwd_cq_norm', cq_raw, gain('b_cq_norm', jb), dcq)
            gw['b_wdq'] = mm_tn(f'l{layer}_bwd_dwdq', h, dcq_raw, (gw['b_wdq'], jb))
            dh = mm_nt(f'l{layer}_bwd_dh', [dcq_raw], [(w['b_wdq'], jb)], out_dtype=F32)
        dx, dxb, gn['mix_norm'][layer] = norm_bwd(f'l{layer}_bwd_mix_norm', xm, gain('mix_norm', layer), dh, dx)
        dx, dxb, gn['ffn_norm1'][layer] = ffn_bwd(f'l{layer}_ffn1', dx, dxb, gain('ffn_norm1', layer), 'ffn1', layer, sv1)
        if layer == n_a:
            xk, hk, ckv_raw, ckv, k_nope, k_rope, v_shared = kv
            dkn, dkr_rot, dv = dkv
            dkr = rope_rows('kv_bwd_rope', dkr_rot, bwd_tabs)
            gw['b_wuk'] = mm_tn('kv_bwd_dwuk', ckv, dkn, (gw['b_wuk'], 0))
            gw['b_wuv'] = mm_tn('kv_bwd_dwuv', ckv, dv, (gw['b_wuv'], 0))
            dckv = mm_nt('kv_bwd_dckv', [dkn, dv], [(w['b_wuk'], 0), (w['b_wuv'], 0)], out_dtype=F32)
            _, dckv_raw, g_ckv = norm_bwd('kv_bwd_cnorm', ckv_raw, norms['b_ckv_norm'], dckv)
            gw['b_wdkv'] = mm_tn('kv_bwd_dwdkv', hk, dckv_raw, (gw['b_wdkv'], 0))
            gw['b_wkr'] = mm_tn('kv_bwd_dwkr', hk, dkr, (gw['b_wkr'], 0))
            dhk = mm_nt('kv_bwd_dhk_c', [dckv_raw], [(w['b_wdkv'], 0)], out_dtype=F32)
            dhk = mm_nt('kv_bwd_dhk_r', [dkr], [(w['b_wkr'], 0)], out_dtype=F32, res=dhk)
            dx, dxb, g_kv = norm_bwd('kv_bwd_norm', xk, norms['kv_norm'], dhk, dx)
            gn['kv_norm'], gn['b_ckv_norm'] = [g_kv], [g_ckv]
    grads_n = {k: jnp.concatenate(v, axis=0) for k, v in gn.items()}
    return loss, dx, gw, grads_n


def _place():
    x, y, c = lax.axis_index("x"), lax.axis_index("y"), lax.axis_index("c")
    chips = [(1 - x, y), (x, 1 - y), (1 - x, 1 - y)]
    return x, y, c, 2 * x + y, chips, [2 * cx + cy for cx, cy in chips]


def _remote(src, dst, send_sem, recv_sem, to):
    return pltpu.make_async_remote_copy(src_ref=src, dst_ref=dst, send_sem=send_sem, recv_sem=recv_sem,
                                        device_id=to, device_id_type=MESH)


def cast_to_slab(name, shard, place, *, tr=256):
    l, r, c = shard.shape
    tr = _tile(r, tr, 16)

    def body(place_ref, x_ref, o_ref):
        o_ref[...] = x_ref[...].astype(BF16)

    grid_spec = pltpu.PrefetchScalarGridSpec(
        num_scalar_prefetch=1, grid=(l, r // tr),
        in_specs=[pl.BlockSpec((None, tr, c), lambda i, j, p: (i, j, 0))],
        out_specs=pl.BlockSpec((None, None, tr, c), lambda i, j, p: (i, p[0], j, 0)),
    )
    return pl.pallas_call(body, name=name, grid_spec=grid_spec,
                          out_shape=jax.ShapeDtypeStruct((l, N_CHIPS, r, c), BF16), compiler_params=_params(2))(place, shard)


def all_gather_weights(name, slabs):
    n = len(slabs)

    def body(*refs):
        outs = refs[n:2 * n]
        send_sems, recv_sems = refs[2 * n:]
        x, y, c, me, chips, chip_ids = _place()
        sibling = (x, y, 1 - c)

        def copy(t, k, chip, hc, to):
            rh = outs[t].shape[2] // 2
            blk = outs[t].at[:, chip, pl.ds(hc * rh, rh), :]
            return _remote(blk, blk, send_sems.at[6 * t + k], recv_sems.at[6 * t + k], to)

        first = [copy(t, k, me, c, (*chips[k], c)) for t in range(n) for k in range(3)]
        for cp in first:
            cp.start()
        passed = []
        for t in range(n):
            for k in range(3):
                copy(t, k, chip_ids[k], c, sibling).wait_recv()
                cp = copy(t, 3 + k, chip_ids[k], c, sibling)
                cp.start()
                passed.append(cp)
        for t in range(n):
            for k in range(3):
                copy(t, 3 + k, chip_ids[k], 1 - c, sibling).wait_recv()
        for cp in first + passed:
            cp.wait_send()

    any_spec = pl.BlockSpec(memory_space=pl.ANY)
    return pl.pallas_call(
        body, name=name, in_specs=[any_spec] * n, out_specs=[any_spec] * n,
        out_shape=[jax.ShapeDtypeStruct(a.shape, a.dtype) for a in slabs],
        scratch_shapes=[pltpu.SemaphoreType.DMA((6 * n,)), pltpu.SemaphoreType.DMA((6 * n,))],
        input_output_aliases={t: t for t in range(n)},
        compiler_params=pltpu.CompilerParams(has_side_effects=True),
    )(*slabs)


def exchange_halves(name, parts):
    n = len(parts)

    def body(*refs):
        ins, outs = refs[:n], refs[n:2 * n]
        send_sems, recv_sems = refs[2 * n:]
        x, y, c, me, chips, chip_ids = _place()
        copies = []
        for t in range(n):
            rh = ins[t].shape[2] // 2
            cp = _remote(ins[t].at[:, :, pl.ds((1 - c) * rh, rh), :], outs[t], send_sems.at[t], recv_sems.at[t],
                         (x, y, 1 - c))
            cp.start()
            copies.append(cp)
        for cp in copies:
            cp.wait()

    any_spec = pl.BlockSpec(memory_space=pl.ANY)
    return pl.pallas_call(
        body, name=name, in_specs=[any_spec] * n, out_specs=[any_spec] * n,
        out_shape=[jax.ShapeDtypeStruct((a.shape[0], a.shape[1], a.shape[2] // 2, a.shape[3]), a.dtype) for a in parts],
        scratch_shapes=[pltpu.SemaphoreType.DMA((n,)), pltpu.SemaphoreType.DMA((n,))],
        compiler_params=pltpu.CompilerParams(has_side_effects=True),
    )(*parts)


def add_halves(name, part, recv, place, *, tr=256):
    l, j, rh, c = recv.shape
    tr = _tile(rh, tr, 16)
    nr = rh // tr

    def body(place_ref, p_ref, r_ref, o_ref):
        o_ref[...] = (p_ref[...].astype(F32) + r_ref[...].astype(F32)).astype(BF16)

    blk = (None, None, tr, c)
    grid_spec = pltpu.PrefetchScalarGridSpec(
        num_scalar_prefetch=1, grid=(l, j, nr),
        in_specs=[pl.BlockSpec(blk, lambda a, b, i, p: (a, b, p[1] * nr + i, 0)),
                  pl.BlockSpec(blk, lambda a, b, i, p: (a, b, i, 0))],
        out_specs=pl.BlockSpec(blk, lambda a, b, i, p: (a, b, i, 0)),
    )
    return pl.pallas_call(body, name=name, grid_spec=grid_spec, out_shape=jax.ShapeDtypeStruct(recv.shape, BF16),
                          compiler_params=_params(3))(place, part, recv)


def scatter_to_chips(name, sums):
    n = len(sums)

    def body(*refs):
        ins, outs = refs[:n], refs[n:2 * n]
        send_sems, recv_sems = refs[2 * n:]
        x, y, c, me, chips, chip_ids = _place()
        copies = []
        for t in range(n):
            for k in range(3):
                cp = _remote(ins[t].at[:, chip_ids[k]], outs[t].at[k], send_sems.at[3 * t + k], recv_sems.at[3 * t + k],
                             (*chips[k], c))
                cp.start()
                copies.append(cp)
        for cp in copies:
            cp.wait()

    any_spec = pl.BlockSpec(memory_space=pl.ANY)
    return pl.pallas_call(
        body, name=name, in_specs=[any_spec] * n, out_specs=[any_spec] * n,
        out_shape=[jax.ShapeDtypeStruct((3, a.shape[0], a.shape[2], a.shape[3]), a.dtype) for a in sums],
        scratch_shapes=[pltpu.SemaphoreType.DMA((3 * n,)), pltpu.SemaphoreType.DMA((3 * n,))],
        compiler_params=pltpu.CompilerParams(has_side_effects=True),
    )(*sums)


def reduce_own(name, part, recv1, recv2, place, *, tr=256):
    l, j, rh, c = recv1.shape
    tr = _tile(rh, tr, 16)
    nr = rh // tr

    def body(place_ref, p_ref, r1_ref, a_ref, b_ref, c_ref, o_ref):
        acc = p_ref[...].astype(F32) + r1_ref[...].astype(F32)
        acc = acc + a_ref[...].astype(F32)
        acc = acc + b_ref[...].astype(F32)
        o_ref[...] = acc + c_ref[...].astype(F32)

    blk = (None, None, tr, c)
    grid_spec = pltpu.PrefetchScalarGridSpec(
        num_scalar_prefetch=1, grid=(l, nr),
        in_specs=[pl.BlockSpec(blk, lambda a, i, p: (a, p[0], p[1] * nr + i, 0)),
                  pl.BlockSpec(blk, lambda a, i, p: (a, p[0], i, 0)),
                  pl.BlockSpec(blk, lambda a, i, p: (0, a, i, 0)),
                  pl.BlockSpec(blk, lambda a, i, p: (1, a, i, 0)),
                  pl.BlockSpec(blk, lambda a, i, p: (2, a, i, 0))],
        out_specs=pl.BlockSpec((None, tr, c), lambda a, i, p: (a, p[1] * nr + i, 0)),
    )
    return pl.pallas_call(body, name=name, grid_spec=grid_spec, out_shape=jax.ShapeDtypeStruct((l, 2 * rh, c), F32),
                          compiler_params=_params(2))(place, part, recv1, recv2, recv2, recv2)


def share_halves(name, grads):
    n = len(grads)

    def body(*refs):
        outs = refs[n:2 * n]
        send_sems, recv_sems = refs[2 * n:]
        x, y, c, me, chips, chip_ids = _place()
        copies = []
        for t in range(n):
            rh = outs[t].shape[1] // 2
            mine = outs[t].at[:, pl.ds(c * rh, rh), :]
            cp = _remote(mine, mine, send_sems.at[t], recv_sems.at[t], (x, y, 1 - c))
            cp.start()
            copies.append(cp)
        for t, cp in enumerate(copies):
            rh = outs[t].shape[1] // 2
            theirs = outs[t].at[:, pl.ds((1 - c) * rh, rh), :]
            cp.wait_send()
            _remote(theirs, theirs, send_sems.at[t], recv_sems.at[t], (x, y, 1 - c)).wait_recv()

    any_spec = pl.BlockSpec(memory_space=pl.ANY)
    return pl.pallas_call(
        body, name=name, in_specs=[any_spec] * n, out_specs=[any_spec] * n,
        out_shape=[jax.ShapeDtypeStruct(a.shape, a.dtype) for a in grads],
        scratch_shapes=[pltpu.SemaphoreType.DMA((n,)), pltpu.SemaphoreType.DMA((n,))],
        input_output_aliases={t: t for t in range(n)},
        compiler_params=pltpu.CompilerParams(has_side_effects=True),
    )(*grads)


def all_reduce_small(name, packed):
    r = packed.shape[0]

    def body(x_ref, o_ref, buf, send_sems, recv_sems):
        x, y, c = lax.axis_index("x"), lax.axis_index("y"), lax.axis_index("c")
        me = 4 * x + 2 * y + c
        buf[me] = x_ref[...]
        copies = []
        for k in range(1, 8):
            to = (x ^ (k >> 2), y ^ ((k >> 1) & 1), c ^ (k & 1))
            cp = _remote(x_ref, buf.at[me], send_sems.at[k - 1], recv_sems.at[k - 1], to)
            cp.start()
            copies.append(cp)
        for k in range(1, 8):
            peer = me ^ k
            _remote(x_ref, buf.at[peer], send_sems.at[k - 1], recv_sems.at[k - 1], (x, y, c)).wait_recv()
        for cp in copies:
            cp.wait_send()
        acc = buf[0]
        for dev in range(1, 8):
            acc = acc + buf[dev]
        o_ref[...] = acc

    vmem = pl.BlockSpec(memory_space=pltpu.VMEM)
    return pl.pallas_call(
        body, name=name, in_specs=[vmem], out_specs=vmem, out_shape=jax.ShapeDtypeStruct(packed.shape, F32),
        scratch_shapes=[pltpu.VMEM((8, r, HEAD), F32), pltpu.SemaphoreType.DMA((7,)), pltpu.SemaphoreType.DMA((7,))],
    )(packed)


def adamw(name, w, g, m, v, *, tr=256):
    l, r, c = w.shape
    tr = _tile(r, tr, 8)

    def body(w_ref, g_ref, m_ref, v_ref, d_ref, nm_ref, nv_ref):
        gv = g_ref[...]
        nm = ADAM_B1 * m_ref[...] + (1.0 - ADAM_B1) * gv
        nv = ADAM_B2 * v_ref[...] + (1.0 - ADAM_B2) * (gv * gv)
        m_hat = nm / (1.0 - ADAM_B1 ** ADAM_STEP)
        v_hat = nv / (1.0 - ADAM_B2 ** ADAM_STEP)
        d_ref[...] = -ADAM_LR * (m_hat / (jnp.sqrt(v_hat) + ADAM_EPS) + ADAM_WD * w_ref[...])
        nm_ref[...] = nm
        nv_ref[...] = nv

    blk = pl.BlockSpec((None, tr, c), lambda i, j: (i, j, 0))
    shape = jax.ShapeDtypeStruct(w.shape, F32)
    return pl.pallas_call(body, name=name, grid=(l, r // tr), in_specs=[blk] * 4, out_specs=[blk] * 3,
                          out_shape=[shape] * 3, compiler_params=_params(2))(w, g, m, v)


SHARDED = ('ffn1_wg', 'ffn1_wu', 'ffn1_wd', 'ffn2_wg', 'ffn2_wu', 'ffn2_wd', 'a_wqkv', 'a_wo', 'b_wdkv', 'b_wkr', 'b_wuk',
           'b_wuv', 'b_wdq', 'b_wuq', 'b_wo')
COLUMN_SHARDED = ('ffn1_wg', 'ffn1_wu', 'ffn2_wg', 'ffn2_wu', 'a_wqkv')
GAINS = ('ffn_norm1', 'mix_norm', 'ffn_norm2', 'kv_norm', 'b_ckv_norm', 'b_cq_norm', 'final_norm')
WEIGHTS = ('ffn_norm1', 'ffn1_wg', 'ffn1_wu', 'ffn1_wd', 'mix_norm', 'ffn_norm2', 'ffn2_wg', 'ffn2_wu', 'ffn2_wd', 'a_wqkv',
           'a_wo', 'kv_norm', 'b_wdkv', 'b_ckv_norm', 'b_wkr', 'b_wuk', 'b_wuv', 'b_wdq', 'b_cq_norm', 'b_wuq', 'b_wo',
           'final_norm')


def _to_comm_shape(name, a):
    if name == 'b_wkr':
        return jnp.pad(a, ((0, 0), (0, HEAD - ROPE)))[None]
    if name == 'b_wuq':
        l, r, h, _ = a.shape
        return jnp.pad(a, ((0, 0), (0, 0), (0, 0), (0, QK_PAD - HEAD - ROPE))).reshape(l, r, h * QK_PAD)
    if name in ('b_wuk', 'b_wuv'):
        return a.reshape(1, a.shape[0], -1)
    return a[None] if a.ndim == 2 else a


def _from_comm_shape(name, g, like):
    if name == 'b_wkr':
        return g[0, :, :ROPE]
    if name == 'b_wuq':
        l, r, h, e = like.shape
        return g.reshape(l, r, h, QK_PAD)[..., :e]
    return g.reshape(like.shape)


def _as3d(a):
    if a.ndim == 1:
        return a.reshape(1, 1, -1)
    if a.ndim == 2:
        return a[None]
    return a.reshape(a.shape[0], a.shape[1], -1)


def kernel(x, ffn_norm1, ffn1_wg, ffn1_wu, ffn1_wd, mix_norm, ffn_norm2, ffn2_wg, ffn2_wu, ffn2_wd, a_wqkv, a_wo, kv_norm, b_wdkv, b_ckv_norm, b_wkr, b_wuk, b_wuv, b_wdq, b_cq_norm, b_wuq, b_wo, final_norm, loss_target, m_ffn_norm1, m_ffn1_wg, m_ffn1_wu, m_ffn1_wd, m_mix_norm, m_ffn_norm2, m_ffn2_wg, m_ffn2_wu, m_ffn2_wd, m_a_wqkv, m_a_wo, m_kv_norm, m_b_wdkv, m_b_ckv_norm, m_b_wkr, m_b_wuk, m_b_wuv, m_b_wdq, m_b_cq_norm, m_b_wuq, m_b_wo, m_final_norm, v_ffn_norm1, v_ffn1_wg, v_ffn1_wu, v_ffn1_wd, v_mix_norm, v_ffn_norm2, v_ffn2_wg, v_ffn2_wu, v_ffn2_wd, v_a_wqkv, v_a_wo, v_kv_norm, v_b_wdkv, v_b_ckv_norm, v_b_wkr, v_b_wuk, v_b_wuv, v_b_wdq, v_b_cq_norm, v_b_wuq, v_b_wo, v_final_norm):
    args = dict(locals())
    weights = {k: args[k] for k in WEIGHTS}
    place = jnp.stack([2 * lax.axis_index("x") + lax.axis_index("y"), lax.axis_index("c")]).astype(jnp.int32)

    slabs = [cast_to_slab(f'cast_{k}', _to_comm_shape(k, weights[k]), place) for k in SHARDED]
    gathered = all_gather_weights('gather_weights', slabs)
    w = {}
    for k, g in zip(SHARDED, gathered):
        l, j, r, c = g.shape
        w[k] = g if k in COLUMN_SHARDED else g.reshape(l, j * r, c)
    norms = {k: weights[k] if weights[k].ndim == 2 else weights[k][None] for k in GAINS}

    loss, grad_x, gw, gn = local_step(x[0], loss_target[0], w, norms)
    loss = lax.psum(loss[0, 0], ("x", "y", "c"))

    parts = [gw[k].reshape(g.shape) for k, g in zip(SHARDED, gathered)]
    recv1 = exchange_halves('grads_to_sibling', parts)
    sums = [add_halves(f'add_{k}', p, r1, place) for k, p, r1 in zip(SHARDED, parts, recv1)]
    recv2 = scatter_to_chips('grads_to_chips', sums)
    halves = [reduce_own(f'reduce_{k}', p, r1, r2, place) for k, p, r1, r2 in zip(SHARDED, parts, recv1, recv2)]
    reduced = share_halves('grads_share', halves)
    grads = {k: _from_comm_shape(k, g, weights[k]) for k, g in zip(SHARDED, reduced)}

    flat = jnp.concatenate([gn[k].reshape(-1) for k in GAINS])
    rows = -(-flat.shape[0] // (8 * HEAD)) * 8
    packed = jnp.pad(flat, (0, rows * HEAD - flat.shape[0])).reshape(rows, HEAD)
    total = all_reduce_small('gain_grads', packed).reshape(-1)
    off = 0
    for k in GAINS:
        grads[k] = total[off:off + weights[k].size].reshape(weights[k].shape)
        off += weights[k].size

    deltas, new_m, new_v = {}, {}, {}
    for k in WEIGHTS:
        shape = weights[k].shape
        d, nm, nv = adamw(f'adamw_{k}', _as3d(weights[k]), _as3d(grads[k]), _as3d(args['m_' + k]), _as3d(args['v_' + k]))
        deltas[k], new_m[k], new_v[k] = d.reshape(shape), nm.reshape(shape), nv.reshape(shape)
    return (loss, grad_x[None], *[grads[k] for k in WEIGHTS], *[deltas[k] for k in WEIGHTS],
            *[new_m[k] for k in WEIGHTS], *[new_v[k] for k in WEIGHTS])
```

```python
import math

import numpy as np
import jax
import jax.numpy as jnp
from jax import lax
from jax.experimental import pallas as pl
from jax.experimental.pallas import tpu as pltpu

F32 = jnp.float32
BF16 = jnp.bfloat16
NORM_EPS = 1e-6
MASK_VALUE = -1e30
HEAD = 128
ROPE = 64
QK_PAD = 256
DILATED_BRANCHES = ((128, 1), (512, 4), (2048, 16))
ROPE_THETA = 10000.0
ADAM_LR, ADAM_B1, ADAM_B2, ADAM_EPS, ADAM_WD, ADAM_STEP = 0.001, 0.9, 0.999, 1e-08, 0.01, 10
VMEM_LIMIT_BYTES = 56 * 1024 * 1024
MM_ROWS, MM_COLS, MM_WIDE, MM_SHALLOW, MM_DEPTH = 512, 512, 1024, 2048, 8192
MM_TALL = 1024
MM_SUB = 256
ROW_TILE = 256
ATTN_TILE = 512
WGRAD_DEPTH = 4096
DILATED_HEADS_PER_STEP, MLA_FWD_HEADS_PER_STEP, MLA_BWD_HEADS_PER_STEP = 16, 4, 4
N_CHIPS = 4
MESH = pl.DeviceIdType.MESH


def _params(n_axes, vmem=VMEM_LIMIT_BYTES):
    return pltpu.CompilerParams(dimension_semantics=("arbitrary",) * n_axes, vmem_limit_bytes=vmem)


def _tile(n, pref, mult=128):
    if n <= pref:
        return n
    t = (pref // mult) * mult
    while t >= mult:
        if n % t == 0:
            return t
        t -= mult
    return n


def _mm_call(name, grid, ins, in_specs, out_shape, out_specs, n_a, n_b, terms, dims, acc_shapes, epilogue, into=None,
             nt_sub=None):
    n_in, n_out, nk = len(ins), len(out_shape), grid[2]
    aliases = {}
    if into is not None:
        ins = list(ins) + [into]
        in_specs = list(in_specs) + [pl.BlockSpec(memory_space=pl.ANY)]
        aliases = {n_in: 0}

    def body(*refs):
        in_refs, refs = refs[:n_in], refs[n_in + len(aliases):]
        out_refs, accs = refs[:n_out], refs[n_out:]
        a_refs, b_refs, extra = in_refs[:n_a], in_refs[n_a:n_a + n_b], in_refs[n_a + n_b:]
        k = pl.program_id(2)
        if nt_sub is not None:
            for c0 in range(0, acc_shapes[0][1], nt_sub):
                cols = slice(c0, c0 + nt_sub)
                part = None
                for t, ia, ib in terms:
                    one = lax.dot_general(a_refs[ia][...], b_refs[ib][cols, :], (dims, ((), ())), preferred_element_type=F32)
                    part = one if part is None else part + one
                epilogue([part], extra, out_refs, cols)
            return
        sums = [None] * len(acc_shapes)
        for t, ia, ib in terms:
            a = a_refs[ia][...].astype(BF16)
            b = b_refs[ib][...].astype(BF16)
            part = lax.dot_general(a, b, (dims, ((), ())), preferred_element_type=F32)
            sums[t] = part if sums[t] is None else sums[t] + part
        if nk == 1:
            epilogue(sums, extra, out_refs)
            return

        @pl.when(k == 0)
        def _():
            for acc, part in zip(accs, sums):
                acc[...] = part

        @pl.when(k > 0)
        def _():
            for acc, part in zip(accs, sums):
                acc[...] += part

        @pl.when(k == nk - 1)
        def _():
            epilogue([acc[...] for acc in accs], extra, out_refs)

    return pl.pallas_call(
        body, name=name, grid=grid, in_specs=in_specs, out_specs=out_specs, out_shape=out_shape,
        scratch_shapes=[pltpu.VMEM(s, F32) for s in acc_shapes] if nk > 1 else [], input_output_aliases=aliases,
        compiler_params=_params(3),
    )(*ins)


def _b_spec_nn(w, tk, tn):
    b, layer = w
    if b.ndim == 3:
        return pl.BlockSpec((None, tk, tn), lambda i, j, k: (layer, k, j))
    per = b.shape[3] // tn
    return pl.BlockSpec((None, None, tk, tn), lambda i, j, k: (layer, j // per, k, j % per))


def _rope_tile(z, cs, s1, s2):
    return z * cs + pltpu.roll(z, 96, 1) * s1 + pltpu.roll(z, 32, 1) * s2


def mm_nn(name, a, w, *, out_dtype, res=None, alpha=1.0, rope=None, tm=MM_ROWS, tn=None, tk=MM_DEPTH):
    m, kk = a.shape
    b = w[0]
    ns = b.shape[-1]
    n = b.shape[1] * ns if b.ndim == 4 else ns
    tm, tk = _tile(m, MM_TALL if kk > MM_SHALLOW else tm, 8), _tile(kk, tk)
    tn = _tile(ns, tn or (MM_WIDE if kk <= MM_SHALLOW else MM_COLS))
    if rope is not None and rope[0] == 'q':
        assert tn % QK_PAD == 0
    grid = (m // tm, n // tn, kk // tk)
    ins = [a, b]
    in_specs = [pl.BlockSpec((tm, tk), lambda i, j, k: (i, k)), _b_spec_nn(w, tk, tn)]
    if res is not None:
        ins.append(res)
        in_specs.append(pl.BlockSpec((tm, tn), lambda i, j, k: (i, j)))
    if rope is not None:
        ins += list(rope[1:])
        in_specs += [pl.BlockSpec((tm, HEAD), lambda i, j, k: (i, 0))] * 3

    def epilogue(accs, extra, outs):
        y = accs[0]
        extra = list(extra)
        if alpha != 1.0:
            y = alpha * y
        if res is not None:
            y = extra.pop(0)[...] + y
        if rope is not None:
            cs, s1, s2 = (e[...] for e in extra)
            step = QK_PAD if rope[0] == 'q' else HEAD
            for c0 in range(0, tn, step):
                if rope[0] == 'q':
                    outs[0][:, c0:c0 + HEAD] = y[:, c0:c0 + HEAD].astype(out_dtype)
                    c0 += HEAD
                outs[0][:, c0:c0 + HEAD] = _rope_tile(y[:, c0:c0 + HEAD], cs, s1, s2).astype(out_dtype)
        else:
            outs[0][...] = y.astype(out_dtype)

    return _mm_call(name, grid, ins, in_specs, [jax.ShapeDtypeStruct((m, n), out_dtype)],
                    [pl.BlockSpec((tm, tn), lambda i, j, k: (i, j))], 1, 1, [(0, 0, 0)], ((1,), (0,)),
                    [(tm, tn)], epilogue)[0]


def ffn_up(name, h, wg, wu, *, tm=MM_ROWS, tk=MM_DEPTH):
    m, kk = h.shape
    fs = wg[0].shape[3]
    n = wg[0].shape[1] * fs
    tm, tk, tn = _tile(m, tm, 8), _tile(kk, tk), fs
    grid = (m // tm, n // tn, kk // tk)

    def epilogue(accs, extra, outs):
        g, u = accs
        outs[0][...] = g.astype(BF16)
        outs[1][...] = u.astype(BF16)
        outs[2][...] = (g * jax.nn.sigmoid(g) * u).astype(BF16)

    o_spec = pl.BlockSpec((tm, tn), lambda i, j, k: (i, j))
    return _mm_call(name, grid, [h, wg[0], wu[0]],
                    [pl.BlockSpec((tm, tk), lambda i, j, k: (i, k)), _b_spec_nn(wg, tk, tn), _b_spec_nn(wu, tk, tn)],
                    [jax.ShapeDtypeStruct((m, n), BF16)] * 3, [o_spec] * 3, 1, 2, [(0, 0, 0), (1, 0, 1)],
                    ((1,), (0,)), [(tm, tn)] * 2, epilogue)


def mm_nt(name, a_list, w_list, *, out_dtype, alpha=1.0, swiglu=None, res=None, tm=MM_ROWS, tn=None, tk=MM_DEPTH):
    m, n = a_list[0].shape
    b0 = w_list[0][0]
    slab = b0.ndim == 4
    kk = b0.shape[-2]
    ns = b0.shape[-1]
    tk = ns if slab else _tile(ns, tk)
    wide = slab or tk <= MM_SHALLOW
    tm, tn = _tile(m, MM_TALL if swiglu is not None else tm, 8), _tile(kk, tn or (MM_WIDE if wide else MM_COLS))
    grid = (m // tm, kk // tn, n // tk)

    def b_spec(w):
        layer = w[1]
        if slab:
            per = ns // tk
            return pl.BlockSpec((None, None, tn, tk), lambda i, j, k: (layer, k // per, j, k % per))
        return pl.BlockSpec((None, tn, tk), lambda i, j, k: (layer, j, k))

    p = len(a_list)
    ins = list(a_list) + [w[0] for w in w_list]
    in_specs = [pl.BlockSpec((tm, tk), lambda i, j, k: (i, k))] * p + [b_spec(w) for w in w_list]
    o_spec = pl.BlockSpec((tm, tn), lambda i, j, k: (i, j))
    if swiglu is not None:
        ins += list(swiglu)
        in_specs += [o_spec, o_spec]
        out_shape = [jax.ShapeDtypeStruct((m, kk), BF16)] * 2

        def epilogue(accs, extra, outs, cols=slice(None)):
            da = alpha * accs[0]
            g = extra[0][:, cols].astype(F32)
            u = extra[1][:, cols].astype(F32)
            sg = jax.nn.sigmoid(g)
            silu = g * sg
            outs[0][:, cols] = (da * u * (sg + silu * (1.0 - sg))).astype(BF16)
            outs[1][:, cols] = (da * silu).astype(BF16)
    else:
        out_shape = [jax.ShapeDtypeStruct((m, kk), out_dtype)]
        if res is not None:
            ins.append(res)
            in_specs.append(o_spec)

        def epilogue(accs, extra, outs):
            y = alpha * accs[0]
            if res is not None:
                y = y + extra[0][...]
            outs[0][...] = y.astype(out_dtype)

    nt_sub = MM_SUB if swiglu is not None and grid[2] == 1 and tn % MM_SUB == 0 and not slab else None
    outs = _mm_call(name, grid, ins, in_specs, out_shape, [o_spec] * len(out_shape), p, p,
                    [(0, q, q) for q in range(p)], ((1,), (1,)), [(tm, tn)], epilogue, nt_sub=nt_sub)
    return outs if swiglu is not None else outs[0]


def mm_tn(name, a, b, into, *, alpha=1.0, tm=MM_ROWS, tn=MM_WIDE, tk=WGRAD_DEPTH):
    buf, layer = into
    m, kk = a.shape
    n = b.shape[1]
    ns = buf.shape[-1]
    tm, tk = _tile(kk, tm), _tile(m, tk, 8)
    tn = ns if ns % 512 else _tile(ns, tn)
    grid = (kk // tm, n // tn, m // tk)
    if buf.ndim == 4:
        per = ns // tn
        o_spec = pl.BlockSpec((None, None, tm, tn), lambda i, j, k: (layer, j // per, i, j % per))
    else:
        o_spec = pl.BlockSpec((None, tm, tn), lambda i, j, k: (layer, i, j))

    def epilogue(accs, extra, outs):
        outs[0][...] = (alpha * accs[0]).astype(buf.dtype)

    return _mm_call(name, grid, [a, b],
                    [pl.BlockSpec((tk, tm), lambda i, j, k: (k, i)), pl.BlockSpec((tk, tn), lambda i, j, k: (k, j))],
                    [jax.ShapeDtypeStruct(buf.shape, buf.dtype)], [o_spec], 1, 1, [(0, 0, 0)], ((0,), (0,)),
                    [(tm, tn)], epilogue, into=buf)[0]


def norm_fwd(name, x, gain, *, tm=ROW_TILE):
    m, d = x.shape
    tm = _tile(m, tm, 8)

    def body(x_ref, g_ref, o_ref):
        xv = x_ref[...]
        r = lax.rsqrt(jnp.mean(xv * xv, axis=-1, keepdims=True) + NORM_EPS)
        o_ref[...] = (xv * r * g_ref[...]).astype(BF16)

    return pl.pallas_call(
        body, name=name, grid=(m // tm,),
        in_specs=[pl.BlockSpec((tm, d), lambda i: (i, 0)), pl.BlockSpec((1, d), lambda i: (0, 0))],
        out_specs=pl.BlockSpec((tm, d), lambda i: (i, 0)), out_shape=jax.ShapeDtypeStruct((m, d), BF16),
        compiler_params=_params(1),
    )(x, gain)


def norm_bwd(name, x, gain, dh, dres=None, *, tm=ROW_TILE):
    m, d = x.shape
    tm = _tile(m, tm, 8)
    n_steps = m // tm
    has_res = dres is not None

    def body(*refs):
        if has_res:
            x_ref, g_ref, dh_ref, dres_ref, dx_ref, dxb_ref, dg_ref, acc = refs
        else:
            x_ref, g_ref, dh_ref, dx_ref, dxb_ref, dg_ref, acc = refs
        i = pl.program_id(0)
        xv = x_ref[...]
        dy = dh_ref[...].astype(F32)
        r = lax.rsqrt(jnp.mean(xv * xv, axis=-1, keepdims=True) + NORM_EPS)
        xhat = xv * r
        dxhat = dy * g_ref[...]
        dx = r * (dxhat - xhat * jnp.mean(dxhat * xhat, axis=-1, keepdims=True))
        if has_res:
            dx = dx + dres_ref[...]
        dx_ref[...] = dx
        dxb_ref[...] = dx.astype(BF16)

        @pl.when(i == 0)
        def _():
            acc[...] = jnp.zeros_like(acc)

        acc[...] += jnp.sum((dy * xhat).reshape(tm // 8, 8, d), axis=0)

        @pl.when(i == n_steps - 1)
        def _():
            dg_ref[...] = jnp.sum(acc[...], axis=0, keepdims=True)

    row = pl.BlockSpec((tm, d), lambda i: (i, 0))
    vec = pl.BlockSpec((1, d), lambda i: (0, 0))
    ins = [x, gain, dh] + ([dres] if has_res else [])
    return pl.pallas_call(
        body, name=name, grid=(n_steps,), in_specs=[row, vec, row] + ([row] if has_res else []),
        out_specs=[row, row, vec],
        out_shape=[jax.ShapeDtypeStruct((m, d), F32), jax.ShapeDtypeStruct((m, d), BF16), jax.ShapeDtypeStruct((1, d), F32)],
        scratch_shapes=[pltpu.VMEM((8, d), F32)], compiler_params=_params(1),
    )(*ins)


def loss_head(name, x, gain, target, *, tm=ROW_TILE):
    m, d = x.shape
    tm = _tile(m, tm, 8)
    n_steps = m // tm

    def body(x_ref, g_ref, t_ref, loss_ref, dx_ref, dxb_ref, dg_ref, acc, lacc):
        i = pl.program_id(0)
        xv = x_ref[...]
        g = g_ref[...]
        r = lax.rsqrt(jnp.mean(xv * xv, axis=-1, keepdims=True) + NORM_EPS)
        xhat = xv * r
        err = xhat * g - t_ref[...]
        dy = err * (1.0 / d)
        dxhat = dy * g
        dx = r * (dxhat - xhat * jnp.mean(dxhat * xhat, axis=-1, keepdims=True))
        dx_ref[...] = dx
        dxb_ref[...] = dx.astype(BF16)

        @pl.when(i == 0)
        def _():
            acc[...] = jnp.zeros_like(acc)
            lacc[...] = jnp.zeros_like(lacc)

        acc[...] += jnp.sum((dy * xhat).reshape(tm // 8, 8, d), axis=0)
        lacc[...] += jnp.sum((err * err).reshape(tm // 8, 8, d), axis=0)

        @pl.when(i == n_steps - 1)
        def _():
            dg_ref[...] = jnp.sum(acc[...], axis=0, keepdims=True)
            loss_ref[...] = (0.5 / d) * jnp.sum(jnp.sum(lacc[...], axis=0, keepdims=True), axis=1, keepdims=True)

    row = pl.BlockSpec((tm, d), lambda i: (i, 0))
    vec = pl.BlockSpec((1, d), lambda i: (0, 0))
    return pl.pallas_call(
        body, name=name, grid=(n_steps,), in_specs=[row, vec, row],
        out_specs=[pl.BlockSpec((1, 1), lambda i: (0, 0)), row, row, vec],
        out_shape=[jax.ShapeDtypeStruct((1, 1), F32), jax.ShapeDtypeStruct((m, d), F32),
                   jax.ShapeDtypeStruct((m, d), BF16), jax.ShapeDtypeStruct((1, d), F32)],
        scratch_shapes=[pltpu.VMEM((8, d), F32), pltpu.VMEM((8, d), F32)], compiler_params=_params(1),
    )(x, gain, target)


def _lane_is(h, shape):
    return lax.broadcasted_iota(jnp.int32, shape, 1) == h


def _take_lane(tile, h):
    return jnp.sum(jnp.where(_lane_is(h, tile.shape), tile, 0.0), axis=1, keepdims=True)


def _merge_lanes(dst, src, start, count):
    lane = lax.broadcasted_iota(jnp.int32, dst.shape, 1)
    return jnp.where((lane >= start) & (lane < start + count), src, dst)


def _band_scores(q, kp, kc, h, c, dilation, n_heads):
    nt = (((1,), (1,)), ((), ()))
    scale = HEAD ** -0.5
    i = lax.broadcasted_iota(jnp.int32, (HEAD, HEAD), 0)
    j = lax.broadcasted_iota(jnp.int32, (HEAD, HEAD), 1)
    slope = jnp.exp(jnp.full((HEAD, HEAD), -8.0 * math.log(2.0) / n_heads, F32) * (h + 1).astype(F32))
    dist = (i - j).astype(F32) * float(dilation)
    s_c = lax.dot_general(q, kc, nt, preferred_element_type=F32) * scale - slope * dist
    s_p = lax.dot_general(q, kp, nt, preferred_element_type=F32) * scale - slope * (dist + float(dilation * HEAD))
    s_c = jnp.where(j <= i, s_c, MASK_VALUE)
    s_p = jnp.where((j >= i) & (c > 0), s_p, MASK_VALUE)
    return s_p, s_c


def dilated_fwd(name, qkv, state, dilation, n_heads, last):
    s, w3 = qkv.shape
    w = w3 // 3
    d = dilation
    rows, nb = s // d, s // (d * HEAD)
    first = state is None
    qkv_v = qkv.reshape(rows, d * w3)
    hb = min(DILATED_HEADS_PER_STEP, n_heads)
    groups = n_heads // hb

    def body(*refs):
        q_ref, kp_ref, kc_ref, vp_ref, vc_ref = refs[:5]
        refs = refs[5:]
        if not first:
            acc_in, m_in, l_in = refs[:3]
            refs = refs[3:]
        if last:
            o_ref, lse_ref = refs
        else:
            acc_out, m_out, l_out = refs
        c, g = pl.program_id(1), pl.program_id(2)
        if first:
            m_tile = l_tile = jnp.zeros((HEAD, HEAD), F32)
        else:
            m_tile, l_tile = m_in[...], l_in[...]
        stat_a, stat_b = m_tile, l_tile
        for hh in range(hb):
            h = g * hb + hh
            cs = slice(hh * HEAD, (hh + 1) * HEAD)
            s_p, s_c = _band_scores(q_ref[:, cs], kp_ref[:, cs], kc_ref[:, cs], h, c, d, n_heads)
            m_new = jnp.maximum(jnp.max(s_c, axis=1, keepdims=True), jnp.max(s_p, axis=1, keepdims=True))
            if not first:
                m_prev = _take_lane(m_tile, h)
                m_new = jnp.maximum(m_new, m_prev)
            p_c = jnp.exp(s_c - m_new)
            p_p = jnp.exp(s_p - m_new)
            l_new = jnp.sum(p_c, axis=1, keepdims=True) + jnp.sum(p_p, axis=1, keepdims=True)
            nn = (((1,), (0,)), ((), ()))
            acc = (lax.dot_general(p_c.astype(BF16), vc_ref[:, cs], nn, preferred_element_type=F32)
                   + lax.dot_general(p_p.astype(BF16), vp_ref[:, cs], nn, preferred_element_type=F32))
            if not first:
                corr = jnp.exp(m_prev - m_new)
                l_new = l_new + corr * _take_lane(l_tile, h)
                acc = acc + corr * acc_in[:, cs]
            lane = _lane_is(h, (HEAD, HEAD))
            if last:
                o_ref[:, cs] = (acc / l_new).astype(BF16)
                stat_a = jnp.where(lane, m_new + jnp.log(l_new), stat_a)
            else:
                acc_out[:, cs] = acc
                stat_a = jnp.where(lane, m_new, stat_a)
                stat_b = jnp.where(lane, l_new, stat_b)
        if last:
            @pl.when(g == 0)
            def _():
                lse_ref[...] = stat_a

            @pl.when(g > 0)
            def _():
                lse_ref[...] = _merge_lanes(lse_ref[...], stat_a, g * hb, hb)
        else:
            @pl.when(g == 0)
            def _():
                m_out[...] = stat_a
                l_out[...] = stat_b

            @pl.when(g > 0)
            def _():
                m_out[...] = _merge_lanes(m_out[...], stat_a, g * hb, hb)
                l_out[...] = _merge_lanes(l_out[...], stat_b, g * hb, hb)

    blk = (HEAD, hb * HEAD)
    prev = lambda c: jnp.maximum(c - 1, 0)
    in_specs = [
        pl.BlockSpec(blk, lambda r, c, g: (c, r * 3 * groups + g)),
        pl.BlockSpec(blk, lambda r, c, g: (prev(c), r * 3 * groups + groups + g)),
        pl.BlockSpec(blk, lambda r, c, g: (c, r * 3 * groups + groups + g)),
        pl.BlockSpec(blk, lambda r, c, g: (prev(c), r * 3 * groups + 2 * groups + g)),
        pl.BlockSpec(blk, lambda r, c, g: (c, r * 3 * groups + 2 * groups + g)),
    ]
    head_spec = pl.BlockSpec(blk, lambda r, c, g: (c, r * groups + g))
    stat_spec = pl.BlockSpec((HEAD, HEAD), lambda r, c, g: (c, r))
    ins = [qkv_v] * 5
    if not first:
        acc0, m0, l0 = state
        ins += [acc0.reshape(rows, d * w), m0.reshape(rows, d * HEAD), l0.reshape(rows, d * HEAD)]
        in_specs += [head_spec, stat_spec, stat_spec]
    stat_shape = jax.ShapeDtypeStruct((rows, d * HEAD), F32)
    if last:
        out_shape = [jax.ShapeDtypeStruct((rows, d * w), BF16), stat_shape]
        out_specs = [head_spec, stat_spec]
    else:
        out_shape = [jax.ShapeDtypeStruct((rows, d * w), F32), stat_shape, stat_shape]
        out_specs = [head_spec, stat_spec, stat_spec]
    outs = pl.pallas_call(body, name=name, grid=(d, nb, groups), in_specs=in_specs, out_specs=out_specs,
                          out_shape=out_shape, compiler_params=_params(3))(*ins)
    if last:
        return outs[0].reshape(s, w), outs[1].reshape(s, HEAD)
    return outs[0].reshape(s, w), outs[1].reshape(s, HEAD), outs[2].reshape(s, HEAD)


def dilated_bwd(name, qkv, o, do, lse, dilation, n_heads):
    s, w3 = qkv.shape
    w = w3 // 3
    d = dilation
    rows, nb = s // d, s // (d * HEAD)
    hb = min(DILATED_HEADS_PER_STEP, n_heads)
    groups = n_heads // hb

    def body(q_ref, kp_ref, kc_ref, vp_ref, vc_ref, o_ref, do_ref, lse_ref, dq_ref, dk_ref, dv_ref, dk_c, dv_c):
        g, c = pl.program_id(1), pl.program_id(2)

        @pl.when(c == 0)
        def _():
            dk_c[...] = jnp.zeros_like(dk_c)
            dv_c[...] = jnp.zeros_like(dv_c)

        @pl.when(c < nb)
        def _():
            lse_tile = lse_ref[...]
            nt = (((1,), (1,)), ((), ()))
            nn = (((1,), (0,)), ((), ()))
            tn = (((0,), (0,)), ((), ()))
            scale = HEAD ** -0.5
            for hh in range(hb):
                h = g * hb + hh
                cs = slice(hh * HEAD, (hh + 1) * HEAD)
                q, kp, kc, vp, vc, dout = q_ref[:, cs], kp_ref[:, cs], kc_ref[:, cs], vp_ref[:, cs], vc_ref[:, cs], do_ref[:, cs]
                s_p, s_c = _band_scores(q, kp, kc, h, c, d, n_heads)
                lse_h = _take_lane(lse_tile, h)
                delta = jnp.sum(dout.astype(F32) * o_ref[:, cs].astype(F32), axis=1, keepdims=True)
                p_c = jnp.exp(s_c - lse_h)
                p_p = jnp.exp(s_p - lse_h)
                ds_c = (p_c * (lax.dot_general(dout, vc, nt, preferred_element_type=F32) - delta) * scale).astype(BF16)
                ds_p = (p_p * (lax.dot_general(dout, vp, nt, preferred_element_type=F32) - delta) * scale).astype(BF16)
                dq_ref[:, cs] = (lax.dot_general(ds_c, kc, nn, preferred_element_type=F32)
                                 + lax.dot_general(ds_p, kp, nn, preferred_element_type=F32)).astype(BF16)
                dk_ref[:, cs] = (dk_c[:, cs] + lax.dot_general(ds_p, q, tn, preferred_element_type=F32)).astype(BF16)
                dv_ref[:, cs] = (dv_c[:, cs]
                                 + lax.dot_general(p_p.astype(BF16), dout, tn, preferred_element_type=F32)).astype(BF16)
                dk_c[:, cs] = lax.dot_general(ds_c, q, tn, preferred_element_type=F32)
                dv_c[:, cs] = lax.dot_general(p_c.astype(BF16), dout, tn, preferred_element_type=F32)

        @pl.when(c == nb)
        def _():
            dk_ref[...] = dk_c[...].astype(BF16)
            dv_ref[...] = dv_c[...].astype(BF16)

    blk = (HEAD, hb * HEAD)
    cur = lambda c: jnp.minimum(c, nb - 1)
    prev = lambda c: jnp.clip(c - 1, 0, nb - 1)
    in_specs = [
        pl.BlockSpec(blk, lambda r, g, c: (cur(c), r * 3 * groups + g)),
        pl.BlockSpec(blk, lambda r, g, c: (prev(c), r * 3 * groups + groups + g)),
        pl.BlockSpec(blk, lambda r, g, c: (cur(c), r * 3 * groups + groups + g)),
        pl.BlockSpec(blk, lambda r, g, c: (prev(c), r * 3 * groups + 2 * groups + g)),
        pl.BlockSpec(blk, lambda r, g, c: (cur(c), r * 3 * groups + 2 * groups + g)),
        pl.BlockSpec(blk, lambda r, g, c: (cur(c), r * groups + g)),
        pl.BlockSpec(blk, lambda r, g, c: (cur(c), r * groups + g)),
        pl.BlockSpec((HEAD, HEAD), lambda r, g, c: (cur(c), r)),
    ]
    out_specs = [
        pl.BlockSpec(blk, lambda r, g, c: (cur(c), r * groups + g)),
        pl.BlockSpec(blk, lambda r, g, c: (prev(c), r * groups + g)),
        pl.BlockSpec(blk, lambda r, g, c: (prev(c), r * groups + g)),
    ]
    qkv_v = qkv.reshape(rows, d * w3)
    outs = pl.pallas_call(
        body, name=name, grid=(d, groups, nb + 1), in_specs=in_specs, out_specs=out_specs,
        out_shape=[jax.ShapeDtypeStruct((rows, d * w), BF16)] * 3,
        scratch_shapes=[pltpu.VMEM(blk, F32)] * 2, compiler_params=_params(3),
    )(qkv_v, qkv_v, qkv_v, qkv_v, qkv_v, o.reshape(rows, d * w), do.reshape(rows, d * w), lse.reshape(rows, d * HEAD))
    return [t.reshape(s, w) for t in outs]


def sum_branches(name, parts, *, tm=ROW_TILE):
    s, w = parts[0][0].shape
    tm = _tile(s, tm, 8)

    def body(*refs):
        out = refs[9]
        for t in range(3):
            acc = refs[t][...].astype(F32) + refs[3 + t][...].astype(F32) + refs[6 + t][...].astype(F32)
            out[:, t * w:(t + 1) * w] = acc.astype(BF16)

    row = pl.BlockSpec((tm, w), lambda i: (i, 0))
    return pl.pallas_call(
        body, name=name, grid=(s // tm,), in_specs=[row] * 9, out_specs=pl.BlockSpec((tm, 3 * w), lambda i: (i, 0)),
        out_shape=jax.ShapeDtypeStruct((s, 3 * w), BF16), compiler_params=_params(1),
    )(*[t for trip in parts for t in trip])


def _triangle(nq, n_heads, by_key):
    a, hh, b = [], [], []
    for outer in range(nq):
        for h in range(n_heads):
            inner = range(outer, nq) if by_key else range(outer + 1)
            for t in inner:
                a.append(outer)
                hh.append(h)
                b.append(t)
    return (jnp.asarray(np.array(a, np.int32)), jnp.asarray(np.array(hh, np.int32)), jnp.asarray(np.array(b, np.int32)))


def _causal_mask(qi, ki, t):
    row = lax.broadcasted_iota(jnp.int32, (t, t), 0) + qi * t
    col = lax.broadcasted_iota(jnp.int32, (t, t), 1) + ki * t
    return col <= row


def _causal_scores(q, kn, kr, mask):
    nt = (((1,), (1,)), ((), ()))
    k = jnp.concatenate([kn, kr], axis=1)
    s = lax.dot_general(q, k, nt, preferred_element_type=F32)
    return s if mask is None else jnp.where(mask, s, MASK_VALUE)


def _on_and_off_diagonal(qi, ki, t, step):
    @pl.when(ki == qi)
    def _():
        step(_causal_mask(qi, ki, t))

    @pl.when(ki != qi)
    def _():
        step(None)


def _heads(hh):
    return slice(hh * QK_PAD, (hh + 1) * QK_PAD), slice(hh * HEAD, (hh + 1) * HEAD)


def mla_fwd(name, q, kn, kr, v, n_heads, *, t=ATTN_TILE):
    s = q.shape[0]
    t = _tile(s, t)
    nq = s // t
    hb = min(MLA_FWD_HEADS_PER_STEP, n_heads)
    qt, gt, kt = _triangle(nq, n_heads // hb, by_key=False)

    def body(qt_ref, gt_ref, kt_ref, q_ref, kn_ref, kr_ref, v_ref, o_ref, lse_ref, m_sc, acc_sc):
        step = pl.program_id(0)
        qi, g, ki = qt_ref[step], gt_ref[step], kt_ref[step]

        @pl.when(ki == 0)
        def _():
            m_sc[...] = jnp.full_like(m_sc, MASK_VALUE)
            acc_sc[...] = jnp.zeros_like(acc_sc)

        def scores_step(mask):
            kr_v = kr_ref[...]
            ones = jnp.ones((t, HEAD), BF16)
            for hh in range(hb):
                wide, cs = _heads(hh)
                sc = _causal_scores(q_ref[:, wide], kn_ref[:, cs], kr_v, mask)
                m_prev = m_sc[hh]
                m_new = jnp.maximum(m_prev, jnp.max(sc, axis=1, keepdims=True))
                p = jnp.exp2(sc - m_new)
                v_ext = jnp.concatenate([v_ref[:, cs], ones], axis=1)
                acc_sc[hh] = jnp.exp2(m_prev - m_new) * acc_sc[hh] + lax.dot_general(
                    p.astype(BF16), v_ext, (((1,), (0,)), ((), ())), preferred_element_type=F32)
                m_sc[hh] = m_new

        _on_and_off_diagonal(qi, ki, t, scores_step)

        @pl.when(ki == qi)
        def _():
            @pl.when(g == 0)
            def _():
                lse_ref[...] = jnp.zeros_like(lse_ref)

            tile = lse_ref[...]
            for hh in range(hb):
                acc = acc_sc[hh]
                l = acc[:, HEAD:]
                o_ref[:, _heads(hh)[1]] = (acc[:, :HEAD] / l).astype(BF16)
                tile = jnp.where(_lane_is(g * hb + hh, (t, HEAD)), m_sc[hh] + jnp.log2(l), tile)
            lse_ref[...] = tile

    grid_spec = pltpu.PrefetchScalarGridSpec(
        num_scalar_prefetch=3, grid=(int(qt.shape[0]),),
        in_specs=[
            pl.BlockSpec((t, hb * QK_PAD), lambda i, a, b, c: (a[i], b[i])),
            pl.BlockSpec((t, hb * HEAD), lambda i, a, b, c: (c[i], b[i])),
            pl.BlockSpec((t, HEAD), lambda i, a, b, c: (c[i], 0)),
            pl.BlockSpec((t, hb * HEAD), lambda i, a, b, c: (c[i], b[i])),
        ],
        out_specs=[pl.BlockSpec((t, hb * HEAD), lambda i, a, b, c: (a[i], b[i])),
                   pl.BlockSpec((t, HEAD), lambda i, a, b, c: (a[i], 0))],
        scratch_shapes=[pltpu.VMEM((hb, t, 1), F32), pltpu.VMEM((hb, t, 2 * HEAD), F32)],
    )
    return pl.pallas_call(
        body, name=name, grid_spec=grid_spec,
        out_shape=[jax.ShapeDtypeStruct((s, n_heads * HEAD), BF16), jax.ShapeDtypeStruct((s, HEAD), F32)],
        compiler_params=_params(1),
    )(qt, gt, kt, q, kn, kr, v)


def mla_bwd_dq(name, q, kn, kr, v, o, do, lse, tabs, n_heads, *, t=ATTN_TILE):
    s = q.shape[0]
    t = _tile(s, t)
    nq = s // t
    hb = min(MLA_BWD_HEADS_PER_STEP, n_heads)
    qt, gt, kt = _triangle(nq, n_heads // hb, by_key=False)

    def body(qt_ref, gt_ref, kt_ref, q_ref, kn_ref, kr_ref, v_ref, o_ref, do_ref, lse_ref, cs_ref, s1_ref, s2_ref,
             dq_ref, delta_ref, acc_sc, delta_sc):
        step = pl.program_id(0)
        qi, g, ki = qt_ref[step], gt_ref[step], kt_ref[step]

        @pl.when(ki == 0)
        def _():
            acc_sc[...] = jnp.zeros_like(acc_sc)

            @pl.when(g == 0)
            def _():
                delta_ref[...] = jnp.zeros_like(delta_ref)

            tile = delta_ref[...]
            for hh in range(hb):
                cs = _heads(hh)[1]
                delta = jnp.sum(do_ref[:, cs].astype(F32) * o_ref[:, cs].astype(F32), axis=1, keepdims=True)
                delta_sc[hh] = delta
                tile = jnp.where(_lane_is(g * hb + hh, (t, HEAD)), delta, tile)
            delta_ref[...] = tile

        def scores_step(mask):
            kr_v, lse_tile = kr_ref[...], lse_ref[...]
            for hh in range(hb):
                wide, cs = _heads(hh)
                kn_v = kn_ref[:, cs]
                sc = _causal_scores(q_ref[:, wide], kn_v, kr_v, mask)
                p = jnp.exp2(sc - _take_lane(lse_tile, g * hb + hh))
                dp = lax.dot_general(do_ref[:, cs], v_ref[:, cs], (((1,), (1,)), ((), ())), preferred_element_type=F32)
                ds = (p * (dp - delta_sc[hh])).astype(BF16)
                k = jnp.concatenate([kn_v, kr_v], axis=1)
                acc_sc[hh] += lax.dot_general(ds, k, (((1,), (0,)), ((), ())), preferred_element_type=F32)

        _on_and_off_diagonal(qi, ki, t, scores_step)

        @pl.when(ki == qi)
        def _():
            for hh in range(hb):
                dq = acc_sc[hh]
                c0 = hh * QK_PAD
                dq_ref[:, c0:c0 + HEAD] = dq[:, :HEAD].astype(BF16)
                dq_ref[:, c0 + HEAD:c0 + QK_PAD] = _rope_tile(dq[:, HEAD:], cs_ref[...], s1_ref[...],
                                                              s2_ref[...]).astype(BF16)

    tab = pl.BlockSpec((t, HEAD), lambda i, a, b, c: (a[i], 0))
    grid_spec = pltpu.PrefetchScalarGridSpec(
        num_scalar_prefetch=3, grid=(int(qt.shape[0]),),
        in_specs=[
            pl.BlockSpec((t, hb * QK_PAD), lambda i, a, b, c: (a[i], b[i])),
            pl.BlockSpec((t, hb * HEAD), lambda i, a, b, c: (c[i], b[i])),
            pl.BlockSpec((t, HEAD), lambda i, a, b, c: (c[i], 0)),
            pl.BlockSpec((t, hb * HEAD), lambda i, a, b, c: (c[i], b[i])),
            pl.BlockSpec((t, hb * HEAD), lambda i, a, b, c: (a[i], b[i])),
            pl.BlockSpec((t, hb * HEAD), lambda i, a, b, c: (a[i], b[i])),
            tab, tab, tab, tab,
        ],
        out_specs=[pl.BlockSpec((t, hb * QK_PAD), lambda i, a, b, c: (a[i], b[i])), tab],
        scratch_shapes=[pltpu.VMEM((hb, t, QK_PAD), F32), pltpu.VMEM((hb, t, 1), F32)],
    )
    return pl.pallas_call(
        body, name=name, grid_spec=grid_spec,
        out_shape=[jax.ShapeDtypeStruct((s, n_heads * QK_PAD), BF16), jax.ShapeDtypeStruct((s, HEAD), F32)],
        compiler_params=_params(1),
    )(qt, gt, kt, q, kn, kr, v, o, do, lse, *tabs)


def mla_bwd_dkv(name, q, kn, kr, v, do, lse, delta, prev, n_heads, inv_q_scale, inv_do_scale, *, t=ATTN_TILE):
    s = q.shape[0]
    t = _tile(s, t)
    nq = s // t
    hb = min(MLA_BWD_HEADS_PER_STEP, n_heads)
    kt, gt, qt = _triangle(nq, n_heads // hb, by_key=True)
    has_prev = prev is not None

    def body(*refs):
        kt_ref, gt_ref, qt_ref, q_ref, kn_ref, kr_ref, v_ref, do_ref, lse_ref, delta_ref = refs[:10]
        refs = refs[10:]
        if has_prev:
            pkn_ref, pkr_ref, pv_ref = refs[:3]
            refs = refs[3:]
        dkn_ref, dkr_ref, dv_ref, dk_sc, dv_sc = refs
        step = pl.program_id(0)
        ki, g, qi = kt_ref[step], gt_ref[step], qt_ref[step]

        @pl.when(qi == ki)
        def _():
            dk_sc[...] = jnp.zeros_like(dk_sc)
            dv_sc[...] = jnp.zeros_like(dv_sc)

        def scores_step(mask):
            kr_v, lse_tile, delta_tile = kr_ref[...], lse_ref[...], delta_ref[...]
            tn = (((0,), (0,)), ((), ()))
            for hh in range(hb):
                wide, cs = _heads(hh)
                q_v, dout = q_ref[:, wide], do_ref[:, cs]
                sc = _causal_scores(q_v, kn_ref[:, cs], kr_v, mask)
                p = jnp.exp2(sc - _take_lane(lse_tile, g * hb + hh))
                dp = lax.dot_general(dout, v_ref[:, cs], (((1,), (1,)), ((), ())), preferred_element_type=F32)
                ds = (p * (dp - _take_lane(delta_tile, g * hb + hh))).astype(BF16)
                dk_sc[hh] += lax.dot_general(ds, q_v, tn, preferred_element_type=F32)
                dv_sc[hh] += lax.dot_general(p.astype(BF16), dout, tn, preferred_element_type=F32)

        _on_and_off_diagonal(qi, ki, t, scores_step)

        @pl.when(qi == nq - 1)
        def _():
            @pl.when(g == 0)
            def _():
                dkr_ref[...] = pkr_ref[...] if has_prev else jnp.zeros_like(dkr_ref)

            dkr = dkr_ref[...]
            for hh in range(hb):
                cs = _heads(hh)[1]
                dk = dk_sc[hh] * inv_q_scale
                dkn, dv = dk[:, :HEAD], dv_sc[hh] * inv_do_scale
                if has_prev:
                    dkn = dkn + pkn_ref[:, cs].astype(F32)
                    dv = dv + pv_ref[:, cs].astype(F32)
                dkn_ref[:, cs] = dkn.astype(BF16)
                dv_ref[:, cs] = dv.astype(BF16)
                dkr = dkr + dk[:, HEAD:]
            dkr_ref[...] = dkr

    head_q = pl.BlockSpec((t, hb * HEAD), lambda i, a, b, c: (c[i], b[i]))
    head_k = pl.BlockSpec((t, hb * HEAD), lambda i, a, b, c: (a[i], b[i]))
    shared_k = pl.BlockSpec((t, HEAD), lambda i, a, b, c: (a[i], 0))
    in_specs = [
        pl.BlockSpec((t, hb * QK_PAD), lambda i, a, b, c: (c[i], b[i])), head_k, shared_k, head_k, head_q,
        pl.BlockSpec((t, HEAD), lambda i, a, b, c: (c[i], 0)), pl.BlockSpec((t, HEAD), lambda i, a, b, c: (c[i], 0)),
    ]
    ins = [q, kn, kr, v, do, lse, delta]
    if has_prev:
        in_specs += [head_k, shared_k, head_k]
        ins += list(prev)
    grid_spec = pltpu.PrefetchScalarGridSpec(
        num_scalar_prefetch=3, grid=(int(kt.shape[0]),), in_specs=in_specs, out_specs=[head_k, shared_k, head_k],
        scratch_shapes=[pltpu.VMEM((hb, t, QK_PAD), F32), pltpu.VMEM((hb, t, HEAD), F32)],
    )
    return pl.pallas_call(
        body, name=name, grid_spec=grid_spec,
        out_shape=[jax.ShapeDtypeStruct((s, n_heads * HEAD), BF16), jax.ShapeDtypeStruct((s, HEAD), F32),
                   jax.ShapeDtypeStruct((s, n_heads * HEAD), BF16)],
        compiler_params=_params(1),
    )(kt, gt, qt, *ins)


def rope_rows(name, z, tabs, *, tm=ATTN_TILE):
    s = z.shape[0]
    tm = _tile(s, tm, 8)

    def body(z_ref, cs_ref, s1_ref, s2_ref, o_ref):
        o_ref[...] = _rope_tile(z_ref[...], cs_ref[...], s1_ref[...], s2_ref[...]).astype(BF16)

    row = pl.BlockSpec((tm, HEAD), lambda i: (i, 0))
    return pl.pallas_call(body, name=name, grid=(s // tm,), in_specs=[row] * 4, out_specs=row,
                          out_shape=jax.ShapeDtypeStruct((s, HEAD), BF16), compiler_params=_params(1))(z, *tabs)


def rope_tables(s):
    inv = 1.0 / (ROPE_THETA ** (jnp.arange(0, ROPE, 2, dtype=F32) / ROPE))
    ang = jnp.arange(s, dtype=F32)[:, None] * inv[None, :]
    cos, sin, zero = jnp.cos(ang), jnp.sin(ang), jnp.zeros((s, ROPE // 2), F32)
    pad = jnp.zeros((s, HEAD - ROPE), F32)
    cs = jnp.concatenate([cos, cos, pad], axis=1)
    lo = jnp.concatenate([sin, zero, pad], axis=1)
    hi = jnp.concatenate([zero, sin, pad], axis=1)
    return (cs, -lo, hi), (cs, lo, -hi)


def local_step(x, target, w, norms):
    s, d = x.shape
    depth = norms['ffn_norm1'].shape[0]
    n_a = w['a_wqkv'].shape[0]
    gw = {k: lax.empty(v.shape, BF16) for k, v in w.items()}
    n_heads = d // HEAD
    fwd_tabs, bwd_tabs = rope_tables(s)
    mla_scale = (HEAD + ROPE) ** -0.5
    q_scale = mla_scale * math.log2(math.e)
    gain = lambda name, i: norms[name][i:i + 1]
    saved = []
    kv = None

    def ffn_fwd(tag, x, g_norm, which, layer):
        wg, wu, wd = ((w[f'{which}_{nm}'], layer) for nm in ('wg', 'wu', 'wd'))
        h = norm_fwd(f'{tag}_norm', x, g_norm)
        g, u, a = ffn_up(f'{tag}_up', h, wg, wu)
        y = mm_nn(f'{tag}_down', a, wd, out_dtype=F32, res=x, alpha=0.5)
        return y, (x, h, g, u, a)

    def ffn_bwd(tag, dx, dxb, g_norm, which, layer, sv):
        wg, wu, wd = ((w[f'{which}_{nm}'], layer) for nm in ('wg', 'wu', 'wd'))
        x, h, g, u, a = sv
        dg, du = mm_nt(f'{tag}_bwd_da', [dxb], [wd], out_dtype=BF16, alpha=0.5, swiglu=(g, u))
        gw[f'{which}_wd'] = mm_tn(f'{tag}_bwd_dwd', a, dxb, (gw[f'{which}_wd'], layer), alpha=0.5)
        gw[f'{which}_wg'] = mm_tn(f'{tag}_bwd_dwg', h, dg, (gw[f'{which}_wg'], layer))
        gw[f'{which}_wu'] = mm_tn(f'{tag}_bwd_dwu', h, du, (gw[f'{which}_wu'], layer))
        dh = mm_nt(f'{tag}_bwd_dh', [dg, du], [wg, wu], out_dtype=F32)
        return norm_bwd(f'{tag}_bwd_norm', x, g_norm, dh, dx)

    for layer in range(depth):
        if layer == n_a:
            hk = norm_fwd('kv_norm', x, norms['kv_norm'])
            ckv_raw = mm_nn('kv_down', hk, (w['b_wdkv'], 0), out_dtype=F32)
            ckv = norm_fwd('kv_cnorm', ckv_raw, norms['b_ckv_norm'])
            k_nope = mm_nn('kv_uk', ckv, (w['b_wuk'], 0), out_dtype=BF16)
            v_shared = mm_nn('kv_uv', ckv, (w['b_wuv'], 0), out_dtype=BF16)
            k_rope = mm_nn('kv_kr', hk, (w['b_wkr'], 0), out_dtype=BF16, rope=('k',) + fwd_tabs, tn=HEAD)
            kv = (x, hk, ckv_raw, ckv, k_nope, k_rope, v_shared)
        x, sv1 = ffn_fwd(f'l{layer}_ffn1', x, gain('ffn_norm1', layer), 'ffn1', layer)
        h = norm_fwd(f'l{layer}_mix_norm', x, gain('mix_norm', layer))
        if layer < n_a:
            qkv = mm_nn(f'l{layer}_qkv', h, (w['a_wqkv'], layer), out_dtype=BF16)
            state = None
            for bi, (window, dilation) in enumerate(DILATED_BRANCHES):
                last = bi == len(DILATED_BRANCHES) - 1
                state = dilated_fwd(f'l{layer}_dil{dilation}', qkv, state, dilation, n_heads, last)
            o, lse = state
            x_new = mm_nn(f'l{layer}_wo', o, (w['a_wo'], layer), out_dtype=F32, res=x)
            svm = (x, h, qkv, o, lse)
        else:
            jb = layer - n_a
            cq_raw = mm_nn(f'l{layer}_dq', h, (w['b_wdq'], jb), out_dtype=F32)
            cq = norm_fwd(f'l{layer}_cq_norm', cq_raw, gain('b_cq_norm', jb))
            q = mm_nn(f'l{layer}_uq', cq, (w['b_wuq'], jb), out_dtype=BF16, alpha=q_scale, rope=('q',) + fwd_tabs)
            o, lse = mla_fwd(f'l{layer}_mla', q, kv[4], kv[5], kv[6], n_heads)
            x_new = mm_nn(f'l{layer}_wo', o, (w['b_wo'], jb), out_dtype=F32, res=x)
            svm = (x, h, cq_raw, cq, q, o, lse)
        x = x_new
        x, sv2 = ffn_fwd(f'l{layer}_ffn2', x, gain('ffn_norm2', layer), 'ffn2', layer)
        saved.append((sv1, svm, sv2))

    loss, dx, dxb, d_final = loss_head('loss_head', x, norms['final_norm'], target)

    gn = {k: [None] * v.shape[0] for k, v in norms.items()}
    gn['final_norm'] = [d_final]
    dkv = None
    for layer in reversed(range(depth)):
        sv1, svm, sv2 = saved[layer]
        dx, dxb, gn['ffn_norm2'][layer] = ffn_bwd(f'l{layer}_ffn2', dx, dxb, gain('ffn_norm2', layer), 'ffn2', layer, sv2)
        if layer < n_a:
            xm, h, qkv, o, lse = svm
            do = mm_nt(f'l{layer}_bwd_do', [dxb], [(w['a_wo'], layer)], out_dtype=BF16)
            gw['a_wo'] = mm_tn(f'l{layer}_bwd_dwo', o, dxb, (gw['a_wo'], layer))
            parts = [dilated_bwd(f'l{layer}_bwd_dil{dilation}', qkv, o, do, lse, dilation, n_heads)
                     for _, dilation in DILATED_BRANCHES]
            dqkv = sum_branches(f'l{layer}_bwd_sum', parts)
            gw['a_wqkv'] = mm_tn(f'l{layer}_bwd_dwqkv', h, dqkv, (gw['a_wqkv'], layer))
            dh = mm_nt(f'l{layer}_bwd_dh', [dqkv], [(w['a_wqkv'], layer)], out_dtype=F32)
        else:
            jb = layer - n_a
            xm, h, cq_raw, cq, q, o, lse = svm
            do = mm_nt(f'l{layer}_bwd_do', [dxb], [(w['b_wo'], jb)], out_dtype=BF16, alpha=mla_scale)
            gw['b_wo'] = mm_tn(f'l{layer}_bwd_dwo', o, dxb, (gw['b_wo'], jb))
            dq, delta = mla_bwd_dq(f'l{layer}_bwd_mla_dq', q, kv[4], kv[5], kv[6], o, do, lse, bwd_tabs, n_heads)
            dkv = mla_bwd_dkv(f'l{layer}_bwd_mla_dkv', q, kv[4], kv[5], kv[6], do, lse, delta, dkv, n_heads,
                              1.0 / q_scale, 1.0 / mla_scale)
            gw['b_wuq'] = mm_tn(f'l{layer}_bwd_dwuq', cq, dq, (gw['b_wuq'], jb))
            dcq = mm_nt(f'l{layer}_bwd_dcq', [dq], [(w['b_wuq'], jb)], out_dtype=F32)
            _, dcq_raw, gn['b_cq_norm'][jb] = norm_bwd(f'l{layer}_bwd_cq_norm', cq_raw, gain('b_cq_norm', jb), dcq)
            gw['b_wdq'] = mm_tn(f'l{layer}_bwd_dwdq', h, dcq_raw, (gw['b_wdq'], jb))
            dh = mm_nt(f'l{layer}_bwd_dh', [dcq_raw], [(w['b_wdq'], jb)], out_dtype=F32)
        dx, dxb, gn['mix_norm'][layer] = norm_bwd(f'l{layer}_bwd_mix_norm', xm, gain('mix_norm', layer), dh, dx)
        dx, dxb, gn['ffn_norm1'][layer] = ffn_bwd(f'l{layer}_ffn1', dx, dxb, gain('ffn_norm1', layer), 'ffn1', layer, sv1)
        if layer == n_a:
            xk, hk, ckv_raw, ckv, k_nope, k_rope, v_shared = kv
            dkn, dkr_rot, dv = dkv
            dkr = rope_rows('kv_bwd_rope', dkr_rot, bwd_tabs)
            gw['b_wuk'] = mm_tn('kv_bwd_dwuk', ckv, dkn, (gw['b_wuk'], 0))
            gw['b_wuv'] = mm_tn('kv_bwd_dwuv', ckv, dv, (gw['b_wuv'], 0))
            dckv = mm_nt('kv_bwd_dckv', [dkn, dv], [(w['b_wuk'], 0), (w['b_wuv'], 0)], out_dtype=F32)
            _, dckv_raw, g_ckv = norm_bwd('kv_bwd_cnorm', ckv_raw, norms['b_ckv_norm'], dckv)
            gw['b_wdkv'] = mm_tn('kv_bwd_dwdkv', hk, dckv_raw, (gw['b_wdkv'], 0))
            gw['b_wkr'] = mm_tn('kv_bwd_dwkr', hk, dkr, (gw['b_wkr'], 0))
            dhk = mm_nt('kv_bwd_dhk_c', [dckv_raw], [(w['b_wdkv'], 0)], out_dtype=F32)
            dhk = mm_nt('kv_bwd_dhk_r', [dkr], [(w['b_wkr'], 0)], out_dtype=F32, res=dhk)
            dx, dxb, g_kv = norm_bwd('kv_bwd_norm', xk, norms['kv_norm'], dhk, dx)
            gn['kv_norm'], gn['b_ckv_norm'] = [g_kv], [g_ckv]
    grads_n = {k: jnp.concatenate(v, axis=0) for k, v in gn.items()}
    return loss, dx, gw, grads_n


def _place():
    x, y, c = lax.axis_index("x"), lax.axis_index("y"), lax.axis_index("c")
    chips = [(1 - x, y), (x, 1 - y), (1 - x, 1 - y)]
    return x, y, c, 2 * x + y, chips, [2 * cx + cy for cx, cy in chips]


def _remote(src, dst, send_sem, recv_sem, to):
    return pltpu.make_async_remote_copy(src_ref=src, dst_ref=dst, send_sem=send_sem, recv_sem=recv_sem,
                                        device_id=to, device_id_type=MESH)


def cast_to_slab(name, shard, place, *, tr=ROW_TILE):
    l, r, c = shard.shape
    tr = _tile(r, tr, 16)

    def body(place_ref, x_ref, o_ref):
        o_ref[...] = x_ref[...].astype(BF16)

    grid_spec = pltpu.PrefetchScalarGridSpec(
        num_scalar_prefetch=1, grid=(l, r // tr),
        in_specs=[pl.BlockSpec((None, tr, c), lambda i, j, p: (i, j, 0))],
        out_specs=pl.BlockSpec((None, None, tr, c), lambda i, j, p: (i, p[0], j, 0)),
    )
    return pl.pallas_call(body, name=name, grid_spec=grid_spec,
                          out_shape=jax.ShapeDtypeStruct((l, N_CHIPS, r, c), BF16), compiler_params=_params(2))(place, shard)


def all_gather_weights(name, slabs):
    n = len(slabs)

    def body(*refs):
        outs = refs[n:2 * n]
        send_sems, recv_sems = refs[2 * n:]
        x, y, c, me, chips, chip_ids = _place()
        sibling = (x, y, 1 - c)

        def copy(t, k, chip, hc, to):
            rh = outs[t].shape[2] // 2
            blk = outs[t].at[:, chip, pl.ds(hc * rh, rh), :]
            return _remote(blk, blk, send_sems.at[6 * t + k], recv_sems.at[6 * t + k], to)

        first = [copy(t, k, me, c, (*chips[k], c)) for t in range(n) for k in range(3)]
        for cp in first:
            cp.start()
        passed = []
        for t in range(n):
            for k in range(3):
                copy(t, k, chip_ids[k], c, sibling).wait_recv()
                cp = copy(t, 3 + k, chip_ids[k], c, sibling)
                cp.start()
                passed.append(cp)
        for t in range(n):
            for k in range(3):
                copy(t, 3 + k, chip_ids[k], 1 - c, sibling).wait_recv()
        for cp in first + passed:
            cp.wait_send()

    any_spec = pl.BlockSpec(memory_space=pl.ANY)
    return pl.pallas_call(
        body, name=name, in_specs=[any_spec] * n, out_specs=[any_spec] * n,
        out_shape=[jax.ShapeDtypeStruct(a.shape, a.dtype) for a in slabs],
        scratch_shapes=[pltpu.SemaphoreType.DMA((6 * n,)), pltpu.SemaphoreType.DMA((6 * n,))],
        input_output_aliases={t: t for t in range(n)},
        compiler_params=pltpu.CompilerParams(has_side_effects=True),
    )(*slabs)


def exchange_halves(name, parts):
    n = len(parts)

    def body(*refs):
        ins, outs = refs[:n], refs[n:2 * n]
        send_sems, recv_sems = refs[2 * n:]
        x, y, c, me, chips, chip_ids = _place()
        copies = []
        for t in range(n):
            rh = ins[t].shape[2] // 2
            cp = _remote(ins[t].at[:, :, pl.ds((1 - c) * rh, rh), :], outs[t], send_sems.at[t], recv_sems.at[t],
                         (x, y, 1 - c))
            cp.start()
            copies.append(cp)
        for cp in copies:
            cp.wait()

    any_spec = pl.BlockSpec(memory_space=pl.ANY)
    return pl.pallas_call(
        body, name=name, in_specs=[any_spec] * n, out_specs=[any_spec] * n,
        out_shape=[jax.ShapeDtypeStruct((a.shape[0], a.shape[1], a.shape[2] // 2, a.shape[3]), a.dtype) for a in parts],
        scratch_shapes=[pltpu.SemaphoreType.DMA((n,)), pltpu.SemaphoreType.DMA((n,))],
        compiler_params=pltpu.CompilerParams(has_side_effects=True),
    )(*parts)


def add_halves(name, part, recv, place, *, tr=ROW_TILE):
    l, j, rh, c = recv.shape
    tr = _tile(rh, tr, 16)
    nr = rh // tr

    def body(place_ref, p_ref, r_ref, o_ref):
        o_ref[...] = (p_ref[...].astype(F32) + r_ref[...].astype(F32)).astype(BF16)

    blk = (None, None, tr, c)
    grid_spec = pltpu.PrefetchScalarGridSpec(
        num_scalar_prefetch=1, grid=(l, j, nr),
        in_specs=[pl.BlockSpec(blk, lambda a, b, i, p: (a, b, p[1] * nr + i, 0)),
                  pl.BlockSpec(blk, lambda a, b, i, p: (a, b, i, 0))],
        out_specs=pl.BlockSpec(blk, lambda a, b, i, p: (a, b, i, 0)),
    )
    return pl.pallas_call(body, name=name, grid_spec=grid_spec, out_shape=jax.ShapeDtypeStruct(recv.shape, BF16),
                          compiler_params=_params(3))(place, part, recv)


def scatter_to_chips(name, sums):
    n = len(sums)

    def body(*refs):
        ins, outs = refs[:n], refs[n:2 * n]
        send_sems, recv_sems = refs[2 * n:]
        x, y, c, me, chips, chip_ids = _place()
        copies = []
        for t in range(n):
            for k in range(3):
                cp = _remote(ins[t].at[:, chip_ids[k]], outs[t].at[k], send_sems.at[3 * t + k], recv_sems.at[3 * t + k],
                             (*chips[k], c))
                cp.start()
                copies.append(cp)
        for cp in copies:
            cp.wait()

    any_spec = pl.BlockSpec(memory_space=pl.ANY)
    return pl.pallas_call(
        body, name=name, in_specs=[any_spec] * n, out_specs=[any_spec] * n,
        out_shape=[jax.ShapeDtypeStruct((3, a.shape[0], a.shape[2], a.shape[3]), a.dtype) for a in sums],
        scratch_shapes=[pltpu.SemaphoreType.DMA((3 * n,)), pltpu.SemaphoreType.DMA((3 * n,))],
        compiler_params=pltpu.CompilerParams(has_side_effects=True),
    )(*sums)


def reduce_own(name, part, recv1, recv2, place, *, tr=ROW_TILE):
    l, j, rh, c = recv1.shape
    tr = _tile(rh, tr, 16)
    nr = rh // tr

    def body(place_ref, p_ref, r1_ref, a_ref, b_ref, c_ref, o_ref):
        acc = p_ref[...].astype(F32) + r1_ref[...].astype(F32)
        acc = acc + a_ref[...].astype(F32)
        acc = acc + b_ref[...].astype(F32)
        o_ref[...] = acc + c_ref[...].astype(F32)

    blk = (None, None, tr, c)
    grid_spec = pltpu.PrefetchScalarGridSpec(
        num_scalar_prefetch=1, grid=(l, nr),
        in_specs=[pl.BlockSpec(blk, lambda a, i, p: (a, p[0], p[1] * nr + i, 0)),
                  pl.BlockSpec(blk, lambda a, i, p: (a, p[0], i, 0)),
                  pl.BlockSpec(blk, lambda a, i, p: (0, a, i, 0)),
                  pl.BlockSpec(blk, lambda a, i, p: (1, a, i, 0)),
                  pl.BlockSpec(blk, lambda a, i, p: (2, a, i, 0))],
        out_specs=pl.BlockSpec((None, tr, c), lambda a, i, p: (a, p[1] * nr + i, 0)),
    )
    return pl.pallas_call(body, name=name, grid_spec=grid_spec, out_shape=jax.ShapeDtypeStruct((l, 2 * rh, c), F32),
                          compiler_params=_params(2))(place, part, recv1, recv2, recv2, recv2)


def share_halves(name, grads):
    n = len(grads)

    def body(*refs):
        outs = refs[n:2 * n]
        send_sems, recv_sems = refs[2 * n:]
        x, y, c, me, chips, chip_ids = _place()
        copies = []
        for t in range(n):
            rh = outs[t].shape[1] // 2
            mine = outs[t].at[:, pl.ds(c * rh, rh), :]
            cp = _remote(mine, mine, send_sems.at[t], recv_sems.at[t], (x, y, 1 - c))
            cp.start()
            copies.append(cp)
        for t, cp in enumerate(copies):
            rh = outs[t].shape[1] // 2
            theirs = outs[t].at[:, pl.ds((1 - c) * rh, rh), :]
            cp.wait_send()
            _remote(theirs, theirs, send_sems.at[t], recv_sems.at[t], (x, y, 1 - c)).wait_recv()

    any_spec = pl.BlockSpec(memory_space=pl.ANY)
    return pl.pallas_call(
        body, name=name, in_specs=[any_spec] * n, out_specs=[any_spec] * n,
        out_shape=[jax.ShapeDtypeStruct(a.shape, a.dtype) for a in grads],
        scratch_shapes=[pltpu.SemaphoreType.DMA((n,)), pltpu.SemaphoreType.DMA((n,))],
        input_output_aliases={t: t for t in range(n)},
        compiler_params=pltpu.CompilerParams(has_side_effects=True),
    )(*grads)


def all_reduce_small(name, packed):
    r = packed.shape[0]

    def body(x_ref, o_ref, buf, send_sems, recv_sems):
        x, y, c = lax.axis_index("x"), lax.axis_index("y"), lax.axis_index("c")
        me = 4 * x + 2 * y + c
        buf[me] = x_ref[...]
        copies = []
        for k in range(1, 8):
            to = (x ^ (k >> 2), y ^ ((k >> 1) & 1), c ^ (k & 1))
            cp = _remote(x_ref, buf.at[me], send_sems.at[k - 1], recv_sems.at[k - 1], to)
            cp.start()
            copies.append(cp)
        for k in range(1, 8):
            peer = me ^ k
            _remote(x_ref, buf.at[peer], send_sems.at[k - 1], recv_sems.at[k - 1], (x, y, c)).wait_recv()
        for cp in copies:
            cp.wait_send()
        acc = buf[0]
        for dev in range(1, 8):
            acc = acc + buf[dev]
        o_ref[...] = acc

    vmem = pl.BlockSpec(memory_space=pltpu.VMEM)
    return pl.pallas_call(
        body, name=name, in_specs=[vmem], out_specs=vmem, out_shape=jax.ShapeDtypeStruct(packed.shape, F32),
        scratch_shapes=[pltpu.VMEM((8, r, HEAD), F32), pltpu.SemaphoreType.DMA((7,)), pltpu.SemaphoreType.DMA((7,))],
    )(packed)


def adamw(name, w, g, m, v, *, tr=ROW_TILE):
    l, r, c = w.shape
    tr = _tile(r, tr, 8)

    def body(w_ref, g_ref, m_ref, v_ref, d_ref, nm_ref, nv_ref):
        gv = g_ref[...]
        nm = ADAM_B1 * m_ref[...] + (1.0 - ADAM_B1) * gv
        nv = ADAM_B2 * v_ref[...] + (1.0 - ADAM_B2) * (gv * gv)
        m_hat = nm / (1.0 - ADAM_B1 ** ADAM_STEP)
        v_hat = nv / (1.0 - ADAM_B2 ** ADAM_STEP)
        d_ref[...] = -ADAM_LR * (m_hat / (jnp.sqrt(v_hat) + ADAM_EPS) + ADAM_WD * w_ref[...])
        nm_ref[...] = nm
        nv_ref[...] = nv

    blk = pl.BlockSpec((None, tr, c), lambda i, j: (i, j, 0))
    shape = jax.ShapeDtypeStruct(w.shape, F32)
    return pl.pallas_call(body, name=name, grid=(l, r // tr), in_specs=[blk] * 4, out_specs=[blk] * 3,
                          out_shape=[shape] * 3, compiler_params=_params(2))(w, g, m, v)


SHARDED = ('ffn1_wg', 'ffn1_wu', 'ffn1_wd', 'ffn2_wg', 'ffn2_wu', 'ffn2_wd', 'a_wqkv', 'a_wo', 'b_wdkv', 'b_wkr', 'b_wuk',
           'b_wuv', 'b_wdq', 'b_wuq', 'b_wo')
COLUMN_SHARDED = ('ffn1_wg', 'ffn1_wu', 'ffn2_wg', 'ffn2_wu', 'a_wqkv')
GAINS = ('ffn_norm1', 'mix_norm', 'ffn_norm2', 'kv_norm', 'b_ckv_norm', 'b_cq_norm', 'final_norm')
WEIGHTS = ('ffn_norm1', 'ffn1_wg', 'ffn1_wu', 'ffn1_wd', 'mix_norm', 'ffn_norm2', 'ffn2_wg', 'ffn2_wu', 'ffn2_wd', 'a_wqkv',
           'a_wo', 'kv_norm', 'b_wdkv', 'b_ckv_norm', 'b_wkr', 'b_wuk', 'b_wuv', 'b_wdq', 'b_cq_norm', 'b_wuq', 'b_wo',
           'final_norm')


def _to_comm_shape(name, a):
    if name == 'b_wkr':
        return jnp.pad(a, ((0, 0), (0, HEAD - ROPE)))[None]
    if name == 'b_wuq':
        l, r, h, _ = a.shape
        return jnp.pad(a, ((0, 0), (0, 0), (0, 0), (0, QK_PAD - HEAD - ROPE))).reshape(l, r, h * QK_PAD)
    if name in ('b_wuk', 'b_wuv'):
        return a.reshape(1, a.shape[0], -1)
    return a[None] if a.ndim == 2 else a


def _from_comm_shape(name, g, like):
    if name == 'b_wkr':
        return g[0, :, :ROPE]
    if name == 'b_wuq':
        l, r, h, e = like.shape
        return g.reshape(l, r, h, QK_PAD)[..., :e]
    return g.reshape(like.shape)


def _as3d(a):
    if a.ndim == 1:
        return a.reshape(1, 1, -1)
    if a.ndim == 2:
        return a[None]
    return a.reshape(a.shape[0], a.shape[1], -1)


def kernel(x, ffn_norm1, ffn1_wg, ffn1_wu, ffn1_wd, mix_norm, ffn_norm2, ffn2_wg, ffn2_wu, ffn2_wd, a_wqkv, a_wo, kv_norm, b_wdkv, b_ckv_norm, b_wkr, b_wuk, b_wuv, b_wdq, b_cq_norm, b_wuq, b_wo, final_norm, loss_target, m_ffn_norm1, m_ffn1_wg, m_ffn1_wu, m_ffn1_wd, m_mix_norm, m_ffn_norm2, m_ffn2_wg, m_ffn2_wu, m_ffn2_wd, m_a_wqkv, m_a_wo, m_kv_norm, m_b_wdkv, m_b_ckv_norm, m_b_wkr, m_b_wuk, m_b_wuv, m_b_wdq, m_b_cq_norm, m_b_wuq, m_b_wo, m_final_norm, v_ffn_norm1, v_ffn1_wg, v_ffn1_wu, v_ffn1_wd, v_mix_norm, v_ffn_norm2, v_ffn2_wg, v_ffn2_wu, v_ffn2_wd, v_a_wqkv, v_a_wo, v_kv_norm, v_b_wdkv, v_b_ckv_norm, v_b_wkr, v_b_wuk, v_b_wuv, v_b_wdq, v_b_cq_norm, v_b_wuq, v_b_wo, v_final_norm):
    args = dict(locals())
    weights = {k: args[k] for k in WEIGHTS}
    place = jnp.stack([2 * lax.axis_index("x") + lax.axis_index("y"), lax.axis_index("c")]).astype(jnp.int32)

    slabs = [cast_to_slab(f'cast_{k}', _to_comm_shape(k, weights[k]), place) for k in SHARDED]
    gathered = all_gather_weights('gather_weights', slabs)
    w = {}
    for k, g in zip(SHARDED, gathered):
        l, j, r, c = g.shape
        w[k] = g if k in COLUMN_SHARDED else g.reshape(l, j * r, c)
    norms = {k: weights[k] if weights[k].ndim == 2 else weights[k][None] for k in GAINS}

    loss, grad_x, gw, gn = local_step(x[0], loss_target[0], w, norms)
    loss = lax.psum(loss[0, 0], ("x", "y", "c"))

    parts = [gw[k].reshape(g.shape) for k, g in zip(SHARDED, gathered)]
    recv1 = exchange_halves('grads_to_sibling', parts)
    sums = [add_halves(f'add_{k}', p, r1, place) for k, p, r1 in zip(SHARDED, parts, recv1)]
    recv2 = scatter_to_chips('grads_to_chips', sums)
    halves = [reduce_own(f'reduce_{k}', p, r1, r2, place) for k, p, r1, r2 in zip(SHARDED, parts, recv1, recv2)]
    reduced = share_halves('grads_share', halves)
    grads = {k: _from_comm_shape(k, g, weights[k]) for k, g in zip(SHARDED, reduced)}

    flat = jnp.concatenate([gn[k].reshape(-1) for k in GAINS])
    rows = -(-flat.shape[0] // (8 * HEAD)) * 8
    packed = jnp.pad(flat, (0, rows * HEAD - flat.shape[0])).reshape(rows, HEAD)
    total = all_reduce_small('gain_grads', packed).reshape(-1)
    off = 0
    for k in GAINS:
        grads[k] = total[off:off + weights[k].size].reshape(weights[k].shape)
        off += weights[k].size

    deltas, new_m, new_v = {}, {}, {}
    for k in WEIGHTS:
        shape = weights[k].shape
        d, nm, nv = adamw(f'adamw_{k}', _as3d(weights[k]), _as3d(grads[k]), _as3d(args['m_' + k]), _as3d(args['v_' + k]))
        deltas[k], new_m[k], new_v[k] = d.reshape(shape), nm.reshape(shape), nv.reshape(shape)
    return (loss, grad_x[None], *[grads[k] for k in WEIGHTS], *[deltas[k] for k in WEIGHTS],
            *[new_m[k] for k in WEIGHTS], *[new_v[k] for k in WEIGHTS])
```

```python
import math

import numpy as np
import jax
import jax.numpy as jnp
from jax import lax
from jax.experimental import pallas as pl
from jax.experimental.pallas import tpu as pltpu

F32 = jnp.float32
BF16 = jnp.bfloat16
NORM_EPS = 1e-6
MASK_VALUE = -1e30
HEAD = 128
ROPE = 64
QK_PAD = 256
DILATED_BRANCHES = ((128, 1), (512, 4), (2048, 16))
ROPE_THETA = 10000.0
ADAM_LR, ADAM_B1, ADAM_B2, ADAM_EPS, ADAM_WD, ADAM_STEP = 0.001, 0.9, 0.999, 1e-08, 0.01, 10
VMEM_LIMIT_BYTES = 56 * 1024 * 1024
MM_ROWS, MM_COLS, MM_WIDE, MM_SHALLOW, MM_DEPTH = 512, 512, 1024, 2048, 8192
MM_TALL = 1024
MM_SUB = 256
ROW_TILE = 256
ATTN_TILE = 512
WGRAD_DEPTH = 4096
DILATED_HEADS_PER_STEP, MLA_FWD_HEADS_PER_STEP, MLA_BWD_HEADS_PER_STEP = 16, 4, 4
N_CHIPS = 4
MESH = pl.DeviceIdType.MESH


def _params(n_axes, vmem=VMEM_LIMIT_BYTES):
    return pltpu.CompilerParams(dimension_semantics=("arbitrary",) * n_axes, vmem_limit_bytes=vmem)


def _tile(n, pref, mult=128):
    if n <= pref:
        return n
    t = (pref // mult) * mult
    while t >= mult:
        if n % t == 0:
            return t
        t -= mult
    return n


def _mm_call(name, grid, ins, in_specs, out_shape, out_specs, n_a, n_b, terms, dims, acc_shapes, epilogue, into=None,
             nt_sub=None):
    n_in, n_out, nk = len(ins), len(out_shape), grid[2]
    aliases = {}
    if into is not None:
        ins = list(ins) + [into]
        in_specs = list(in_specs) + [pl.BlockSpec(memory_space=pl.ANY)]
        aliases = {n_in: 0}

    def body(*refs):
        in_refs, refs = refs[:n_in], refs[n_in + len(aliases):]
        out_refs, accs = refs[:n_out], refs[n_out:]
        a_refs, b_refs, extra = in_refs[:n_a], in_refs[n_a:n_a + n_b], in_refs[n_a + n_b:]
        k = pl.program_id(2)
        if nt_sub is not None:
            for c0 in range(0, acc_shapes[0][1], nt_sub):
                cols = slice(c0, c0 + nt_sub)
                part = None
                for t, ia, ib in terms:
                    one = lax.dot_general(a_refs[ia][...], b_refs[ib][cols, :], (dims, ((), ())), preferred_element_type=F32)
                    part = one if part is None else part + one
                epilogue([part], extra, out_refs, cols)
            return
        sums = [None] * len(acc_shapes)
        for t, ia, ib in terms:
            a = a_refs[ia][...].astype(BF16)
            b = b_refs[ib][...].astype(BF16)
            part = lax.dot_general(a, b, (dims, ((), ())), preferred_element_type=F32)
            sums[t] = part if sums[t] is None else sums[t] + part
        if nk == 1:
            epilogue(sums, extra, out_refs)
            return

        @pl.when(k == 0)
        def _():
            for acc, part in zip(accs, sums):
                acc[...] = part

        @pl.when(k > 0)
        def _():
            for acc, part in zip(accs, sums):
                acc[...] += part

        @pl.when(k == nk - 1)
        def _():
            epilogue([acc[...] for acc in accs], extra, out_refs)

    return pl.pallas_call(
        body, name=name, grid=grid, in_specs=in_specs, out_specs=out_specs, out_shape=out_shape,
        scratch_shapes=[pltpu.VMEM(s, F32) for s in acc_shapes] if nk > 1 else [], input_output_aliases=aliases,
        compiler_params=_params(3),
    )(*ins)


def _b_spec_nn(w, tk, tn):
    b, layer = w
    if b.ndim == 3:
        return pl.BlockSpec((None, tk, tn), lambda i, j, k: (layer, k, j))
    per = b.shape[3] // tn
    return pl.BlockSpec((None, None, tk, tn), lambda i, j, k: (layer, j // per, k, j % per))


def _rope_tile(z, cs, s1, s2):
    return z * cs + pltpu.roll(z, 96, 1) * s1 + pltpu.roll(z, 32, 1) * s2


def mm_nn(name, a, w, *, out_dtype, res=None, alpha=1.0, rope=None, tm=MM_ROWS, tn=None, tk=MM_DEPTH):
    m, kk = a.shape
    b = w[0]
    ns = b.shape[-1]
    n = b.shape[1] * ns if b.ndim == 4 else ns
    tm, tk = _tile(m, MM_TALL if kk > MM_SHALLOW else tm, 8), _tile(kk, tk)
    tn = _tile(ns, tn or (MM_WIDE if kk <= MM_SHALLOW else MM_COLS))
    if rope is not None and rope[0] == 'q':
        assert tn % QK_PAD == 0
    grid = (m // tm, n // tn, kk // tk)
    ins = [a, b]
    in_specs = [pl.BlockSpec((tm, tk), lambda i, j, k: (i, k)), _b_spec_nn(w, tk, tn)]
    if res is not None:
        ins.append(res)
        in_specs.append(pl.BlockSpec((tm, tn), lambda i, j, k: (i, j)))
    if rope is not None:
        ins += list(rope[1:])
        in_specs += [pl.BlockSpec((tm, HEAD), lambda i, j, k: (i, 0))] * 3

    def epilogue(accs, extra, outs):
        y = accs[0]
        extra = list(extra)
        if alpha != 1.0:
            y = alpha * y
        if res is not None:
            y = extra.pop(0)[...] + y
        if rope is not None:
            cs, s1, s2 = (e[...] for e in extra)
            step = QK_PAD if rope[0] == 'q' else HEAD
            for c0 in range(0, tn, step):
                if rope[0] == 'q':
                    outs[0][:, c0:c0 + HEAD] = y[:, c0:c0 + HEAD].astype(out_dtype)
                    c0 += HEAD
                outs[0][:, c0:c0 + HEAD] = _rope_tile(y[:, c0:c0 + HEAD], cs, s1, s2).astype(out_dtype)
        else:
            outs[0][...] = y.astype(out_dtype)

    return _mm_call(name, grid, ins, in_specs, [jax.ShapeDtypeStruct((m, n), out_dtype)],
                    [pl.BlockSpec((tm, tn), lambda i, j, k: (i, j))], 1, 1, [(0, 0, 0)], ((1,), (0,)),
                    [(tm, tn)], epilogue)[0]


def ffn_up(name, h, wg, wu, *, tm=MM_ROWS, tk=MM_DEPTH):
    m, kk = h.shape
    fs = wg[0].shape[3]
    n = wg[0].shape[1] * fs
    tm, tk, tn = _tile(m, tm, 8), _tile(kk, tk), fs
    grid = (m // tm, n // tn, kk // tk)

    def epilogue(accs, extra, outs):
        g, u = accs
        outs[0][...] = g.astype(BF16)
        outs[1][...] = u.astype(BF16)
        outs[2][...] = (g * jax.nn.sigmoid(g) * u).astype(BF16)

    o_spec = pl.BlockSpec((tm, tn), lambda i, j, k: (i, j))
    return _mm_call(name, grid, [h, wg[0], wu[0]],
                    [pl.BlockSpec((tm, tk), lambda i, j, k: (i, k)), _b_spec_nn(wg, tk, tn), _b_spec_nn(wu, tk, tn)],
                    [jax.ShapeDtypeStruct((m, n), BF16)] * 3, [o_spec] * 3, 1, 2, [(0, 0, 0), (1, 0, 1)],
                    ((1,), (0,)), [(tm, tn)] * 2, epilogue)


def mm_nt(name, a_list, w_list, *, out_dtype, alpha=1.0, swiglu=None, res=None, tm=MM_ROWS, tn=None, tk=MM_DEPTH):
    m, n = a_list[0].shape
    b0 = w_list[0][0]
    slab = b0.ndim == 4
    kk = b0.shape[-2]
    ns = b0.shape[-1]
    tk = ns if slab else _tile(ns, tk)
    wide = slab or tk <= MM_SHALLOW
    tm, tn = _tile(m, MM_TALL if swiglu is not None else tm, 8), _tile(kk, tn or (MM_WIDE if wide else MM_COLS))
    grid = (m // tm, kk // tn, n // tk)

    def b_spec(w):
        layer = w[1]
        if slab:
            per = ns // tk
            return pl.BlockSpec((None, None, tn, tk), lambda i, j, k: (layer, k // per, j, k % per))
        return pl.BlockSpec((None, tn, tk), lambda i, j, k: (layer, j, k))

    p = len(a_list)
    ins = list(a_list) + [w[0] for w in w_list]
    in_specs = [pl.BlockSpec((tm, tk), lambda i, j, k: (i, k))] * p + [b_spec(w) for w in w_list]
    o_spec = pl.BlockSpec((tm, tn), lambda i, j, k: (i, j))
    if swiglu is not None:
        ins += list(swiglu)
        in_specs += [o_spec, o_spec]
        out_shape = [jax.ShapeDtypeStruct((m, kk), BF16)] * 2

        def epilogue(accs, extra, outs, cols=slice(None)):
            da = alpha * accs[0]
            g = extra[0][:, cols].astype(F32)
            u = extra[1][:, cols].astype(F32)
            sg = jax.nn.sigmoid(g)
            silu = g * sg
            outs[0][:, cols] = (da * u * (sg + silu * (1.0 - sg))).astype(BF16)
            outs[1][:, cols] = (da * silu).astype(BF16)
    else:
        out_shape = [jax.ShapeDtypeStruct((m, kk), out_dtype)]
        if res is not None:
            ins.append(res)
            in_specs.append(o_spec)

        def epilogue(accs, extra, outs):
            y = alpha * accs[0]
            if res is not None:
                y = y + extra[0][...]
            outs[0][...] = y.astype(out_dtype)

    nt_sub = MM_SUB if swiglu is not None and grid[2] == 1 and tn % MM_SUB == 0 and not slab else None
    outs = _mm_call(name, grid, ins, in_specs, out_shape, [o_spec] * len(out_shape), p, p,
                    [(0, q, q) for q in range(p)], ((1,), (1,)), [(tm, tn)], epilogue, nt_sub=nt_sub)
    return outs if swiglu is not None else outs[0]


def mm_tn(name, a, b, into, *, alpha=1.0, tm=MM_ROWS, tn=MM_WIDE, tk=WGRAD_DEPTH):
    buf, layer = into
    m, kk = a.shape
    n = b.shape[1]
    ns = buf.shape[-1]
    tm, tk = _tile(kk, tm), _tile(m, tk, 8)
    tn = ns if ns % 512 else _tile(ns, tn)
    grid = (kk // tm, n // tn, m // tk)
    if buf.ndim == 4:
        per = ns // tn
        o_spec = pl.BlockSpec((None, None, tm, tn), lambda i, j, k: (layer, j // per, i, j % per))
    else:
        o_spec = pl.BlockSpec((None, tm, tn), lambda i, j, k: (layer, i, j))

    def epilogue(accs, extra, outs):
        outs[0][...] = (alpha * accs[0]).astype(buf.dtype)

    return _mm_call(name, grid, [a, b],
                    [pl.BlockSpec((tk, tm), lambda i, j, k: (k, i)), pl.BlockSpec((tk, tn), lambda i, j, k: (k, j))],
                    [jax.ShapeDtypeStruct(buf.shape, buf.dtype)], [o_spec], 1, 1, [(0, 0, 0)], ((0,), (0,)),
                    [(tm, tn)], epilogue, into=buf)[0]


def norm_fwd(name, x, gain, *, tm=ROW_TILE):
    m, d = x.shape
    tm = _tile(m, tm, 8)

    def body(x_ref, g_ref, o_ref):
        xv = x_ref[...]
        r = lax.rsqrt(jnp.mean(xv * xv, axis=-1, keepdims=True) + NORM_EPS)
        o_ref[...] = (xv * r * g_ref[...]).astype(BF16)

    return pl.pallas_call(
        body, name=name, grid=(m // tm,),
        in_specs=[pl.BlockSpec((tm, d), lambda i: (i, 0)), pl.BlockSpec((1, d), lambda i: (0, 0))],
        out_specs=pl.BlockSpec((tm, d), lambda i: (i, 0)), out_shape=jax.ShapeDtypeStruct((m, d), BF16),
        compiler_params=_params(1),
    )(x, gain)


def norm_bwd(name, x, gain, dh, dres=None, *, tm=ROW_TILE):
    m, d = x.shape
    tm = _tile(m, tm, 8)
    n_steps = m // tm
    has_res = dres is not None

    def body(*refs):
        if has_res:
            x_ref, g_ref, dh_ref, dres_ref, dx_ref, dxb_ref, dg_ref, acc = refs
        else:
            x_ref, g_ref, dh_ref, dx_ref, dxb_ref, dg_ref, acc = refs
        i = pl.program_id(0)
        xv = x_ref[...]
        dy = dh_ref[...].astype(F32)
        r = lax.rsqrt(jnp.mean(xv * xv, axis=-1, keepdims=True) + NORM_EPS)
        xhat = xv * r
        dxhat = dy * g_ref[...]
        dx = r * (dxhat - xhat * jnp.mean(dxhat * xhat, axis=-1, keepdims=True))
        if has_res:
            dx = dx + dres_ref[...]
        dx_ref[...] = dx
        dxb_ref[...] = dx.astype(BF16)

        @pl.when(i == 0)
        def _():
            acc[...] = jnp.zeros_like(acc)

        acc[...] += jnp.sum((dy * xhat).reshape(tm // 8, 8, d), axis=0)

        @pl.when(i == n_steps - 1)
        def _():
            dg_ref[...] = jnp.sum(acc[...], axis=0, keepdims=True)

    row = pl.BlockSpec((tm, d), lambda i: (i, 0))
    vec = pl.BlockSpec((1, d), lambda i: (0, 0))
    ins = [x, gain, dh] + ([dres] if has_res else [])
    return pl.pallas_call(
        body, name=name, grid=(n_steps,), in_specs=[row, vec, row] + ([row] if has_res else []),
        out_specs=[row, row, vec],
        out_shape=[jax.ShapeDtypeStruct((m, d), F32), jax.ShapeDtypeStruct((m, d), BF16), jax.ShapeDtypeStruct((1, d), F32)],
        scratch_shapes=[pltpu.VMEM((8, d), F32)], compiler_params=_params(1),
    )(*ins)


def loss_head(name, x, gain, target, *, tm=ROW_TILE):
    m, d = x.shape
    tm = _tile(m, tm, 8)
    n_steps = m // tm

    def body(x_ref, g_ref, t_ref, loss_ref, dx_ref, dxb_ref, dg_ref, acc, lacc):
        i = pl.program_id(0)
        xv = x_ref[...]
        g = g_ref[...]
        r = lax.rsqrt(jnp.mean(xv * xv, axis=-1, keepdims=True) + NORM_EPS)
        xhat = xv * r
        err = xhat * g - t_ref[...]
        dy = err * (1.0 / d)
        dxhat = dy * g
        dx = r * (dxhat - xhat * jnp.mean(dxhat * xhat, axis=-1, keepdims=True))
        dx_ref[...] = dx
        dxb_ref[...] = dx.astype(BF16)

        @pl.when(i == 0)
        def _():
            acc[...] = jnp.zeros_like(acc)
            lacc[...] = jnp.zeros_like(lacc)

        acc[...] += jnp.sum((dy * xhat).reshape(tm // 8, 8, d), axis=0)
        lacc[...] += jnp.sum((err * err).reshape(tm // 8, 8, d), axis=0)

        @pl.when(i == n_steps - 1)
        def _():
            dg_ref[...] = jnp.sum(acc[...], axis=0, keepdims=True)
            loss_ref[...] = (0.5 / d) * jnp.sum(jnp.sum(lacc[...], axis=0, keepdims=True), axis=1, keepdims=True)

    row = pl.BlockSpec((tm, d), lambda i: (i, 0))
    vec = pl.BlockSpec((1, d), lambda i: (0, 0))
    return pl.pallas_call(
        body, name=name, grid=(n_steps,), in_specs=[row, vec, row],
        out_specs=[pl.BlockSpec((1, 1), lambda i: (0, 0)), row, row, vec],
        out_shape=[jax.ShapeDtypeStruct((1, 1), F32), jax.ShapeDtypeStruct((m, d), F32),
                   jax.ShapeDtypeStruct((m, d), BF16), jax.ShapeDtypeStruct((1, d), F32)],
        scratch_shapes=[pltpu.VMEM((8, d), F32), pltpu.VMEM((8, d), F32)], compiler_params=_params(1),
    )(x, gain, target)


def _lane_is(h, shape):
    return lax.broadcasted_iota(jnp.int32, shape, 1) == h


def _take_lane(tile, h):
    return jnp.sum(jnp.where(_lane_is(h, tile.shape), tile, 0.0), axis=1, keepdims=True)


def _merge_lanes(dst, src, start, count):
    lane = lax.broadcasted_iota(jnp.int32, dst.shape, 1)
    return jnp.where((lane >= start) & (lane < start + count), src, dst)


def _band_scores(q, kp, kc, h, c, dilation, n_heads):
    nt = (((1,), (1,)), ((), ()))
    scale = HEAD ** -0.5
    i = lax.broadcasted_iota(jnp.int32, (HEAD, HEAD), 0)
    j = lax.broadcasted_iota(jnp.int32, (HEAD, HEAD), 1)
    slope = jnp.exp(jnp.full((HEAD, HEAD), -8.0 * math.log(2.0) / n_heads, F32) * (h + 1).astype(F32))
    dist = (i - j).astype(F32) * float(dilation)
    s_c = lax.dot_general(q, kc, nt, preferred_element_type=F32) * scale - slope * dist
    s_p = lax.dot_general(q, kp, nt, preferred_element_type=F32) * scale - slope * (dist + float(dilation * HEAD))
    s_c = jnp.where(j <= i, s_c, MASK_VALUE)
    s_p = jnp.where((j >= i) & (c > 0), s_p, MASK_VALUE)
    return s_p, s_c


def dilated_fwd(name, qkv, state, dilation, n_heads, last):
    s, w3 = qkv.shape
    w = w3 // 3
    d = dilation
    rows, nb = s // d, s // (d * HEAD)
    first = state is None
    qkv_v = qkv.reshape(rows, d * w3)
    hb = min(DILATED_HEADS_PER_STEP, n_heads)
    groups = n_heads // hb

    def body(*refs):
        q_ref, kp_ref, kc_ref, vp_ref, vc_ref = refs[:5]
        refs = refs[5:]
        if not first:
            acc_in, m_in, l_in = refs[:3]
            refs = refs[3:]
        if last:
            o_ref, lse_ref = refs
        else:
            acc_out, m_out, l_out = refs
        c, g = pl.program_id(1), pl.program_id(2)
        if first:
            m_tile = l_tile = jnp.zeros((HEAD, HEAD), F32)
        else:
            m_tile, l_tile = m_in[...], l_in[...]
        stat_a, stat_b = m_tile, l_tile
        for hh in range(hb):
            h = g * hb + hh
            cs = slice(hh * HEAD, (hh + 1) * HEAD)
            s_p, s_c = _band_scores(q_ref[:, cs], kp_ref[:, cs], kc_ref[:, cs], h, c, d, n_heads)
            m_new = jnp.maximum(jnp.max(s_c, axis=1, keepdims=True), jnp.max(s_p, axis=1, keepdims=True))
            if not first:
                m_prev = _take_lane(m_tile, h)
                m_new = jnp.maximum(m_new, m_prev)
            p_c = jnp.exp(s_c - m_new)
            p_p = jnp.exp(s_p - m_new)
            l_new = jnp.sum(p_c, axis=1, keepdims=True) + jnp.sum(p_p, axis=1, keepdims=True)
            nn = (((1,), (0,)), ((), ()))
            acc = (lax.dot_general(p_c.astype(BF16), vc_ref[:, cs], nn, preferred_element_type=F32)
                   + lax.dot_general(p_p.astype(BF16), vp_ref[:, cs], nn, preferred_element_type=F32))
            if not first:
                corr = jnp.exp(m_prev - m_new)
                l_new = l_new + corr * _take_lane(l_tile, h)
                acc = acc + corr * acc_in[:, cs]
            lane = _lane_is(h, (HEAD, HEAD))
            if last:
                o_ref[:, cs] = (acc / l_new).astype(BF16)
                stat_a = jnp.where(lane, m_new + jnp.log(l_new), stat_a)
            else:
                acc_out[:, cs] = acc
                stat_a = jnp.where(lane, m_new, stat_a)
                stat_b = jnp.where(lane, l_new, stat_b)
        if last:
            @pl.when(g == 0)
            def _():
                lse_ref[...] = stat_a

            @pl.when(g > 0)
            def _():
                lse_ref[...] = _merge_lanes(lse_ref[...], stat_a, g * hb, hb)
        else:
            @pl.when(g == 0)
            def _():
                m_out[...] = stat_a
                l_out[...] = stat_b

            @pl.when(g > 0)
            def _():
                m_out[...] = _merge_lanes(m_out[...], stat_a, g * hb, hb)
                l_out[...] = _merge_lanes(l_out[...], stat_b, g * hb, hb)

    blk = (HEAD, hb * HEAD)
    prev = lambda c: jnp.maximum(c - 1, 0)
    in_specs = [
        pl.BlockSpec(blk, lambda r, c, g: (c, r * 3 * groups + g)),
        pl.BlockSpec(blk, lambda r, c, g: (prev(c), r * 3 * groups + groups + g)),
        pl.BlockSpec(blk, lambda r, c, g: (c, r * 3 * groups + groups + g)),
        pl.BlockSpec(blk, lambda r, c, g: (prev(c), r * 3 * groups + 2 * groups + g)),
        pl.BlockSpec(blk, lambda r, c, g: (c, r * 3 * groups + 2 * groups + g)),
    ]
    head_spec = pl.BlockSpec(blk, lambda r, c, g: (c, r * groups + g))
    stat_spec = pl.BlockSpec((HEAD, HEAD), lambda r, c, g: (c, r))
    ins = [qkv_v] * 5
    if not first:
        acc0, m0, l0 = state
        ins += [acc0.reshape(rows, d * w), m0.reshape(rows, d * HEAD), l0.reshape(rows, d * HEAD)]
        in_specs += [head_spec, stat_spec, stat_spec]
    stat_shape = jax.ShapeDtypeStruct((rows, d * HEAD), F32)
    if last:
        out_shape = [jax.ShapeDtypeStruct((rows, d * w), BF16), stat_shape]
        out_specs = [head_spec, stat_spec]
    else:
        out_shape = [jax.ShapeDtypeStruct((rows, d * w), F32), stat_shape, stat_shape]
        out_specs = [head_spec, stat_spec, stat_spec]
    outs = pl.pallas_call(body, name=name, grid=(d, nb, groups), in_specs=in_specs, out_specs=out_specs,
                          out_shape=out_shape, compiler_params=_params(3))(*ins)
    if last:
        return outs[0].reshape(s, w), outs[1].reshape(s, HEAD)
    return outs[0].reshape(s, w), outs[1].reshape(s, HEAD), outs[2].reshape(s, HEAD)


def dilated_bwd(name, qkv, o, do, lse, dilation, n_heads):
    s, w3 = qkv.shape
    w = w3 // 3
    d = dilation
    rows, nb = s // d, s // (d * HEAD)
    hb = min(DILATED_HEADS_PER_STEP, n_heads)
    groups = n_heads // hb

    def body(q_ref, kp_ref, kc_ref, vp_ref, vc_ref, o_ref, do_ref, lse_ref, dq_ref, dk_ref, dv_ref, dk_c, dv_c):
        g, c = pl.program_id(1), pl.program_id(2)

        @pl.when(c == 0)
        def _():
            dk_c[...] = jnp.zeros_like(dk_c)
            dv_c[...] = jnp.zeros_like(dv_c)

        @pl.when(c < nb)
        def _():
            lse_tile = lse_ref[...]
            nt = (((1,), (1,)), ((), ()))
            nn = (((1,), (0,)), ((), ()))
            tn = (((0,), (0,)), ((), ()))
            scale = HEAD ** -0.5
            for hh in range(hb):
                h = g * hb + hh
                cs = slice(hh * HEAD, (hh + 1) * HEAD)
                q, kp, kc, vp, vc, dout = q_ref[:, cs], kp_ref[:, cs], kc_ref[:, cs], vp_ref[:, cs], vc_ref[:, cs], do_ref[:, cs]
                s_p, s_c = _band_scores(q, kp, kc, h, c, d, n_heads)
                lse_h = _take_lane(lse_tile, h)
                delta = jnp.sum(dout.astype(F32) * o_ref[:, cs].astype(F32), axis=1, keepdims=True)
                p_c = jnp.exp(s_c - lse_h)
                p_p = jnp.exp(s_p - lse_h)
                ds_c = (p_c * (lax.dot_general(dout, vc, nt, preferred_element_type=F32) - delta) * scale).astype(BF16)
                ds_p = (p_p * (lax.dot_general(dout, vp, nt, preferred_element_type=F32) - delta) * scale).astype(BF16)
                dq_ref[:, cs] = (lax.dot_general(ds_c, kc, nn, preferred_element_type=F32)
                                 + lax.dot_general(ds_p, kp, nn, preferred_element_type=F32)).astype(BF16)
                dk_ref[:, cs] = (dk_c[:, cs] + lax.dot_general(ds_p, q, tn, preferred_element_type=F32)).astype(BF16)
                dv_ref[:, cs] = (dv_c[:, cs]
                                 + lax.dot_general(p_p.astype(BF16), dout, tn, preferred_element_type=F32)).astype(BF16)
                dk_c[:, cs] = lax.dot_general(ds_c, q, tn, preferred_element_type=F32)
                dv_c[:, cs] = lax.dot_general(p_c.astype(BF16), dout, tn, preferred_element_type=F32)

        @pl.when(c == nb)
        def _():
            dk_ref[...] = dk_c[...].astype(BF16)
            dv_ref[...] = dv_c[...].astype(BF16)

    blk = (HEAD, hb * HEAD)
    cur = lambda c: jnp.minimum(c, nb - 1)
    prev = lambda c: jnp.clip(c - 1, 0, nb - 1)
    in_specs = [
        pl.BlockSpec(blk, lambda r, g, c: (cur(c), r * 3 * groups + g)),
        pl.BlockSpec(blk, lambda r, g, c: (prev(c), r * 3 * groups + groups + g)),
        pl.BlockSpec(blk, lambda r, g, c: (cur(c), r * 3 * groups + groups + g)),
        pl.BlockSpec(blk, lambda r, g, c: (prev(c), r * 3 * groups + 2 * groups + g)),
        pl.BlockSpec(blk, lambda r, g, c: (cur(c), r * 3 * groups + 2 * groups + g)),
        pl.BlockSpec(blk, lambda r, g, c: (cur(c), r * groups + g)),
        pl.BlockSpec(blk, lambda r, g, c: (cur(c), r * groups + g)),
        pl.BlockSpec((HEAD, HEAD), lambda r, g, c: (cur(c), r)),
    ]
    out_specs = [
        pl.BlockSpec(blk, lambda r, g, c: (cur(c), r * groups + g)),
        pl.BlockSpec(blk, lambda r, g, c: (prev(c), r * groups + g)),
        pl.BlockSpec(blk, lambda r, g, c: (prev(c), r * groups + g)),
    ]
    qkv_v = qkv.reshape(rows, d * w3)
    outs = pl.pallas_call(
        body, name=name, grid=(d, groups, nb + 1), in_specs=in_specs, out_specs=out_specs,
        out_shape=[jax.ShapeDtypeStruct((rows, d * w), BF16)] * 3,
        scratch_shapes=[pltpu.VMEM(blk, F32)] * 2, compiler_params=_params(3),
    )(qkv_v, qkv_v, qkv_v, qkv_v, qkv_v, o.reshape(rows, d * w), do.reshape(rows, d * w), lse.reshape(rows, d * HEAD))
    return [t.reshape(s, w) for t in outs]


def sum_branches(name, parts, *, tm=ROW_TILE):
    s, w = parts[0][0].shape
    tm = _tile(s, tm, 8)

    def body(*refs):
        out = refs[9]
        for t in range(3):
            acc = refs[t][...].astype(F32) + refs[3 + t][...].astype(F32) + refs[6 + t][...].astype(F32)
            out[:, t * w:(t + 1) * w] = acc.astype(BF16)

    row = pl.BlockSpec((tm, w), lambda i: (i, 0))
    return pl.pallas_call(
        body, name=name, grid=(s // tm,), in_specs=[row] * 9, out_specs=pl.BlockSpec((tm, 3 * w), lambda i: (i, 0)),
        out_shape=jax.ShapeDtypeStruct((s, 3 * w), BF16), compiler_params=_params(1),
    )(*[t for trip in parts for t in trip])


def _triangle(nq, n_heads, by_key):
    a, hh, b = [], [], []
    for outer in range(nq):
        for h in range(n_heads):
            inner = range(outer, nq) if by_key else range(outer + 1)
            for t in inner:
                a.append(outer)
                hh.append(h)
                b.append(t)
    return (jnp.asarray(np.array(a, np.int32)), jnp.asarray(np.array(hh, np.int32)), jnp.asarray(np.array(b, np.int32)))


def _causal_mask(qi, ki, t):
    row = lax.broadcasted_iota(jnp.int32, (t, t), 0) + qi * t
    col = lax.broadcasted_iota(jnp.int32, (t, t), 1) + ki * t
    return col <= row


def _causal_scores(q, kn, kr, mask):
    nt = (((1,), (1,)), ((), ()))
    k = jnp.concatenate([kn, kr], axis=1)
    s = lax.dot_general(q, k, nt, preferred_element_type=F32)
    return s if mask is None else jnp.where(mask, s, MASK_VALUE)


def _on_and_off_diagonal(qi, ki, t, step):
    @pl.when(ki == qi)
    def _():
        step(_causal_mask(qi, ki, t))

    @pl.when(ki != qi)
    def _():
        step(None)


def _heads(hh):
    return slice(hh * QK_PAD, (hh + 1) * QK_PAD), slice(hh * HEAD, (hh + 1) * HEAD)


def mla_fwd(name, q, kn, kr, v, n_heads, *, t=ATTN_TILE):
    s = q.shape[0]
    t = _tile(s, t)
    nq = s // t
    hb = min(MLA_FWD_HEADS_PER_STEP, n_heads)
    qt, gt, kt = _triangle(nq, n_heads // hb, by_key=False)

    def body(qt_ref, gt_ref, kt_ref, q_ref, kn_ref, kr_ref, v_ref, o_ref, lse_ref, m_sc, acc_sc):
        step = pl.program_id(0)
        qi, g, ki = qt_ref[step], gt_ref[step], kt_ref[step]

        @pl.when(ki == 0)
        def _():
            m_sc[...] = jnp.full_like(m_sc, MASK_VALUE)
            acc_sc[...] = jnp.zeros_like(acc_sc)

        def scores_step(mask):
            kr_v = kr_ref[...]
            ones = jnp.ones((t, HEAD), BF16)
            for hh in range(hb):
                wide, cs = _heads(hh)
                sc = _causal_scores(q_ref[:, wide], kn_ref[:, cs], kr_v, mask)
                m_prev = m_sc[hh]
                m_new = jnp.maximum(m_prev, jnp.max(sc, axis=1, keepdims=True))
                p = jnp.exp2(sc - m_new)
                v_ext = jnp.concatenate([v_ref[:, cs], ones], axis=1)
                acc_sc[hh] = jnp.exp2(m_prev - m_new) * acc_sc[hh] + lax.dot_general(
                    p.astype(BF16), v_ext, (((1,), (0,)), ((), ())), preferred_element_type=F32)
                m_sc[hh] = m_new

        _on_and_off_diagonal(qi, ki, t, scores_step)

        @pl.when(ki == qi)
        def _():
            @pl.when(g == 0)
            def _():
                lse_ref[...] = jnp.zeros_like(lse_ref)

            tile = lse_ref[...]
            for hh in range(hb):
                acc = acc_sc[hh]
                l = acc[:, HEAD:]
                o_ref[:, _heads(hh)[1]] = (acc[:, :HEAD] / l).astype(BF16)
                tile = jnp.where(_lane_is(g * hb + hh, (t, HEAD)), m_sc[hh] + jnp.log2(l), tile)
            lse_ref[...] = tile

    grid_spec = pltpu.PrefetchScalarGridSpec(
        num_scalar_prefetch=3, grid=(int(qt.shape[0]),),
        in_specs=[
            pl.BlockSpec((t, hb * QK_PAD), lambda i, a, b, c: (a[i], b[i])),
            pl.BlockSpec((t, hb * HEAD), lambda i, a, b, c: (c[i], b[i])),
            pl.BlockSpec((t, HEAD), lambda i, a, b, c: (c[i], 0)),
            pl.BlockSpec((t, hb * HEAD), lambda i, a, b, c: (c[i], b[i])),
        ],
        out_specs=[pl.BlockSpec((t, hb * HEAD), lambda i, a, b, c: (a[i], b[i])),
                   pl.BlockSpec((t, HEAD), lambda i, a, b, c: (a[i], 0))],
        scratch_shapes=[pltpu.VMEM((hb, t, 1), F32), pltpu.VMEM((hb, t, 2 * HEAD), F32)],
    )
    return pl.pallas_call(
        body, name=name, grid_spec=grid_spec,
        out_shape=[jax.ShapeDtypeStruct((s, n_heads * HEAD), BF16), jax.ShapeDtypeStruct((s, HEAD), F32)],
        compiler_params=_params(1),
    )(qt, gt, kt, q, kn, kr, v)


def mla_bwd_dq(name, q, kn, kr, v, o, do, lse, tabs, n_heads, *, t=ATTN_TILE):
    s = q.shape[0]
    t = _tile(s, t)
    nq = s // t
    hb = min(MLA_BWD_HEADS_PER_STEP, n_heads)
    qt, gt, kt = _triangle(nq, n_heads // hb, by_key=False)

    def body(qt_ref, gt_ref, kt_ref, q_ref, kn_ref, kr_ref, v_ref, o_ref, do_ref, lse_ref, cs_ref, s1_ref, s2_ref,
             dq_ref, delta_ref, acc_sc, delta_sc):
        step = pl.program_id(0)
        qi, g, ki = qt_ref[step], gt_ref[step], kt_ref[step]

        @pl.when(ki == 0)
        def _():
            acc_sc[...] = jnp.zeros_like(acc_sc)

            @pl.when(g == 0)
            def _():
                delta_ref[...] = jnp.zeros_like(delta_ref)

            tile = delta_ref[...]
            for hh in range(hb):
                cs = _heads(hh)[1]
                delta = jnp.sum(do_ref[:, cs].astype(F32) * o_ref[:, cs].astype(F32), axis=1, keepdims=True)
                delta_sc[hh] = delta
                tile = jnp.where(_lane_is(g * hb + hh, (t, HEAD)), delta, tile)
            delta_ref[...] = tile

        def scores_step(mask):
            kr_v, lse_tile = kr_ref[...], lse_ref[...]
            for hh in range(hb):
                wide, cs = _heads(hh)
                kn_v = kn_ref[:, cs]
                sc = _causal_scores(q_ref[:, wide], kn_v, kr_v, mask)
                p = jnp.exp2(sc - _take_lane(lse_tile, g * hb + hh))
                dp = lax.dot_general(do_ref[:, cs], v_ref[:, cs], (((1,), (1,)), ((), ())), preferred_element_type=F32)
                ds = (p * (dp - delta_sc[hh])).astype(BF16)
                k = jnp.concatenate([kn_v, kr_v], axis=1)
                acc_sc[hh] += lax.dot_general(ds, k, (((1,), (0,)), ((), ())), preferred_element_type=F32)

        _on_and_off_diagonal(qi, ki, t, scores_step)

        @pl.when(ki == qi)
        def _():
            for hh in range(hb):
                dq = acc_sc[hh]
                c0 = hh * QK_PAD
                dq_ref[:, c0:c0 + HEAD] = dq[:, :HEAD].astype(BF16)
                dq_ref[:, c0 + HEAD:c0 + QK_PAD] = _rope_tile(dq[:, HEAD:], cs_ref[...], s1_ref[...],
                                                              s2_ref[...]).astype(BF16)

    tab = pl.BlockSpec((t, HEAD), lambda i, a, b, c: (a[i], 0))
    grid_spec = pltpu.PrefetchScalarGridSpec(
        num_scalar_prefetch=3, grid=(int(qt.shape[0]),),
        in_specs=[
            pl.BlockSpec((t, hb * QK_PAD), lambda i, a, b, c: (a[i], b[i])),
            pl.BlockSpec((t, hb * HEAD), lambda i, a, b, c: (c[i], b[i])),
            pl.BlockSpec((t, HEAD), lambda i, a, b, c: (c[i], 0)),
            pl.BlockSpec((t, hb * HEAD), lambda i, a, b, c: (c[i], b[i])),
            pl.BlockSpec((t, hb * HEAD), lambda i, a, b, c: (a[i], b[i])),
            pl.BlockSpec((t, hb * HEAD), lambda i, a, b, c: (a[i], b[i])),
            tab, tab, tab, tab,
        ],
        out_specs=[pl.BlockSpec((t, hb * QK_PAD), lambda i, a, b, c: (a[i], b[i])), tab],
        scratch_shapes=[pltpu.VMEM((hb, t, QK_PAD), F32), pltpu.VMEM((hb, t, 1), F32)],
    )
    return pl.pallas_call(
        body, name=name, grid_spec=grid_spec,
        out_shape=[jax.ShapeDtypeStruct((s, n_heads * QK_PAD), BF16), jax.ShapeDtypeStruct((s, HEAD), F32)],
        compiler_params=_params(1),
    )(qt, gt, kt, q, kn, kr, v, o, do, lse, *tabs)


def mla_bwd_dkv(name, q, kn, kr, v, do, lse, delta, prev, n_heads, inv_q_scale, inv_do_scale, *, t=ATTN_TILE):
    s = q.shape[0]
    t = _tile(s, t)
    nq = s // t
    hb = min(MLA_BWD_HEADS_PER_STEP, n_heads)
    kt, gt, qt = _triangle(nq, n_heads // hb, by_key=True)
    has_prev = prev is not None

    def body(*refs):
        kt_ref, gt_ref, qt_ref, q_ref, kn_ref, kr_ref, v_ref, do_ref, lse_ref, delta_ref = refs[:10]
        refs = refs[10:]
        if has_prev:
            pkn_ref, pkr_ref, pv_ref = refs[:3]
            refs = refs[3:]
        dkn_ref, dkr_ref, dv_ref, dk_sc, dv_sc = refs
        step = pl.program_id(0)
        ki, g, qi = kt_ref[step], gt_ref[step], qt_ref[step]

        @pl.when(qi == ki)
        def _():
            dk_sc[...] = jnp.zeros_like(dk_sc)
            dv_sc[...] = jnp.zeros_like(dv_sc)

        def scores_step(mask):
            kr_v, lse_tile, delta_tile = kr_ref[...], lse_ref[...], delta_ref[...]
            tn = (((0,), (0,)), ((), ()))
            for hh in range(hb):
                wide, cs = _heads(hh)
                q_v, dout = q_ref[:, wide], do_ref[:, cs]
                sc = _causal_scores(q_v, kn_ref[:, cs], kr_v, mask)
                p = jnp.exp2(sc - _take_lane(lse_tile, g * hb + hh))
                dp = lax.dot_general(dout, v_ref[:, cs], (((1,), (1,)), ((), ())), preferred_element_type=F32)
                ds = (p * (dp - _take_lane(delta_tile, g * hb + hh))).astype(BF16)
                dk_sc[hh] += lax.dot_general(ds, q_v, tn, preferred_element_type=F32)
                dv_sc[hh] += lax.dot_general(p.astype(BF16), dout, tn, preferred_element_type=F32)

        _on_and_off_diagonal(qi, ki, t, scores_step)

        @pl.when(qi == nq - 1)
        def _():
            @pl.when(g == 0)
            def _():
                dkr_ref[...] = pkr_ref[...] if has_prev else jnp.zeros_like(dkr_ref)

            dkr = dkr_ref[...]
            for hh in range(hb):
                cs = _heads(hh)[1]
                dk = dk_sc[hh] * inv_q_scale
                dkn, dv = dk[:, :HEAD], dv_sc[hh] * inv_do_scale
                if has_prev:
                    dkn = dkn + pkn_ref[:, cs].astype(F32)
                    dv = dv + pv_ref[:, cs].astype(F32)
                dkn_ref[:, cs] = dkn.astype(BF16)
                dv_ref[:, cs] = dv.astype(BF16)
                dkr = dkr + dk[:, HEAD:]
            dkr_ref[...] = dkr

    head_q = pl.BlockSpec((t, hb * HEAD), lambda i, a, b, c: (c[i], b[i]))
    head_k = pl.BlockSpec((t, hb * HEAD), lambda i, a, b, c: (a[i], b[i]))
    shared_k = pl.BlockSpec((t, HEAD), lambda i, a, b, c: (a[i], 0))
    in_specs = [
        pl.BlockSpec((t, hb * QK_PAD), lambda i, a, b, c: (c[i], b[i])), head_k, shared_k, head_k, head_q,
        pl.BlockSpec((t, HEAD), lambda i, a, b, c: (c[i], 0)), pl.BlockSpec((t, HEAD), lambda i, a, b, c: (c[i], 0)),
    ]
    ins = [q, kn, kr, v, do, lse, delta]
    if has_prev:
        in_specs += [head_k, shared_k, head_k]
        ins += list(prev)
    grid_spec = pltpu.PrefetchScalarGridSpec(
        num_scalar_prefetch=3, grid=(int(kt.shape[0]),), in_specs=in_specs, out_specs=[head_k, shared_k, head_k],
        scratch_shapes=[pltpu.VMEM((hb, t, QK_PAD), F32), pltpu.VMEM((hb, t, HEAD), F32)],
    )
    return pl.pallas_call(
        body, name=name, grid_spec=grid_spec,
        out_shape=[jax.ShapeDtypeStruct((s, n_heads * HEAD), BF16), jax.ShapeDtypeStruct((s, HEAD), F32),
                   jax.ShapeDtypeStruct((s, n_heads * HEAD), BF16)],
        compiler_params=_params(1),
    )(kt, gt, qt, *ins)


def rope_rows(name, z, tabs, *, tm=ATTN_TILE):
    s = z.shape[0]
    tm = _tile(s, tm, 8)

    def body(z_ref, cs_ref, s1_ref, s2_ref, o_ref):
        o_ref[...] = _rope_tile(z_ref[...], cs_ref[...], s1_ref[...], s2_ref[...]).astype(BF16)

    row = pl.BlockSpec((tm, HEAD), lambda i: (i, 0))
    return pl.pallas_call(body, name=name, grid=(s // tm,), in_specs=[row] * 4, out_specs=row,
                          out_shape=jax.ShapeDtypeStruct((s, HEAD), BF16), compiler_params=_params(1))(z, *tabs)


def rope_tables(s):
    inv = 1.0 / (ROPE_THETA ** (jnp.arange(0, ROPE, 2, dtype=F32) / ROPE))
    ang = jnp.arange(s, dtype=F32)[:, None] * inv[None, :]
    cos, sin, zero = jnp.cos(ang), jnp.sin(ang), jnp.zeros((s, ROPE // 2), F32)
    pad = jnp.zeros((s, HEAD - ROPE), F32)
    cs = jnp.concatenate([cos, cos, pad], axis=1)
    lo = jnp.concatenate([sin, zero, pad], axis=1)
    hi = jnp.concatenate([zero, sin, pad], axis=1)
    return (cs, -lo, hi), (cs, lo, -hi)


def local_step(x, target, w, norms):
    s, d = x.shape
    depth = norms['ffn_norm1'].shape[0]
    n_a = w['a_wqkv'].shape[0]
    gw = {k: lax.empty(v.shape, BF16) for k, v in w.items()}
    n_heads = d // HEAD
    fwd_tabs, bwd_tabs = rope_tables(s)
    mla_scale = (HEAD + ROPE) ** -0.5
    q_scale = mla_scale * math.log2(math.e)
    gain = lambda name, i: norms[name][i:i + 1]
    saved = []
    kv = None

    def ffn_fwd(tag, x, g_norm, which, layer):
        wg, wu, wd = ((w[f'{which}_{nm}'], layer) for nm in ('wg', 'wu', 'wd'))
        h = norm_fwd(f'{tag}_norm', x, g_norm)
        g, u, a = ffn_up(f'{tag}_up', h, wg, wu)
        y = mm_nn(f'{tag}_down', a, wd, out_dtype=F32, res=x, alpha=0.5)
        return y, (x, h, g, u, a)

    def ffn_bwd(tag, dx, dxb, g_norm, which, layer, sv):
        wg, wu, wd = ((w[f'{which}_{nm}'], layer) for nm in ('wg', 'wu', 'wd'))
        x, h, g, u, a = sv
        dg, du = mm_nt(f'{tag}_bwd_da', [dxb], [wd], out_dtype=BF16, alpha=0.5, swiglu=(g, u))
        gw[f'{which}_wd'] = mm_tn(f'{tag}_bwd_dwd', a, dxb, (gw[f'{which}_wd'], layer), alpha=0.5)
        gw[f'{which}_wg'] = mm_tn(f'{tag}_bwd_dwg', h, dg, (gw[f'{which}_wg'], layer))
        gw[f'{which}_wu'] = mm_tn(f'{tag}_bwd_dwu', h, du, (gw[f'{which}_wu'], layer))
        dh = mm_nt(f'{tag}_bwd_dh', [dg, du], [wg, wu], out_dtype=F32)
        return norm_bwd(f'{tag}_bwd_norm', x, g_norm, dh, dx)

    for layer in range(depth):
        if layer == n_a:
            hk = norm_fwd('kv_norm', x, norms['kv_norm'])
            ckv_raw = mm_nn('kv_down', hk, (w['b_wdkv'], 0), out_dtype=F32)
            ckv = norm_fwd('kv_cnorm', ckv_raw, norms['b_ckv_norm'])
            k_nope = mm_nn('kv_uk', ckv, (w['b_wuk'], 0), out_dtype=BF16)
            v_shared = mm_nn('kv_uv', ckv, (w['b_wuv'], 0), out_dtype=BF16)
            k_rope = mm_nn('kv_kr', hk, (w['b_wkr'], 0), out_dtype=BF16, rope=('k',) + fwd_tabs, tn=HEAD)
            kv = (x, hk, ckv_raw, ckv, k_nope, k_rope, v_shared)
        x, sv1 = ffn_fwd(f'l{layer}_ffn1', x, gain('ffn_norm1', layer), 'ffn1', layer)
        h = norm_fwd(f'l{layer}_mix_norm', x, gain('mix_norm', layer))
        if layer < n_a:
            qkv = mm_nn(f'l{layer}_qkv', h, (w['a_wqkv'], layer), out_dtype=BF16)
            state = None
            for bi, (window, dilation) in enumerate(DILATED_BRANCHES):
                last = bi == len(DILATED_BRANCHES) - 1
                state = dilated_fwd(f'l{layer}_dil{dilation}', qkv, state, dilation, n_heads, last)
            o, lse = state
            x_new = mm_nn(f'l{layer}_wo', o, (w['a_wo'], layer), out_dtype=F32, res=x)
            svm = (x, h, qkv, o, lse)
        else:
            jb = layer - n_a
            cq_raw = mm_nn(f'l{layer}_dq', h, (w['b_wdq'], jb), out_dtype=F32)
            cq = norm_fwd(f'l{layer}_cq_norm', cq_raw, gain('b_cq_norm', jb))
            q = mm_nn(f'l{layer}_uq', cq, (w['b_wuq'], jb), out_dtype=BF16, alpha=q_scale, rope=('q',) + fwd_tabs)
            o, lse = mla_fwd(f'l{layer}_mla', q, kv[4], kv[5], kv[6], n_heads)
            x_new = mm_nn(f'l{layer}_wo', o, (w['b_wo'], jb), out_dtype=F32, res=x)
            svm = (x, h, cq_raw, cq, q, o, lse)
        x = x_new
        x, sv2 = ffn_fwd(f'l{layer}_ffn2', x, gain('ffn_norm2', layer), 'ffn2', layer)
        saved.append((sv1, svm, sv2))

    loss, dx, dxb, d_final = loss_head('loss_head', x, norms['final_norm'], target)

    gn = {k: [None] * v.shape[0] for k, v in norms.items()}
    gn['final_norm'] = [d_final]
    dkv = None
    for layer in reversed(range(depth)):
        sv1, svm, sv2 = saved[layer]
        dx, dxb, gn['ffn_norm2'][layer] = ffn_bwd(f'l{layer}_ffn2', dx, dxb, gain('ffn_norm2', layer), 'ffn2', layer, sv2)
        if layer < n_a:
            xm, h, qkv, o, lse = svm
            do = mm_nt(f'l{layer}_bwd_do', [dxb], [(w['a_wo'], layer)], out_dtype=BF16)
            gw['a_wo'] = mm_tn(f'l{layer}_bwd_dwo', o, dxb, (gw['a_wo'], layer))
            parts = [dilated_bwd(f'l{layer}_bwd_dil{dilation}', qkv, o, do, lse, dilation, n_heads)
                     for _, dilation in DILATED_BRANCHES]
            dqkv = sum_branches(f'l{layer}_bwd_sum', parts)
            gw['a_wqkv'] = mm_tn(f'l{layer}_bwd_dwqkv', h, dqkv, (gw['a_wqkv'], layer))
            dh = mm_nt(f'l{layer}_bwd_dh', [dqkv], [(w['a_wqkv'], layer)], out_dtype=F32)
        else:
            jb = layer - n_a
            xm, h, cq_raw, cq, q, o, lse = svm
            do = mm_nt(f'l{layer}_bwd_do', [dxb], [(w['b_wo'], jb)], out_dtype=BF16, alpha=mla_scale)
            gw['b_wo'] = mm_tn(f'l{layer}_bwd_dwo', o, dxb, (gw['b_wo'], jb))
            dq, delta = mla_bwd_dq(f'l{layer}_bwd_mla_dq', q, kv[4], kv[5], kv[6], o, do, lse, bwd_tabs, n_heads)
            dkv = mla_bwd_dkv(f'l{layer}_bwd_mla_dkv', q, kv[4], kv[5], kv[6], do, lse, delta, dkv, n_heads,
                              1.0 / q_scale, 1.0 / mla_scale)
            gw['b_wuq'] = mm_tn(f'l{layer}_bwd_dwuq', cq, dq, (gw['b_wuq'], jb))
            dcq = mm_nt(f'l{layer}_bwd_dcq', [dq], [(w['b_wuq'], jb)], out_dtype=F32)
            _, dcq_raw, gn['b_cq_norm'][jb] = norm_bwd(f'l{layer}_bwd_cq_norm', cq_raw, gain('b_cq_norm', jb), dcq)
            gw['b_wdq'] = mm_tn(f'l{layer}_bwd_dwdq', h, dcq_raw, (gw['b_wdq'], jb))
            dh = mm_nt(f'l{layer}_bwd_dh', [dcq_raw], [(w['b_wdq'], jb)], out_dtype=F32)
        dx, dxb, gn['mix_norm'][layer] = norm_bwd(f'l{layer}_bwd_mix_norm', xm, gain('mix_norm', layer), dh, dx)
        dx, dxb, gn['ffn_norm1'][layer] = ffn_bwd(f'l{layer}_ffn1', dx, dxb, gain('ffn_norm1', layer), 'ffn1', layer, sv1)
        if layer == n_a:
            xk, hk, ckv_raw, ckv, k_nope, k_rope, v_shared = kv
            dkn, dkr_rot, dv = dkv
            dkr = rope_rows('kv_bwd_rope', dkr_rot, bwd_tabs)
            gw['b_wuk'] = mm_tn('kv_bwd_dwuk', ckv, dkn, (gw['b_wuk'], 0))
            gw['b_wuv'] = mm_tn('kv_bwd_dwuv', ckv, dv, (gw['b_wuv'], 0))
            dckv = mm_nt('kv_bwd_dckv', [dkn, dv], [(w['b_wuk'], 0), (w['b_wuv'], 0)], out_dtype=F32)
            _, dckv_raw, g_ckv = norm_bwd('kv_bwd_cnorm', ckv_raw, norms['b_ckv_norm'], dckv)
            gw['b_wdkv'] = mm_tn('kv_bwd_dwdkv', hk, dckv_raw, (gw['b_wdkv'], 0))
            gw['b_wkr'] = mm_tn('kv_bwd_dwkr', hk, dkr, (gw['b_wkr'], 0))
            dhk = mm_nt('kv_bwd_dhk_c', [dckv_raw], [(w['b_wdkv'], 0)], out_dtype=F32)
            dhk = mm_nt('kv_bwd_dhk_r', [dkr], [(w['b_wkr'], 0)], out_dtype=F32, res=dhk)
            dx, dxb, g_kv = norm_bwd('kv_bwd_norm', xk, norms['kv_norm'], dhk, dx)
            gn['kv_norm'], gn['b_ckv_norm'] = [g_kv], [g_ckv]
    grads_n = {k: jnp.concatenate(v, axis=0) for k, v in gn.items()}
    return loss, dx, gw, grads_n


def _place():
    x, y, c = lax.axis_index("x"), lax.axis_index("y"), lax.axis_index("c")
    chips = [(1 - x, y), (x, 1 - y), (1 - x, 1 - y)]
    return x, y, c, 2 * x + y, chips, [2 * cx + cy for cx, cy in chips]


def _remote(src, dst, send_sem, recv_sem, to):
    return pltpu.make_async_remote_copy(src_ref=src, dst_ref=dst, send_sem=send_sem, recv_sem=recv_sem,
                                        device_id=to, device_id_type=MESH)


def cast_to_slab(name, shard, place, *, tr=ROW_TILE):
    l, r, c = shard.shape
    tr = _tile(r, tr, 16)

    def body(place_ref, x_ref, o_ref):
        o_ref[...] = x_ref[...].astype(BF16)

    grid_spec = pltpu.PrefetchScalarGridSpec(
        num_scalar_prefetch=1, grid=(l, r // tr),
        in_specs=[pl.BlockSpec((None, tr, c), lambda i, j, p: (i, j, 0))],
        out_specs=pl.BlockSpec((None, None, tr, c), lambda i, j, p: (i, p[0], j, 0)),
    )
    return pl.pallas_call(body, name=name, grid_spec=grid_spec,
                          out_shape=jax.ShapeDtypeStruct((l, N_CHIPS, r, c), BF16), compiler_params=_params(2))(place, shard)


def all_gather_weights(name, slabs):
    n = len(slabs)

    def body(*refs):
        outs = refs[n:2 * n]
        send_sems, recv_sems = refs[2 * n:]
        x, y, c, me, chips, chip_ids = _place()
        sibling = (x, y, 1 - c)

        def copy(t, k, chip, hc, to, quarter=None):
            rh = outs[t].shape[2] // 2
            rows = pl.ds(hc * rh, rh) if quarter is None else pl.ds(hc * rh + quarter * (rh // 2), rh // 2)
            blk = outs[t].at[:, chip, rows, :]
            return _remote(blk, blk, send_sems.at[8 * t + k], recv_sems.at[8 * t + k], to)

        x_nbr, y_nbr = (*chips[0], c), (*chips[1], c)
        first = [copy(t, k, me, c, (x_nbr, y_nbr)[k]) for t in range(n) for k in range(2)]
        for cp in first:
            cp.start()
        passed = []
        for t in range(n):
            copy(t, 0, chip_ids[0], c, sibling).wait_recv()
            copy(t, 1, chip_ids[1], c, sibling).wait_recv()
            later = [copy(t, 2, chip_ids[0], c, y_nbr, quarter=0), copy(t, 3, chip_ids[1], c, x_nbr, quarter=1),
                     copy(t, 4, chip_ids[0], c, sibling), copy(t, 5, chip_ids[1], c, sibling)]
            for cp in later:
                cp.start()
            passed += later
        for t in range(n):
            copy(t, 2, chip_ids[2], c, sibling, quarter=0).wait_recv()
            copy(t, 3, chip_ids[2], c, sibling, quarter=1).wait_recv()
            cp = copy(t, 6, chip_ids[2], c, sibling)
            cp.start()
            passed.append(cp)
        for t in range(n):
            for k in range(3):
                copy(t, 4 + k, chip_ids[k], 1 - c, sibling).wait_recv()
        for cp in first + passed:
            cp.wait_send()

    any_spec = pl.BlockSpec(memory_space=pl.ANY)
    return pl.pallas_call(
        body, name=name, in_specs=[any_spec] * n, out_specs=[any_spec] * n,
        out_shape=[jax.ShapeDtypeStruct(a.shape, a.dtype) for a in slabs],
        scratch_shapes=[pltpu.SemaphoreType.DMA((8 * n,)), pltpu.SemaphoreType.DMA((8 * n,))],
        input_output_aliases={t: t for t in range(n)},
        compiler_params=pltpu.CompilerParams(has_side_effects=True),
    )(*slabs)


def exchange_halves(name, parts):
    n = len(parts)

    def body(*refs):
        ins, outs = refs[:n], refs[n:2 * n]
        send_sems, recv_sems = refs[2 * n:]
        x, y, c, me, chips, chip_ids = _place()
        copies = []
        for t in range(n):
            rh = ins[t].shape[2] // 2
            cp = _remote(ins[t].at[:, :, pl.ds((1 - c) * rh, rh), :], outs[t], send_sems.at[t], recv_sems.at[t],
                         (x, y, 1 - c))
            cp.start()
            copies.append(cp)
        for cp in copies:
            cp.wait()

    any_spec = pl.BlockSpec(memory_space=pl.ANY)
    return pl.pallas_call(
        body, name=name, in_specs=[any_spec] * n, out_specs=[any_spec] * n,
        out_shape=[jax.ShapeDtypeStruct((a.shape[0], a.shape[1], a.shape[2] // 2, a.shape[3]), a.dtype) for a in parts],
        scratch_shapes=[pltpu.SemaphoreType.DMA((n,)), pltpu.SemaphoreType.DMA((n,))],
        compiler_params=pltpu.CompilerParams(has_side_effects=True),
    )(*parts)


def add_halves(name, part, recv, place, *, tr=ROW_TILE):
    l, j, rh, c = recv.shape
    tr = _tile(rh, tr, 16)
    nr = rh // tr

    def body(place_ref, p_ref, r_ref, o_ref):
        o_ref[...] = (p_ref[...].astype(F32) + r_ref[...].astype(F32)).astype(BF16)

    blk = (None, None, tr, c)
    grid_spec = pltpu.PrefetchScalarGridSpec(
        num_scalar_prefetch=1, grid=(l, j, nr),
        in_specs=[pl.BlockSpec(blk, lambda a, b, i, p: (a, b, p[1] * nr + i, 0)),
                  pl.BlockSpec(blk, lambda a, b, i, p: (a, b, i, 0))],
        out_specs=pl.BlockSpec(blk, lambda a, b, i, p: (a, b, i, 0)),
    )
    return pl.pallas_call(body, name=name, grid_spec=grid_spec, out_shape=jax.ShapeDtypeStruct(recv.shape, BF16),
                          compiler_params=_params(3))(place, part, recv)


def scatter_to_chips(name, sums):
    n = len(sums)

    def body(*refs):
        ins, outs = refs[:n], refs[n:2 * n]
        send_sems, recv_sems = refs[2 * n:]
        x, y, c, me, chips, chip_ids = _place()
        copies = []
        for t in range(n):
            for k in range(3):
                cp = _remote(ins[t].at[:, chip_ids[k]], outs[t].at[k], send_sems.at[3 * t + k], recv_sems.at[3 * t + k],
                             (*chips[k], c))
                cp.start()
                copies.append(cp)
        for cp in copies:
            cp.wait()

    any_spec = pl.BlockSpec(memory_space=pl.ANY)
    return pl.pallas_call(
        body, name=name, in_specs=[any_spec] * n, out_specs=[any_spec] * n,
        out_shape=[jax.ShapeDtypeStruct((3, a.shape[0], a.shape[2], a.shape[3]), a.dtype) for a in sums],
        scratch_shapes=[pltpu.SemaphoreType.DMA((3 * n,)), pltpu.SemaphoreType.DMA((3 * n,))],
        compiler_params=pltpu.CompilerParams(has_side_effects=True),
    )(*sums)


def reduce_own(name, part, recv1, recv2, place, *, tr=ROW_TILE):
    l, j, rh, c = recv1.shape
    tr = _tile(rh, tr, 16)
    nr = rh // tr

    def body(place_ref, p_ref, r1_ref, a_ref, b_ref, c_ref, o_ref):
        acc = p_ref[...].astype(F32) + r1_ref[...].astype(F32)
        acc = acc + a_ref[...].astype(F32)
        acc = acc + b_ref[...].astype(F32)
        o_ref[...] = acc + c_ref[...].astype(F32)

    blk = (None, None, tr, c)
    grid_spec = pltpu.PrefetchScalarGridSpec(
        num_scalar_prefetch=1, grid=(l, nr),
        in_specs=[pl.BlockSpec(blk, lambda a, i, p: (a, p[0], p[1] * nr + i, 0)),
                  pl.BlockSpec(blk, lambda a, i, p: (a, p[0], i, 0)),
                  pl.BlockSpec(blk, lambda a, i, p: (0, a, i, 0)),
                  pl.BlockSpec(blk, lambda a, i, p: (1, a, i, 0)),
                  pl.BlockSpec(blk, lambda a, i, p: (2, a, i, 0))],
        out_specs=pl.BlockSpec((None, tr, c), lambda a, i, p: (a, p[1] * nr + i, 0)),
    )
    return pl.pallas_call(body, name=name, grid_spec=grid_spec, out_shape=jax.ShapeDtypeStruct((l, 2 * rh, c), F32),
                          compiler_params=_params(2))(place, part, recv1, recv2, recv2, recv2)


def share_halves(name, grads):
    n = len(grads)

    def body(*refs):
        outs = refs[n:2 * n]
        send_sems, recv_sems = refs[2 * n:]
        x, y, c, me, chips, chip_ids = _place()
        copies = []
        for t in range(n):
            rh = outs[t].shape[1] // 2
            mine = outs[t].at[:, pl.ds(c * rh, rh), :]
            cp = _remote(mine, mine, send_sems.at[t], recv_sems.at[t], (x, y, 1 - c))
            cp.start()
            copies.append(cp)
        for t, cp in enumerate(copies):
            rh = outs[t].shape[1] // 2
            theirs = outs[t].at[:, pl.ds((1 - c) * rh, rh), :]
            cp.wait_send()
            _remote(theirs, theirs, send_sems.at[t], recv_sems.at[t], (x, y, 1 - c)).wait_recv()

    any_spec = pl.BlockSpec(memory_space=pl.ANY)
    return pl.pallas_call(
        body, name=name, in_specs=[any_spec] * n, out_specs=[any_spec] * n,
        out_shape=[jax.ShapeDtypeStruct(a.shape, a.dtype) for a in grads],
        scratch_shapes=[pltpu.SemaphoreType.DMA((n,)), pltpu.SemaphoreType.DMA((n,))],
        input_output_aliases={t: t for t in range(n)},
        compiler_params=pltpu.CompilerParams(has_side_effects=True),
    )(*grads)


def all_reduce_small(name, packed):
    r = packed.shape[0]

    def body(x_ref, o_ref, buf, send_sems, recv_sems):
        x, y, c = lax.axis_index("x"), lax.axis_index("y"), lax.axis_index("c")
        me = 4 * x + 2 * y + c
        buf[me] = x_ref[...]
        copies = []
        for k in range(1, 8):
            to = (x ^ (k >> 2), y ^ ((k >> 1) & 1), c ^ (k & 1))
            cp = _remote(x_ref, buf.at[me], send_sems.at[k - 1], recv_sems.at[k - 1], to)
            cp.start()
            copies.append(cp)
        for k in range(1, 8):
            peer = me ^ k
            _remote(x_ref, buf.at[peer], send_sems.at[k - 1], recv_sems.at[k - 1], (x, y, c)).wait_recv()
        for cp in copies:
            cp.wait_send()
        acc = buf[0]
        for dev in range(1, 8):
            acc = acc + buf[dev]
        o_ref[...] = acc

    vmem = pl.BlockSpec(memory_space=pltpu.VMEM)
    return pl.pallas_call(
        body, name=name, in_specs=[vmem], out_specs=vmem, out_shape=jax.ShapeDtypeStruct(packed.shape, F32),
        scratch_shapes=[pltpu.VMEM((8, r, HEAD), F32), pltpu.SemaphoreType.DMA((7,)), pltpu.SemaphoreType.DMA((7,))],
    )(packed)


def adamw(name, w, g, m, v, *, tr=ROW_TILE):
    l, r, c = w.shape
    tr = _tile(r, tr, 8)

    def body(w_ref, g_ref, m_ref, v_ref, d_ref, nm_ref, nv_ref):
        gv = g_ref[...]
        nm = ADAM_B1 * m_ref[...] + (1.0 - ADAM_B1) * gv
        nv = ADAM_B2 * v_ref[...] + (1.0 - ADAM_B2) * (gv * gv)
        m_hat = nm / (1.0 - ADAM_B1 ** ADAM_STEP)
        v_hat = nv / (1.0 - ADAM_B2 ** ADAM_STEP)
        d_ref[...] = -ADAM_LR * (m_hat / (jnp.sqrt(v_hat) + ADAM_EPS) + ADAM_WD * w_ref[...])
        nm_ref[...] = nm
        nv_ref[...] = nv

    blk = pl.BlockSpec((None, tr, c), lambda i, j: (i, j, 0))
    shape = jax.ShapeDtypeStruct(w.shape, F32)
    return pl.pallas_call(body, name=name, grid=(l, r // tr), in_specs=[blk] * 4, out_specs=[blk] * 3,
                          out_shape=[shape] * 3, compiler_params=_params(2))(w, g, m, v)


SHARDED = ('ffn1_wg', 'ffn1_wu', 'ffn1_wd', 'ffn2_wg', 'ffn2_wu', 'ffn2_wd', 'a_wqkv', 'a_wo', 'b_wdkv', 'b_wkr', 'b_wuk',
           'b_wuv', 'b_wdq', 'b_wuq', 'b_wo')
COLUMN_SHARDED = ('ffn1_wg', 'ffn1_wu', 'ffn2_wg', 'ffn2_wu', 'a_wqkv')
GAINS = ('ffn_norm1', 'mix_norm', 'ffn_norm2', 'kv_norm', 'b_ckv_norm', 'b_cq_norm', 'final_norm')
WEIGHTS = ('ffn_norm1', 'ffn1_wg', 'ffn1_wu', 'ffn1_wd', 'mix_norm', 'ffn_norm2', 'ffn2_wg', 'ffn2_wu', 'ffn2_wd', 'a_wqkv',
           'a_wo', 'kv_norm', 'b_wdkv', 'b_ckv_norm', 'b_wkr', 'b_wuk', 'b_wuv', 'b_wdq', 'b_cq_norm', 'b_wuq', 'b_wo',
           'final_norm')


def _to_comm_shape(name, a):
    if name == 'b_wkr':
        return jnp.pad(a, ((0, 0), (0, HEAD - ROPE)))[None]
    if name == 'b_wuq':
        l, r, h, _ = a.shape
        return jnp.pad(a, ((0, 0), (0, 0), (0, 0), (0, QK_PAD - HEAD - ROPE))).reshape(l, r, h * QK_PAD)
    if name in ('b_wuk', 'b_wuv'):
        return a.reshape(1, a.shape[0], -1)
    return a[None] if a.ndim == 2 else a


def _from_comm_shape(name, g, like):
    if name == 'b_wkr':
        return g[0, :, :ROPE]
    if name == 'b_wuq':
        l, r, h, e = like.shape
        return g.reshape(l, r, h, QK_PAD)[..., :e]
    return g.reshape(like.shape)


def _as3d(a):
    if a.ndim == 1:
        return a.reshape(1, 1, -1)
    if a.ndim == 2:
        return a[None]
    return a.reshape(a.shape[0], a.shape[1], -1)


def kernel(x, ffn_norm1, ffn1_wg, ffn1_wu, ffn1_wd, mix_norm, ffn_norm2, ffn2_wg, ffn2_wu, ffn2_wd, a_wqkv, a_wo, kv_norm, b_wdkv, b_ckv_norm, b_wkr, b_wuk, b_wuv, b_wdq, b_cq_norm, b_wuq, b_wo, final_norm, loss_target, m_ffn_norm1, m_ffn1_wg, m_ffn1_wu, m_ffn1_wd, m_mix_norm, m_ffn_norm2, m_ffn2_wg, m_ffn2_wu, m_ffn2_wd, m_a_wqkv, m_a_wo, m_kv_norm, m_b_wdkv, m_b_ckv_norm, m_b_wkr, m_b_wuk, m_b_wuv, m_b_wdq, m_b_cq_norm, m_b_wuq, m_b_wo, m_final_norm, v_ffn_norm1, v_ffn1_wg, v_ffn1_wu, v_ffn1_wd, v_mix_norm, v_ffn_norm2, v_ffn2_wg, v_ffn2_wu, v_ffn2_wd, v_a_wqkv, v_a_wo, v_kv_norm, v_b_wdkv, v_b_ckv_norm, v_b_wkr, v_b_wuk, v_b_wuv, v_b_wdq, v_b_cq_norm, v_b_wuq, v_b_wo, v_final_norm):
    args = dict(locals())
    weights = {k: args[k] for k in WEIGHTS}
    place = jnp.stack([2 * lax.axis_index("x") + lax.axis_index("y"), lax.axis_index("c")]).astype(jnp.int32)

    slabs = [cast_to_slab(f'cast_{k}', _to_comm_shape(k, weights[k]), place) for k in SHARDED]
    gathered = all_gather_weights('gather_weights', slabs)
    w = {}
    for k, g in zip(SHARDED, gathered):
        l, j, r, c = g.shape
        w[k] = g if k in COLUMN_SHARDED else g.reshape(l, j * r, c)
    norms = {k: weights[k] if weights[k].ndim == 2 else weights[k][None] for k in GAINS}

    loss, grad_x, gw, gn = local_step(x[0], loss_target[0], w, norms)
    loss = lax.psum(loss[0, 0], ("x", "y", "c"))

    parts = [gw[k].reshape(g.shape) for k, g in zip(SHARDED, gathered)]
    recv1 = exchange_halves('grads_to_sibling', parts)
    sums = [add_halves(f'add_{k}', p, r1, place) for k, p, r1 in zip(SHARDED, parts, recv1)]
    recv2 = scatter_to_chips('grads_to_chips', sums)
    halves = [reduce_own(f'reduce_{k}', p, r1, r2, place) for k, p, r1, r2 in zip(SHARDED, parts, recv1, recv2)]
    reduced = share_halves('grads_share', halves)
    grads = {k: _from_comm_shape(k, g, weights[k]) for k, g in zip(SHARDED, reduced)}

    flat = jnp.concatenate([gn[k].reshape(-1) for k in GAINS])
    rows = -(-flat.shape[0] // (8 * HEAD)) * 8
    packed = jnp.pad(flat, (0, rows * HEAD - flat.shape[0])).reshape(rows, HEAD)
    total = all_reduce_small('gain_grads', packed).reshape(-1)
    off = 0
    for k in GAINS:
        grads[k] = total[off:off + weights[k].size].reshape(weights[k].shape)
        off += weights[k].size

    deltas, new_m, new_v = {}, {}, {}
    for k in WEIGHTS:
        shape = weights[k].shape
        d, nm, nv = adamw(f'adamw_{k}', _as3d(weights[k]), _as3d(grads[k]), _as3d(args['m_' + k]), _as3d(args['v_' + k]))
        deltas[k], new_m[k], new_v[k] = d.reshape(shape), nm.reshape(shape), nv.reshape(shape)
    return (loss, grad_x[None], *[grads[k] for k in WEIGHTS], *[deltas[k] for k in WEIGHTS],
            *[new_m[k] for k in WEIGHTS], *[new_v[k] for k in WEIGHTS])
```
